```python
import math
import jax, jax.numpy as jnp
from jax import lax
import numpy as np

D_MODEL = 1024
BATCH = 8
SEQ = 4096
DEPTH = 4
DEC_BATCH = 2
DEC_SEQ = 16384
PAST_LEN = 128

EXPAND = 2
D_INNER = EXPAND * D_MODEL
MIXER_ORDER = ("hyena", "swa", "mlstm")
N_MIXERS = 3
RMS_EPS = 1e-6
HY_EMB = 33
HY_FILT = 64
HY_TARGET = 1e-2
HY_FAST = 0.3
HY_SLOW = 1.5
HY_MIN_DECAY = math.log(HY_TARGET) / HY_SLOW
HY_MAX_DECAY = math.log(HY_TARGET) / HY_FAST
HEAD_DIM = 128
N_Q_HEADS = D_INNER // HEAD_DIM
N_KV_HEADS = 4
GROUP = N_Q_HEADS // N_KV_HEADS
WINDOW = 128
BLOCK = 128
ROT_DIM = HEAD_DIM // 4
ROPE_THETA = 500000.0
Q_DIM = N_Q_HEADS * HEAD_DIM
KV_DIM = N_KV_HEADS * HEAD_DIM
ML_HEADS = 4
ML_HEAD_DIM = D_INNER // ML_HEADS
ML_QKV_BLOCK = 4
ML_CHUNK = 128
LN_EPS = 1e-6

kernel_name = "hybrid_hyena_swa_mlstm_encoder"


def _rmsnorm(x, w):
    xf = x.astype(jnp.float32)
    inv = lax.rsqrt(jnp.mean(xf * xf, axis=-1, keepdims=True) + RMS_EPS)
    return (xf * inv).astype(x.dtype) * w


def _centred_dwconv3(u, w, b):
    L = u.shape[1]
    up = jnp.pad(u, ((0, 0), (1, 1), (0, 0)))
    return up[:, :L] * w[0] + up[:, 1:L + 1] * w[1] + up[:, 2:] * w[2] + b


def _hyena_filters(L, w1, b1, w2, b2, w3, b3, freq, wout):
    f32 = jnp.float32
    t = jnp.linspace(0.0, 1.0, L, dtype=f32)[:, None]
    bands = (HY_EMB - 1) // 2
    ang = 2.0 * math.pi * jnp.arange(L, dtype=f32)[:, None] / L
    fb = jnp.linspace(1e-4, bands - 1, bands, dtype=f32)[None, :]
    feat = jnp.concatenate([t, jnp.cos(fb * ang), -jnp.sin(fb * ang)], axis=-1)
    fr = freq.astype(f32)
    h = jnp.sin(fr * (feat @ w1.astype(f32) + b1.astype(f32)))
    h = jnp.sin(fr * (h @ w2.astype(f32) + b2.astype(f32)))
    h = jnp.sin(fr * (h @ w3.astype(f32) + b3.astype(f32)))
    h = h @ wout.astype(f32)
    deltas = jnp.abs(jnp.linspace(HY_MIN_DECAY, HY_MAX_DECAY, D_INNER, dtype=f32))
    decay = jnp.exp(-t * deltas)
    return h.reshape(L, 4, D_INNER) * decay[:, None, :]


def _two_sided(h_fwd, h_bwd):
    return jnp.concatenate([h_fwd, jnp.zeros_like(h_fwd[:1]), h_bwd[:0:-1]], axis=0)


def _fftconv(u, k2, d):
    L = u.shape[1]
    uf = jnp.fft.rfft(u, n=2 * L, axis=1)
    kf = jnp.fft.rfft(k2, n=2 * L, axis=0)
    y = jnp.fft.irfft(uf * kf[None], n=2 * L, axis=1)[:, :L]
    return y + u * d.astype(jnp.float32)


def _hyena_mixer(xn, w_in, conv_w, conv_b, fw1, fb1, fw2, fb2, fw3, fb3, ffreq, fwout, bias_d, w_out):
    B, L, _ = xn.shape
    proj = xn @ w_in
    streams, gate = proj[..., :3 * D_INNER], proj[..., 3 * D_INNER:]
    streams = _centred_dwconv3(streams, conv_w, conv_b).astype(jnp.float32)
    v, x1, x2 = jnp.split(streams, 3, axis=-1)
    h = _hyena_filters(L, fw1, fb1, fw2, fb2, fw3, fb3, ffreq, fwout)
    k1 = _two_sided(h[:, 0], h[:, 2])
    k2 = _two_sided(h[:, 1], h[:, 3])
    z = x1 * _fftconv(v, k1, bias_d[0])
    z = x2 * _fftconv(z, k2, bias_d[1])
    return (z.astype(xn.dtype) * jax.nn.silu(gate)) @ w_out


def _rope_partial(x, pos):
    half = ROT_DIM // 2
    inv = ROPE_THETA ** (-jnp.arange(half, dtype=jnp.float32) / half)
    ang = pos[:, None] * inv[None, :]
    cos = jnp.cos(ang)[None, :, None, :]
    sin = jnp.sin(ang)[None, :, None, :]
    xr = x[..., :ROT_DIM].astype(jnp.float32)
    a, b = xr[..., :half], xr[..., half:]
    rot = jnp.concatenate([a * cos - b * sin, b * cos + a * sin], axis=-1).astype(x.dtype)
    return jnp.concatenate([rot, x[..., ROT_DIM:]], axis=-1)


def _swa_mixer(xn, w_in, sink, w_out):
    B, L, _ = xn.shape
    nb = L // BLOCK
    proj = xn @ w_in
    q, k, v, gate = jnp.split(proj, [Q_DIM, Q_DIM + KV_DIM, Q_DIM + 2 * KV_DIM], axis=-1)
    pos = jnp.arange(L, dtype=jnp.float32)
    q = _rope_partial(q.reshape(B, L, N_Q_HEADS, HEAD_DIM), pos)
    k = _rope_partial(k.reshape(B, L, N_KV_HEADS, HEAD_DIM), pos)
    v = v.reshape(B, L, N_KV_HEADS, HEAD_DIM)
    qb = q.reshape(B, nb, BLOCK, N_KV_HEADS, GROUP, HEAD_DIM)

    def neighbours(t):
        tp = jnp.pad(t, ((0, 0), (BLOCK, BLOCK), (0, 0), (0, 0))).reshape(B, nb + 2, BLOCK, N_KV_HEADS, HEAD_DIM)
        return jnp.concatenate([tp[:, :-2], tp[:, 1:-1], tp[:, 2:]], axis=2)

    kb, vb = neighbours(k), neighbours(v)
    s = jnp.einsum('bnqhgd,bnkhd->bnhgqk', qb, kb, preferred_element_type=jnp.float32) * (HEAD_DIM ** -0.5)
    qi = jnp.arange(BLOCK)[:, None]
    kj = jnp.arange(3 * BLOCK)[None, :] - BLOCK
    key_abs = jnp.arange(nb)[:, None, None] * BLOCK + kj[None]
    mask = (jnp.abs(qi - kj) <= WINDOW)[None] & (key_abs >= 0) & (key_abs < L)
    s = jnp.where(mask[None, :, None, None], s, -jnp.inf)
    sink_l = sink.astype(jnp.float32).reshape(N_KV_HEADS, GROUP)[None, None, :, :, None, None]
    m = jnp.maximum(s.max(axis=-1, keepdims=True), sink_l)
    p = jnp.exp(s - m)
    p = p / (p.sum(axis=-1, keepdims=True) + jnp.exp(sink_l - m))
    o = jnp.einsum('bnhgqk,bnkhd->bnqhgd', p.astype(vb.dtype), vb).reshape(B, L, D_INNER)
    return (o * jax.nn.silu(gate)) @ w_out


def _blockdiag(x, w):
    B, L, _ = x.shape
    xb = x.reshape(B, L, D_INNER // ML_QKV_BLOCK, ML_QKV_BLOCK)
    return jnp.einsum('blni,nio->blno', xb, w).reshape(B, L, D_INNER)


def _mlstm_scan(q, k, v, ig, lf):
    B, H, L, dh = q.shape
    nc = L // ML_CHUNK

    def chunks(t):
        return jnp.moveaxis(t.reshape(B, H, nc, ML_CHUNK, *t.shape[3:]), 2, 0)

    tril = jnp.tril(jnp.ones((ML_CHUNK, ML_CHUNK), dtype=bool))

    def step(carry, xs):
        C, n, m = carry
        qc, kc, vc, ic, fc = xs
        b = jnp.cumsum(fc, axis=-1)
        dmat = jnp.where(tril, b[..., :, None] - b[..., None, :] + ic[..., None, :], -jnp.inf)
        inter = b + m[..., None]
        m_out = jnp.maximum(inter, dmat.max(axis=-1))
        w_intra = jnp.exp(dmat - m_out[..., None])
        w_state = jnp.exp(inter - m_out)
        s = jnp.einsum('bhtd,bhsd->bhts', qc, kc) * w_intra
        num = jnp.einsum('bhts,bhsd->bhtd', s, vc) + w_state[..., None] * jnp.einsum('bhed,bhtd->bhte', C, qc)
        den = s.sum(axis=-1) + w_state * jnp.einsum('bhd,bhtd->bht', n, qc)
        h = num / jnp.maximum(jnp.abs(den), jnp.exp(-m_out))[..., None]
        bl = b[..., -1]
        m_new = m_out[..., -1]
        wk = jnp.exp(bl[..., None] - b + ic - m_new[..., None])
        carry_decay = jnp.exp(bl + m - m_new)
        C = carry_decay[..., None, None] * C + jnp.einsum('bhse,bhsd->bhed', vc * wk[..., None], kc)
        n = carry_decay[..., None] * n + jnp.einsum('bhs,bhsd->bhd', wk, kc)
        return (C, n, m_new), h

    init = (jnp.zeros((B, H, dh, dh), jnp.float32), jnp.zeros((B, H, dh), jnp.float32), jnp.zeros((B, H), jnp.float32))
    _, hs = lax.scan(step, init, (chunks(q), chunks(k), chunks(v), chunks(ig), chunks(lf)))
    return jnp.moveaxis(hs, 0, 2).reshape(B, H, L, dh)


def _mlstm_mixer(xn, w_in, conv_w, conv_b, wq, wk, wv, w_gate, b_gate, head_norm, skip, w_out):
    B, L, _ = xn.shape
    proj = xn @ w_in
    xi, z = jnp.split(proj, 2, axis=-1)
    xc = jax.nn.silu(_centred_dwconv3(xi, conv_w, conv_b))
    q, k, v = _blockdiag(xc, wq), _blockdiag(xc, wk), _blockdiag(xi, wv)
    gates = (jnp.concatenate([q, k, v], axis=-1) @ w_gate + b_gate).astype(jnp.float32)

    def heads(t):
        return t.astype(jnp.float32).reshape(B, L, ML_HEADS, ML_HEAD_DIM).transpose(0, 2, 1, 3)

    qh, kh, vh = heads(q), heads(k) * (ML_HEAD_DIM ** -0.5), heads(v)
    i_f, f_f, i_b, f_b = jnp.split(gates.transpose(0, 2, 1), 4, axis=1)
    h_fwd = _mlstm_scan(qh, kh, vh, i_f, jax.nn.log_sigmoid(f_f))
    fl = lambda t: jnp.flip(t, axis=2)
    h_bwd = fl(_mlstm_scan(fl(qh), fl(kh), fl(vh), fl(i_b), fl(jax.nn.log_sigmoid(f_b))))
    h = h_fwd + h_bwd
    mu = h.mean(axis=-1, keepdims=True)
    var = jnp.mean(jnp.square(h - mu), axis=-1, keepdims=True)
    h = ((h - mu) * lax.rsqrt(var + LN_EPS)).transpose(0, 2, 1, 3).reshape(B, L, D_INNER).astype(xn.dtype) * head_norm
    return ((h + skip * xc) * jax.nn.silu(z)) @ w_out


def _trunk(x, layers, final_norm):
    for i in range(DEPTH):
        kind = MIXER_ORDER[i % N_MIXERS]
        norm_w, params = layers[i]
        xn = _rmsnorm(x, norm_w)
        if kind == "hyena":
            y = _hyena_mixer(xn, *params)
        elif kind == "swa":
            y = _swa_mixer(xn, *params)
        else:
            y = _mlstm_mixer(xn, *params)
        x = x + y.astype(x.dtype)
    return _rmsnorm(x, final_norm)


def _normal(key, shape, scale):
    return jax.random.normal(key, shape, jnp.float32) * scale


def _hyena_params(keys, p):
    E = D_INNER
    return {
        p + "norm": 1.0 + _normal(next(keys), (D_MODEL,), 0.01),
        p + "hy_w_in": _normal(next(keys), (D_MODEL, 4 * E), D_MODEL ** -0.5),
        p + "hy_conv_w": _normal(next(keys), (3, 3 * E), 3 ** -0.5),
        p + "hy_conv_b": _normal(next(keys), (3 * E,), 0.01),
        p + "hy_filt_w1": _normal(next(keys), (HY_EMB, HY_FILT), HY_EMB ** -0.5),
        p + "hy_filt_b1": _normal(next(keys), (HY_FILT,), 0.1),
        p + "hy_filt_w2": _normal(next(keys), (HY_FILT, HY_FILT), HY_FILT ** -0.5),
        p + "hy_filt_b2": _normal(next(keys), (HY_FILT,), 0.1),
        p + "hy_filt_w3": _normal(next(keys), (HY_FILT, HY_FILT), HY_FILT ** -0.5),
        p + "hy_filt_b3": _normal(next(keys), (HY_FILT,), 0.1),
        p + "hy_filt_freq": 1.0 + _normal(next(keys), (HY_FILT,), 0.01),
        p + "hy_filt_wout": _normal(next(keys), (HY_FILT, 4 * E), 0.005),
        p + "hy_bias_d": _normal(next(keys), (2, E), 0.5),
        p + "hy_w_out": _normal(next(keys), (E, D_MODEL), E ** -0.5),
    }


def setup_inputs(seed: int = 0) -> dict:
    key = jax.random.key(seed)
    keys = iter(jax.random.split(key, 64))
    E = D_INNER
    inp = {
        "x_prompt": jax.random.normal(next(keys), (BATCH, SEQ, D_MODEL), jnp.float32),
        "x_sample": jax.random.normal(next(keys), (DEC_BATCH, DEC_SEQ, D_MODEL), jnp.float32),
    }
    inp.update(_hyena_params(keys, "l0_"))
    inp.update({
        "l1_norm": 1.0 + _normal(next(keys), (D_MODEL,), 0.01),
        "l1_swa_w_in": _normal(next(keys), (D_MODEL, 2 * Q_DIM + 2 * KV_DIM), D_MODEL ** -0.5),
        "l1_swa_sink": _normal(next(keys), (N_Q_HEADS,), 0.5),
        "l1_swa_w_out": _normal(next(keys), (E, D_MODEL), E ** -0.5),
    })
    f_bias = jnp.linspace(3.0, 6.0, ML_HEADS, dtype=jnp.float32)
    b_gate = jnp.concatenate([_normal(next(keys), (ML_HEADS,), 0.1), f_bias + _normal(next(keys), (ML_HEADS,), 0.01),
                              _normal(next(keys), (ML_HEADS,), 0.1), f_bias + _normal(next(keys), (ML_HEADS,), 0.01)])
    nblk = E // ML_QKV_BLOCK
    inp.update({
        "l2_norm": 1.0 + _normal(next(keys), (D_MODEL,), 0.01),
        "l2_ml_w_in": _normal(next(keys), (D_MODEL, 2 * E), D_MODEL ** -0.5),
        "l2_ml_conv_w": _normal(next(keys), (3, E), 3 ** -0.5),
        "l2_ml_conv_b": _normal(next(keys), (E,), 0.01),
        "l2_ml_wq": _normal(next(keys), (nblk, ML_QKV_BLOCK, ML_QKV_BLOCK), ML_QKV_BLOCK ** -0.5),
        "l2_ml_wk": _normal(next(keys), (nblk, ML_QKV_BLOCK, ML_QKV_BLOCK), ML_QKV_BLOCK ** -0.5),
        "l2_ml_wv": _normal(next(keys), (nblk, ML_QKV_BLOCK, ML_QKV_BLOCK), ML_QKV_BLOCK ** -0.5),
        "l2_ml_w_gate": _normal(next(keys), (3 * E, 4 * ML_HEADS), 0.01),
        "l2_ml_b_gate": b_gate,
        "l2_ml_head_norm": 1.0 + _normal(next(keys), (E,), 0.01),
        "l2_ml_skip": 1.0 + _normal(next(keys), (E,), 0.01),
        "l2_ml_w_out": _normal(next(keys), (E, D_MODEL), E ** -0.5),
    })
    inp.update(_hyena_params(keys, "l3_"))
    inp["final_norm"] = 1.0 + _normal(next(keys), (D_MODEL,), 0.01)
    return inp


def reference(x_prompt, x_sample,
              l0_norm, l0_hy_w_in, l0_hy_conv_w, l0_hy_conv_b, l0_hy_filt_w1, l0_hy_filt_b1, l0_hy_filt_w2, l0_hy_filt_b2,
              l0_hy_filt_w3, l0_hy_filt_b3, l0_hy_filt_freq, l0_hy_filt_wout, l0_hy_bias_d, l0_hy_w_out,
              l1_norm, l1_swa_w_in, l1_swa_sink, l1_swa_w_out,
              l2_norm, l2_ml_w_in, l2_ml_conv_w, l2_ml_conv_b, l2_ml_wq, l2_ml_wk, l2_ml_wv, l2_ml_w_gate, l2_ml_b_gate,
              l2_ml_head_norm, l2_ml_skip, l2_ml_w_out,
              l3_norm, l3_hy_w_in, l3_hy_conv_w, l3_hy_conv_b, l3_hy_filt_w1, l3_hy_filt_b1, l3_hy_filt_w2, l3_hy_filt_b2,
              l3_hy_filt_w3, l3_hy_filt_b3, l3_hy_filt_freq, l3_hy_filt_wout, l3_hy_bias_d, l3_hy_w_out,
              final_norm):
    layers = [
        (l0_norm, (l0_hy_w_in, l0_hy_conv_w, l0_hy_conv_b, l0_hy_filt_w1, l0_hy_filt_b1, l0_hy_filt_w2, l0_hy_filt_b2,
                   l0_hy_filt_w3, l0_hy_filt_b3, l0_hy_filt_freq, l0_hy_filt_wout, l0_hy_bias_d, l0_hy_w_out)),
        (l1_norm, (l1_swa_w_in, l1_swa_sink, l1_swa_w_out)),
        (l2_norm, (l2_ml_w_in, l2_ml_conv_w, l2_ml_conv_b, l2_ml_wq, l2_ml_wk, l2_ml_wv, l2_ml_w_gate, l2_ml_b_gate,
                   l2_ml_head_norm, l2_ml_skip, l2_ml_w_out)),
        (l3_norm, (l3_hy_w_in, l3_hy_conv_w, l3_hy_conv_b, l3_hy_filt_w1, l3_hy_filt_b1, l3_hy_filt_w2, l3_hy_filt_b2,
                   l3_hy_filt_w3, l3_hy_filt_b3, l3_hy_filt_freq, l3_hy_filt_wout, l3_hy_bias_d, l3_hy_w_out)),
    ]
    y_prompt = _trunk(x_prompt, layers, final_norm)
    y_sample = _trunk(x_sample, layers, final_norm)
    return (y_prompt, y_sample)
```

```python
import cmath
import functools
import math

import jax
import jax.numpy as jnp
import numpy as np
from jax import lax
from jax.experimental import pallas as pl
from jax.experimental.pallas import tpu as pltpu

F32 = jnp.float32
BF16 = jnp.bfloat16

D_MODEL = 1024
D_INNER = 2048
RMS_EPS = 1e-6
LN_EPS = 1e-6

LANES = 128
SUBLANES = 8
TILE_CH = SUBLANES * LANES
N_GROUPS = D_INNER // TILE_CH
VMEM_LIMIT = 56 * 1024 * 1024

HY_EMB = 33
HY_FILT = 64
HY_MIN_DECAY = math.log(1e-2) / 1.5
HY_MAX_DECAY = math.log(1e-2) / 0.3
FFT_N2 = 128
FFT_N2_RADICES = (8, 4, 4)

HEAD_DIM = 128
N_Q_HEADS = D_INNER // HEAD_DIM
N_KV_HEADS = 4
GROUP = N_Q_HEADS // N_KV_HEADS
WINDOW = 128
BLOCK = 128
ROT_DIM = HEAD_DIM // 4
ROPE_THETA = 500000.0
Q_DIM = N_Q_HEADS * HEAD_DIM
KV_DIM = N_KV_HEADS * HEAD_DIM

ML_HEADS = 4
ML_HEAD_DIM = D_INNER // ML_HEADS
ML_QKV_BLOCK = 4
ML_CHUNK = 128


def _cparams(sem):
    return pltpu.CompilerParams(dimension_semantics=sem, vmem_limit_bytes=VMEM_LIMIT)


def _silu(x):
    return x * (1.0 / (1.0 + jnp.exp(-x)))


def _add(a, b):
    if a is None:
        return b
    if b is None:
        return a
    return a + b


def _sub(a, b):
    if b is None:
        return a
    if a is None:
        return -b
    return a - b


def _scale(a, c):
    if a is None or c == 0.0:
        return None
    if c == 1.0:
        return a
    if c == -1.0:
        return -a
    return a * c


def _cadd(x, y):
    return (_add(x[0], y[0]), _add(x[1], y[1]))


def _csub(x, y):
    return (_sub(x[0], y[0]), _sub(x[1], y[1]))


def _snap(v):
    for t in (0.0, 1.0, -1.0):
        if abs(v - t) < 1e-14:
            return t
    return v


def _cmul_const(x, c):
    cr, ci = _snap(c.real), _snap(c.imag)
    re = _sub(_scale(x[0], cr), _scale(x[1], ci))
    im = _add(_scale(x[0], ci), _scale(x[1], cr))
    return (re, im)


def _cmul(x, y):
    xr, xi = x
    yr, yi = y
    if xi is None:
        return (xr * yr, xr * yi)
    return (xr * yr - xi * yi, xr * yi + xi * yr)


def _dense(x):
    re, im = x
    if re is None:
        re = jnp.zeros_like(im)
    if im is None:
        im = jnp.zeros_like(re)
    return (re, im)


def _small_dft(xs, sign):
    r = len(xs)
    if r == 1:
        return list(xs)
    ev = _small_dft(xs[0::2], sign)
    od = _small_dft(xs[1::2], sign)
    out = [None] * r
    for k in range(r // 2):
        t = _cmul_const(od[k], cmath.exp(sign * 2j * math.pi * k / r))
        out[k] = _cadd(ev[k], t)
        out[k + r // 2] = _csub(ev[k], t)
    return out


def _fft_plan(n, radices, sign):
    assert math.prod(radices) == n
    passes = []
    s, ncur = 1, n
    for r in radices:
        m = ncur // r
        bfs = []
        for p in range(m):
            tws = [cmath.exp(sign * 2j * math.pi * p * k / ncur) for k in range(r)]
            for q in range(s):
                ins = [q + s * (p + m * j) for j in range(r)]
                outs = [q + s * (r * p + k) for k in range(r)]
                bfs.append((ins, outs, tws))
        passes.append(bfs)
        ncur, s = m, s * r
    return passes


def _run_fft(plan, sign, load, store, temps):
    npass = len(plan)
    for pi, bfs in enumerate(plan):
        if pi == 0:
            ld = load
        else:
            tr, ti = temps[(pi - 1) % 2]
            ld = lambda i, tr=tr, ti=ti: (tr[i], ti[i])
        if pi == npass - 1:
            st = store
        else:
            tr2, ti2 = temps[pi % 2]

            def st(i, x, tr2=tr2, ti2=ti2):
                tr2[i] = x[0]
                ti2[i] = x[1]
        for ins, outs, tws in bfs:
            ys = _small_dft([ld(i) for i in ins], sign)
            for o, y, tw in zip(outs, ys, tws):
                st(o, _dense(_cmul_const(y, tw)))


def _radices(n):
    out = []
    while n > 1:
        r = 8 if n % 8 == 0 else (4 if n % 4 == 0 else 2)
        out.append(r)
        n //= r
    return tuple(out)


def _norm_matmul_kernel(x_ref, nw_ref, w_ref, o_ref, xn_ref):
    @pl.when(pl.program_id(1) == 0)
    def _():
        x = x_ref[...]
        inv = lax.rsqrt(jnp.mean(x * x, axis=-1, keepdims=True) + RMS_EPS)
        xn_ref[...] = ((x * inv) * nw_ref[...]).astype(BF16)

    o_ref[...] = jnp.dot(xn_ref[...], w_ref[...], preferred_element_type=F32)


def _norm_matmul(x2d, norm_w, w, tm=1024, tn=1024):
    t, d = x2d.shape
    n = w.shape[1]
    tm = min(tm, t)
    tn = min(tn, n)
    return pl.pallas_call(
        _norm_matmul_kernel,
        grid=(t // tm, n // tn),
        in_specs=[
            pl.BlockSpec((tm, d), lambda i, j: (i, 0)),
            pl.BlockSpec((1, d), lambda i, j: (0, 0)),
            pl.BlockSpec((d, tn), lambda i, j: (0, j)),
        ],
        out_specs=pl.BlockSpec((tm, tn), lambda i, j: (i, j)),
        out_shape=jax.ShapeDtypeStruct((t, n), F32),
        scratch_shapes=[pltpu.VMEM((tm, d), BF16)],
        compiler_params=_cparams(("arbitrary", "arbitrary")),
        name="norm_matmul",
    )(x2d, norm_w.reshape(1, d), w.astype(BF16))


def _out_kernel(*refs, mode, final):
    if mode == "mlstm":
        hf_ref, hb_ref, xc_ref, g_ref, hn_ref, sk_ref, w_ref, r_ref = refs[:8]
        rest = refs[8:]
        h = hf_ref[...] + hb_ref[...]
        parts = []
        for hd in range(ML_HEADS):
            seg = h[:, hd * ML_HEAD_DIM:(hd + 1) * ML_HEAD_DIM]
            mu = jnp.mean(seg, axis=-1, keepdims=True)
            cen = seg - mu
            var = jnp.mean(cen * cen, axis=-1, keepdims=True)
            parts.append(cen * lax.rsqrt(var + LN_EPS))
        a = jnp.concatenate(parts, axis=-1) * hn_ref[...] + sk_ref[...] * xc_ref[...]
    else:
        a_ref, g_ref, w_ref, r_ref = refs[:4]
        rest = refs[4:]
        a = a_ref[...]
    if final:
        fn_ref, o_ref = rest
    else:
        (o_ref,) = rest
    y = jnp.dot((a * _silu(g_ref[...])).astype(BF16), w_ref[...], preferred_element_type=F32)
    x = r_ref[...] + y
    if final:
        inv = lax.rsqrt(jnp.mean(x * x, axis=-1, keepdims=True) + RMS_EPS)
        x = (x * inv) * fn_ref[...]
    o_ref[...] = x


def _out_matmul(acts, gate_arr, gate_blk, w, resid, final_norm=None, extra=(), tm=256):
    t = resid.shape[0]
    tm = min(tm, t)
    e = D_INNER
    mode = "mlstm" if len(acts) == 3 else "plain"
    in_specs = [pl.BlockSpec((tm, e), lambda i: (i, 0)) for _ in acts]
    in_specs.append(pl.BlockSpec((tm, e), lambda i: (i, gate_blk)))
    args = list(acts) + [gate_arr]
    for v in extra:
        in_specs.append(pl.BlockSpec((1, e), lambda i: (0, 0)))
        args.append(v.reshape(1, e))
    in_specs.append(pl.BlockSpec((e, D_MODEL), lambda i: (0, 0)))
    args.append(w.astype(BF16))
    in_specs.append(pl.BlockSpec((tm, D_MODEL), lambda i: (i, 0)))
    args.append(resid)
    if final_norm is not None:
        in_specs.append(pl.BlockSpec((1, D_MODEL), lambda i: (0, 0)))
        args.append(final_norm.reshape(1, D_MODEL))
    return pl.pallas_call(
        functools.partial(_out_kernel, mode=mode, final=final_norm is not None),
        grid=(t // tm,),
        in_specs=in_specs,
        out_specs=pl.BlockSpec((tm, D_MODEL), lambda i: (i, 0)),
        out_shape=jax.ShapeDtypeStruct((t, D_MODEL), F32),
        compiler_params=_cparams(("arbitrary",)),
        name="out_matmul_" + mode,
    )(*args)


def _dwconv_kernel(x_ref, p_ref, n_ref, w_ref, b_ref, o_ref, *, tl, nblk, act):
    i = pl.program_id(1)
    x = x_ref[0, :, 0]
    prev = jnp.where(i > 0, p_ref[0, :, 0], 0.0)
    nxt = jnp.where(i < nblk - 1, n_ref[0, :, 0], 0.0)
    xp = jnp.concatenate([prev, x[:-1]], axis=0)
    xn = jnp.concatenate([x[1:], nxt], axis=0)
    y = xp * w_ref[0, 0] + x * w_ref[1, 0] + xn * w_ref[2, 0] + b_ref[0]
    if act:
        y = _silu(y)
    o_ref[0, :, 0] = y


def _dwconv3(proj, n_in_groups, n_out_groups, conv_w, conv_b, act, tl=256):
    b, l, _ = proj.shape
    tl = min(tl, l)
    nblk = l // tl
    x5 = proj.reshape(b, l, n_in_groups, SUBLANES, LANES)
    w4 = conv_w.reshape(3, n_out_groups, SUBLANES, LANES)
    b3 = conv_b.reshape(n_out_groups, SUBLANES, LANES)
    return pl.pallas_call(
        functools.partial(_dwconv_kernel, tl=tl, nblk=nblk, act=act),
        grid=(b, nblk, n_out_groups),
        in_specs=[
            pl.BlockSpec((1, tl, 1, SUBLANES, LANES), lambda bb, i, g: (bb, i, g, 0, 0)),
            pl.BlockSpec((1, 1, 1, SUBLANES, LANES),
                         lambda bb, i, g: (bb, jnp.maximum(i * tl - 1, 0), g, 0, 0)),
            pl.BlockSpec((1, 1, 1, SUBLANES, LANES),
                         lambda bb, i, g: (bb, jnp.minimum((i + 1) * tl, l - 1), g, 0, 0)),
            pl.BlockSpec((3, 1, SUBLANES, LANES), lambda bb, i, g: (0, g, 0, 0)),
            pl.BlockSpec((1, SUBLANES, LANES), lambda bb, i, g: (g, 0, 0)),
        ],
        out_specs=pl.BlockSpec((1, tl, 1, SUBLANES, LANES), lambda bb, i, g: (bb, i, g, 0, 0)),
        out_shape=jax.ShapeDtypeStruct((b, l, n_out_groups, SUBLANES, LANES), F32),
        compiler_params=_cparams(("arbitrary", "arbitrary", "arbitrary")),
        name="dwconv3",
    )(x5, x5, x5, w4, b3)


def _filter_kernel(feat_ref, t_ref, w1_ref, b1_ref, w2_ref, b2_ref, w3_ref, b3_ref, fr_ref,
                   wo_ref, dl_ref, o_ref):
    hp = lax.Precision.HIGHEST
    fr = fr_ref[...]
    h = jnp.sin(fr * (jnp.dot(feat_ref[...], w1_ref[...], precision=hp,
                              preferred_element_type=F32) + b1_ref[...]))
    h = jnp.sin(fr * (jnp.dot(h, w2_ref[...], precision=hp, preferred_element_type=F32) + b2_ref[...]))
    h = jnp.sin(fr * (jnp.dot(h, w3_ref[...], precision=hp, preferred_element_type=F32) + b3_ref[...]))
    k = jnp.dot(h, wo_ref[0], precision=hp, preferred_element_type=F32)
    tcol = t_ref[...]
    decay = jnp.exp(-tcol[:, 0:1] * dl_ref[...])
    o_ref[0] = k * decay * tcol[:, 1:2]


def _hyena_two_sided_filters(l, fw1, fb1, fw2, fb2, fw3, fb3, ffreq, fwout, tr=256):
    n = 2 * l
    e = D_INNER
    tr = min(tr, l)
    idx = jnp.arange(n)
    src = jnp.where(idx < l, idx, n - idx)
    valid = (idx != l).astype(F32)
    src = jnp.where(idx == l, 0, src)
    t_all = jnp.linspace(0.0, 1.0, l, dtype=F32)
    bands = (HY_EMB - 1) // 2
    ang = 2.0 * math.pi * jnp.arange(l, dtype=F32)[:, None] / l
    fb = jnp.linspace(1e-4, bands - 1, bands, dtype=F32)[None, :]
    feat = jnp.concatenate([t_all[:, None], jnp.cos(fb * ang), -jnp.sin(fb * ang)], axis=-1)
    feat = jnp.pad(feat, ((0, 0), (0, LANES - HY_EMB)))[src]
    tcol = jnp.stack([t_all[src], valid], axis=-1)
    w1p = jnp.pad(fw1, ((0, LANES - HY_EMB), (0, 0)))
    deltas = jnp.abs(jnp.linspace(HY_MIN_DECAY, HY_MAX_DECAY, e, dtype=F32)).reshape(1, e)
    wo = fwout.reshape(HY_FILT, 4, e).transpose(1, 0, 2)
    nhalf = l // tr
    row = lambda v: v.reshape(1, HY_FILT)
    full = lambda shp: pl.BlockSpec(shp, lambda f, i: tuple(0 for _ in shp))
    return pl.pallas_call(
        _filter_kernel,
        grid=(2, n // tr),
        in_specs=[
            pl.BlockSpec((tr, LANES), lambda f, i: (i, 0)),
            pl.BlockSpec((tr, 2), lambda f, i: (i, 0)),
            full((LANES, HY_FILT)), full((1, HY_FILT)),
            full((HY_FILT, HY_FILT)), full((1, HY_FILT)),
            full((HY_FILT, HY_FILT)), full((1, HY_FILT)),
            full((1, HY_FILT)),
            pl.BlockSpec((1, HY_FILT, e), lambda f, i: (f + 2 * (i // nhalf), 0, 0)),
            full((1, e)),
        ],
        out_specs=pl.BlockSpec((1, tr, e), lambda f, i: (f, i, 0)),
        out_shape=jax.ShapeDtypeStruct((2, n, e), F32),
        compiler_params=_cparams(("arbitrary", "arbitrary")),
        name="hyena_filter",
    )(feat, tcol, w1p, row(fb1), fw2, row(fb2), fw3, row(fb3), row(ffreq), wo, deltas)


def _fft_a_kernel(x_ref, zr_ref, zi_ref, *scratch, n1, ncol, ngrp, packed):
    plan = _fft_plan(n1, _radices(n1), -1)
    temps = [(scratch[0], scratch[1]), (scratch[2], scratch[3])]

    def body(c, carry):
        j = c // ngrp
        g = c % ngrp

        def load(i):
            if packed:
                if i >= n1 // 2:
                    return (None, None)
                return (x_ref[0, 0, i, j, g], x_ref[1, 0, i, j, g])
            return (x_ref[0, i, j, g], None)

        def store(i, x):
            zr_ref[0, i, j, g] = x[0]
            zi_ref[0, i, j, g] = x[1]

        _run_fft(plan, -1, load, store, temps)
        return carry

    lax.fori_loop(0, ncol, body, 0)


FFT_COL_ROWS = 256


def _fft_a(x, n1, n2, packed):
    g = N_GROUPS
    bn2 = min(max(FFT_COL_ROWS // n1, 1), n2)
    if packed:
        p = x.shape[1]
        in_spec = pl.BlockSpec((2, 1, n1 // 2, bn2, g, SUBLANES, LANES),
                               lambda pp, j: (0, pp, 0, j, 0, 0, 0))
    else:
        p = x.shape[0]
        in_spec = pl.BlockSpec((1, n1, bn2, g, SUBLANES, LANES), lambda pp, j: (pp, 0, j, 0, 0, 0))
    out_spec = pl.BlockSpec((1, n1, bn2, g, SUBLANES, LANES), lambda pp, j: (pp, 0, j, 0, 0, 0))
    shp = jax.ShapeDtypeStruct((p, n1, n2, g, SUBLANES, LANES), F32)
    return pl.pallas_call(
        functools.partial(_fft_a_kernel, n1=n1, ncol=bn2 * g, ngrp=g, packed=packed),
        grid=(p, n2 // bn2),
        in_specs=[in_spec],
        out_specs=[out_spec, out_spec],
        out_shape=[shp, shp],
        scratch_shapes=[pltpu.VMEM((n1, SUBLANES, LANES), F32) for _ in range(4)],
        compiler_params=_cparams(("arbitrary", "arbitrary")),
        name="fft_a_packed" if packed else "fft_a_real",
    )(x)


def _fft_mid_kernel(*refs, n2, ncol, ngrp, conv, scale):
    if conv:
        zr_ref, zi_ref, kr_ref, ki_ref, twr_ref, twi_ref, or_ref, oi_ref = refs[:8]
        scratch = refs[8:]
    else:
        zr_ref, zi_ref, twr_ref, twi_ref, or_ref, oi_ref = refs[:6]
        scratch = refs[6:]
    fwd = _fft_plan(n2, FFT_N2_RADICES, -1)
    inv = _fft_plan(n2, FFT_N2_RADICES, +1)
    temps = [(scratch[0], scratch[1]), (scratch[2], scratch[3])]
    sr, si = scratch[4], scratch[5]

    def body(c, carry):
        i = c // ngrp
        g = c % ngrp

        def tw(r):
            return (twr_ref[i, pl.ds(r, 1), :], twi_ref[i, pl.ds(r, 1), :])

        def load(r):
            return _cmul((zr_ref[0, i, r, g], zi_ref[0, i, r, g]), tw(r))

        if conv:
            def store_spec(r, x):
                y = _cmul(x, (kr_ref[i, r, g], ki_ref[i, r, g]))
                sr[r] = y[0]
                si[r] = y[1]

            _run_fft(fwd, -1, load, store_spec, temps)

            def store_out(r, x):
                t = tw(r)
                y = _cmul(x, (t[0], -t[1]))
                or_ref[0, i, r, g] = y[0]
                oi_ref[0, i, r, g] = y[1]

            _run_fft(inv, +1, lambda r: (sr[r], si[r]), store_out, temps)
        else:
            def store_spec(r, x):
                or_ref[0, i, r, g] = x[0] * scale
                oi_ref[0, i, r, g] = x[1] * scale

            _run_fft(fwd, -1, load, store_spec, temps)
        return carry

    lax.fori_loop(0, ncol, body, 0)


def _twiddle_table(n1, n2):
    n = n1 * n2
    m = (jnp.arange(n1)[:, None] * jnp.arange(n2)[None, :]) % n
    ang = m.astype(F32) * (-2.0 * math.pi / n)
    shape = (n1, n2, LANES)
    return (jnp.broadcast_to(jnp.cos(ang)[..., None], shape),
            jnp.broadcast_to(jnp.sin(ang)[..., None], shape))


def _fft_mid(zr, zi, tw, filt=None, scale=1.0, bk1=2):
    p, n1, n2, g = zr.shape[:4]
    bk1 = min(bk1, n1)
    zspec = pl.BlockSpec((1, bk1, n2, g, SUBLANES, LANES), lambda i, pp: (pp, i, 0, 0, 0, 0))
    twspec = pl.BlockSpec((bk1, n2, LANES), lambda i, pp: (i, 0, 0))
    conv = filt is not None
    in_specs = [zspec, zspec]
    args = [zr, zi]
    if conv:
        kspec = pl.BlockSpec((bk1, n2, g, SUBLANES, LANES), lambda i, pp: (i, 0, 0, 0, 0))
        in_specs += [kspec, kspec]
        args += [filt[0], filt[1]]
    in_specs += [twspec, twspec]
    args += [tw[0], tw[1]]
    shp = jax.ShapeDtypeStruct(zr.shape, F32)
    return pl.pallas_call(
        functools.partial(_fft_mid_kernel, n2=n2, ncol=bk1 * g, ngrp=g, conv=conv, scale=scale),
        grid=(n1 // bk1, p),
        in_specs=in_specs,
        out_specs=[zspec, zspec],
        out_shape=[shp, shp],
        scratch_shapes=[pltpu.VMEM((n2, SUBLANES, LANES), F32) for _ in range(6)],
        compiler_params=_cparams(("arbitrary", "arbitrary")),
        name="fft_mid_conv" if conv else "fft_mid_spec",
    )(*args)


def _fft_ainv_kernel(zr_ref, zi_ref, u_ref, x_ref, d_ref, o_ref, *scratch, n1, ncol, ngrp):
    plan = _fft_plan(n1, _radices(n1), +1)
    temps = [(scratch[0], scratch[1]), (scratch[2], scratch[3])]

    def body(c, carry):
        j = c // ngrp
        g = c % ngrp
        d = d_ref[g]

        def load(i):
            return (zr_ref[0, i, j, g], zi_ref[0, i, j, g])

        def store(i, y):
            if i < n1 // 2:
                o_ref[0, 0, i, j, g] = x_ref[0, 0, i, j, g] * (y[0] + u_ref[0, 0, i, j, g] * d)
                o_ref[1, 0, i, j, g] = x_ref[1, 0, i, j, g] * (y[1] + u_ref[1, 0, i, j, g] * d)

        _run_fft(plan, +1, load, store, temps)
        return carry

    lax.fori_loop(0, ncol, body, 0)


def _fft_ainv(zr, zi, u, u_blk, x, x_blk, d):
    p, n1, n2, g = zr.shape[:4]
    bn2 = min(max(FFT_COL_ROWS // n1, 1), n2)
    zspec = pl.BlockSpec((1, n1, bn2, g, SUBLANES, LANES), lambda pp, j: (pp, 0, j, 0, 0, 0))

    def sspec(blk):
        return pl.BlockSpec((2, 1, n1 // 2, bn2, g, SUBLANES, LANES),
                            lambda pp, j: (0, pp, 0, j, blk, 0, 0))

    return pl.pallas_call(
        functools.partial(_fft_ainv_kernel, n1=n1, ncol=bn2 * g, ngrp=g),
        grid=(p, n2 // bn2),
        in_specs=[zspec, zspec, sspec(u_blk), sspec(x_blk),
                  pl.BlockSpec((g, SUBLANES, LANES), lambda pp, j: (0, 0, 0))],
        out_specs=sspec(0),
        out_shape=jax.ShapeDtypeStruct((2, p, n1 // 2, n2, g, SUBLANES, LANES), F32),
        scratch_shapes=[pltpu.VMEM((n1, SUBLANES, LANES), F32) for _ in range(4)],
        compiler_params=_cparams(("arbitrary", "arbitrary")),
        name="fft_a_inv",
    )(zr, zi, u, x, d)


def _hyena_layer(x, norm_w, w_in, conv_w, conv_b, fw1, fb1, fw2, fb2, fw3, fb3, ffreq, fwout,
                 bias_d, w_out, final_norm=None):
    b, l, _ = x.shape
    e = D_INNER
    n = 2 * l
    n2 = min(FFT_N2, n // 2)
    n1 = n // n2
    p = b // 2
    x2d = x.reshape(b * l, D_MODEL)
    proj = _norm_matmul(x2d, norm_w, w_in)
    streams = _dwconv3(proj.reshape(b, l, 4 * e), 4 * N_GROUPS, 3 * N_GROUPS, conv_w, conv_b, False)
    sv = streams.reshape(2, p, n1 // 2, n2, 3 * N_GROUPS, SUBLANES, LANES)

    tw = _twiddle_table(n1, n2)
    kt = _hyena_two_sided_filters(l, fw1, fb1, fw2, fb2, fw3, fb3, ffreq, fwout)
    kr, ki = _fft_a(kt.reshape(2, n1, n2, N_GROUPS, SUBLANES, LANES), n1, n2, packed=False)
    kfr, kfi = _fft_mid(kr, ki, tw, scale=1.0 / n)
    d = bias_d.reshape(2, N_GROUPS, SUBLANES, LANES)

    zr, zi = _fft_a(sv, n1, n2, packed=True)
    zr, zi = _fft_mid(zr, zi, tw, filt=(kfr[0], kfi[0]))
    z1 = _fft_ainv(zr, zi, sv, 0, sv, 1, d[0])
    zr, zi = _fft_a(z1, n1, n2, packed=True)
    zr, zi = _fft_mid(zr, zi, tw, filt=(kfr[1], kfi[1]))
    z2 = _fft_ainv(zr, zi, z1, 0, sv, 2, d[1])
    out = _out_matmul([z2.reshape(b * l, e)], proj, 3, w_out, x2d, final_norm)
    return out.reshape(b, l, D_MODEL)


def _swa_kernel(q_ref, kp_ref, kc_ref, kn_ref, vp_ref, vc_ref, vn_ref,
                cq_ref, sq_ref, ckp_ref, skp_ref, ckn_ref, skn_ref, sink_ref, o_ref, *, nb):
    n = pl.program_id(1)
    lane = lax.broadcasted_iota(jnp.int32, (BLOCK, HEAD_DIM), 1)
    half = ROT_DIM // 2

    def rope(x, c, s):
        sw = jnp.where(lane < half, pltpu.roll(x, HEAD_DIM - half, 1), pltpu.roll(x, half, 1))
        return x * c + sw * s

    cq, sq = cq_ref[...], sq_ref[...]
    cks = (ckp_ref[...], cq, ckn_ref[...])
    sks = (skp_ref[...], sq, skn_ref[...])
    qi = lax.broadcasted_iota(jnp.int32, (BLOCK, 3 * BLOCK), 0)
    kj = lax.broadcasted_iota(jnp.int32, (BLOCK, 3 * BLOCK), 1) - BLOCK
    mask = jnp.abs(qi - kj) <= WINDOW
    mask = mask & ((kj >= 0) | (n > 0)) & ((kj < BLOCK) | (n < nb - 1))
    mask4 = jnp.concatenate([mask] * GROUP, axis=0)
    scale = HEAD_DIM ** -0.5
    for h in range(N_KV_HEADS):
        sl = slice(h * HEAD_DIM, (h + 1) * HEAD_DIM)
        kparts = [rope(r[0, :, sl], c, s) for r, c, s in zip((kp_ref, kc_ref, kn_ref), cks, sks)]
        kh = jnp.concatenate(kparts, axis=0).astype(BF16)
        vh = jnp.concatenate([vp_ref[0, :, sl], vc_ref[0, :, sl], vn_ref[0, :, sl]],
                             axis=0).astype(BF16)
        qs = []
        for g in range(GROUP):
            hq = h * GROUP + g
            qs.append(rope(q_ref[0, :, hq * HEAD_DIM:(hq + 1) * HEAD_DIM], cq, sq))
        qh = jnp.concatenate(qs, axis=0).astype(BF16)
        s = lax.dot_general(qh, kh, (((1,), (1,)), ((), ())), preferred_element_type=F32) * scale
        s = jnp.where(mask4, s, -jnp.inf)
        sink = jnp.concatenate(
            [jnp.full((BLOCK, 1), 1.0, F32) * sink_ref[h * GROUP + g] for g in range(GROUP)], axis=0)
        m = jnp.maximum(jnp.max(s, axis=-1, keepdims=True), sink)
        pr = jnp.exp(s - m)
        den = jnp.sum(pr, axis=-1, keepdims=True) + jnp.exp(sink - m)
        o = jnp.dot(pr.astype(BF16), vh, preferred_element_type=F32) * (1.0 / den)
        for g in range(GROUP):
            hq = h * GROUP + g
            o_ref[0, :, hq * HEAD_DIM:(hq + 1) * HEAD_DIM] = o[g * BLOCK:(g + 1) * BLOCK]


def _rope_tables(l):
    half = ROT_DIM // 2
    inv = ROPE_THETA ** (-jnp.arange(half, dtype=F32) / half)
    ang = jnp.arange(l, dtype=F32)[:, None] * inv[None, :]
    cos, sin = jnp.cos(ang), jnp.sin(ang)
    ones = jnp.ones((l, HEAD_DIM - ROT_DIM), F32)
    return (jnp.concatenate([cos, cos, ones], axis=-1),
            jnp.concatenate([-sin, sin, 0.0 * ones], axis=-1))


def _swa_layer(x, norm_w, w_in, sink, w_out, final_norm=None):
    b, l, _ = x.shape
    nb = l // BLOCK
    x2d = x.reshape(b * l, D_MODEL)
    w_perm = jnp.concatenate([w_in[:, :Q_DIM], w_in[:, Q_DIM + 2 * KV_DIM:],
                              w_in[:, Q_DIM:Q_DIM + 2 * KV_DIM]], axis=1)
    proj = _norm_matmul(x2d, norm_w, w_perm)
    p3 = proj.reshape(b, l, 2 * Q_DIM + 2 * KV_DIM)
    ct, st = _rope_tables(l)
    kblk = 2 * Q_DIM // KV_DIM
    prv = lambda bb, i: jnp.maximum(i - 1, 0)
    nxt = lambda bb, i: jnp.minimum(i + 1, nb - 1)
    kv = lambda col, f: pl.BlockSpec((1, BLOCK, KV_DIM), lambda bb, i: (bb, f(bb, i), col))
    cur = lambda bb, i: i
    tab = lambda f: pl.BlockSpec((BLOCK, HEAD_DIM), lambda bb, i: (f(bb, i), 0))
    o = pl.pallas_call(
        functools.partial(_swa_kernel, nb=nb),
        grid=(b, nb),
        in_specs=[
            pl.BlockSpec((1, BLOCK, Q_DIM), lambda bb, i: (bb, i, 0)),
            kv(kblk, prv), kv(kblk, cur), kv(kblk, nxt),
            kv(kblk + 1, prv), kv(kblk + 1, cur), kv(kblk + 1, nxt),
            tab(cur), tab(cur), tab(prv), tab(prv), tab(nxt), tab(nxt),
            pl.BlockSpec(memory_space=pltpu.SMEM),
        ],
        out_specs=pl.BlockSpec((1, BLOCK, Q_DIM), lambda bb, i: (bb, i, 0)),
        out_shape=jax.ShapeDtypeStruct((b, l, Q_DIM), F32),
        compiler_params=_cparams(("arbitrary", "arbitrary")),
        name="swa_attention",
    )(p3, p3, p3, p3, p3, p3, p3, ct, st, ct, st, ct, st, sink)
    out = _out_matmul([o.reshape(b * l, D_INNER)], proj, 1, w_out, x2d, final_norm)
    return out.reshape(b, l, D_MODEL)


def _ml_qkv_kernel(xc_ref, xi_ref, wq_ref, wk_ref, wv_ref, wg_ref, bg_ref, q_ref, k_ref, v_ref,
                   g_ref, gt_ref):
    hp = lax.Precision.HIGHEST
    nt = D_INNER // LANES
    qs, ks, vs = [], [], []
    for j in range(nt):
        sl = slice(j * LANES, (j + 1) * LANES)
        xc = xc_ref[:, sl]
        qs.append(jnp.dot(xc, wq_ref[j], precision=hp, preferred_element_type=F32))
        ks.append(jnp.dot(xc, wk_ref[j], precision=hp, preferred_element_type=F32))
        vs.append(jnp.dot(xi_ref[:, sl], wv_ref[j], precision=hp, preferred_element_type=F32))
    qkv = jnp.concatenate(qs + ks + vs, axis=-1)
    gates = jnp.dot(qkv, wg_ref[...], precision=hp, preferred_element_type=F32) + bg_ref[...]
    g_ref[...] = gates
    gt_ref[...] = gates.T
    e = D_INNER
    q_ref[...] = qkv[:, :e].astype(BF16)
    k_ref[...] = (qkv[:, e:2 * e] * (ML_HEAD_DIM ** -0.5)).astype(BF16)
    v_ref[...] = qkv[:, 2 * e:].astype(BF16)


def _blockdiag_tiles(w):
    per = LANES // ML_QKV_BLOCK
    wt = w.reshape(D_INNER // LANES, per, ML_QKV_BLOCK, ML_QKV_BLOCK)
    eye = jnp.eye(per, dtype=w.dtype)
    return jnp.einsum("tnio,nm->tnimo", wt, eye).reshape(D_INNER // LANES, LANES, LANES)


def _ml_qkv(xc2d, proj, wq, wk, wv, w_gate, b_gate, tm=256):
    t = xc2d.shape[0]
    tm = min(tm, t)
    e = D_INNER
    ng = 4 * ML_HEADS
    nt = e // LANES
    wspec = pl.BlockSpec((nt, LANES, LANES), lambda i: (0, 0, 0))
    act = jax.ShapeDtypeStruct((t, e), BF16)
    return pl.pallas_call(
        _ml_qkv_kernel,
        grid=(t // tm,),
        in_specs=[
            pl.BlockSpec((tm, e), lambda i: (i, 0)),
            pl.BlockSpec((tm, e), lambda i: (i, 0)),
            wspec, wspec, wspec,
            pl.BlockSpec((3 * e, ng), lambda i: (0, 0)),
            pl.BlockSpec((1, ng), lambda i: (0, 0)),
        ],
        out_specs=[pl.BlockSpec((tm, e), lambda i: (i, 0))] * 3
        + [pl.BlockSpec((tm, ng), lambda i: (i, 0)), pl.BlockSpec((ng, tm), lambda i: (0, i))],
        out_shape=[act, act, act, jax.ShapeDtypeStruct((t, ng), F32),
                   jax.ShapeDtypeStruct((ng, t), F32)],
        compiler_params=_cparams(("arbitrary",)),
        name="mlstm_qkv_gates",
    )(xc2d, proj, _blockdiag_tiles(wq), _blockdiag_tiles(wk), _blockdiag_tiles(wv), w_gate,
      b_gate.reshape(1, ng))


def _log_sigmoid(x):
    return jnp.minimum(x, 0.0) - jnp.log(1.0 + jnp.exp(-jnp.abs(x)))


def _ml_scan_kernel(q_ref, k_ref, v_ref, g_ref, gt_ref, o_ref, c_ref, n_ref, m_ref):
    d = pl.program_id(1)
    hd = pl.program_id(2)
    c = pl.program_id(3)
    lc = ML_CHUNK
    ng = 4 * ML_HEADS

    @pl.when(c == 0)
    def _():
        c_ref[...] = jnp.zeros_like(c_ref)
        n_ref[...] = jnp.zeros_like(n_ref)
        m_ref[...] = jnp.zeros_like(m_ref)

    col_i = 2 * ML_HEADS * d + hd
    col_f = col_i + ML_HEADS
    lane = lax.broadcasted_iota(jnp.int32, (lc, ng), 1)
    g = g_ref[...]
    i_col = jnp.sum(jnp.where(lane == col_i, g, 0.0), axis=-1, keepdims=True)
    f_col = _log_sigmoid(jnp.sum(jnp.where(lane == col_f, g, 0.0), axis=-1, keepdims=True))
    sub = lax.broadcasted_iota(jnp.int32, (ng, lc), 0)
    gt = gt_ref[...]
    i_row = jnp.sum(jnp.where(sub == col_i, gt, 0.0), axis=0, keepdims=True)
    f_row = _log_sigmoid(jnp.sum(jnp.where(sub == col_f, gt, 0.0), axis=0, keepdims=True))

    sgn = 1 - 2 * d
    t_idx = lax.broadcasted_iota(jnp.int32, (lc, lc), 0)
    s_idx = lax.broadcasted_iota(jnp.int32, (lc, lc), 1)
    causal = (t_idx - s_idx) * sgn >= 0
    causal_t = (s_idx - t_idx) * sgn >= 0
    b_col = jnp.sum(jnp.where(causal, f_row, 0.0), axis=-1, keepdims=True)
    b_row = jnp.sum(jnp.where(causal_t, f_col, 0.0), axis=0, keepdims=True)
    m_prev = m_ref[...]
    dmat = jnp.where(causal, b_col - b_row + i_row, -jnp.inf)
    inter = b_col + m_prev
    m_out = jnp.maximum(inter, jnp.max(dmat, axis=-1, keepdims=True))
    w_intra = jnp.exp(dmat - m_out)
    w_state = jnp.exp(inter - m_out)

    q = q_ref[...]
    k = k_ref[...]
    v = v_ref[...]
    s = lax.dot_general(q, k, (((1,), (1,)), ((), ())), preferred_element_type=F32) * w_intra
    num = jnp.dot(s.astype(BF16), v, preferred_element_type=F32)
    num = num + w_state * jnp.dot(q, c_ref[...].astype(BF16), preferred_element_type=F32)
    qn = jnp.sum(q.astype(F32) * n_ref[...], axis=-1, keepdims=True)
    den = jnp.sum(s, axis=-1, keepdims=True) + w_state * qn
    o_ref[0] = num / jnp.maximum(jnp.abs(den), jnp.exp(-m_out))

    last = jnp.where(d == 0, lc - 1, 0)
    row = lax.broadcasted_iota(jnp.int32, (lc, 1), 0)
    bl = jnp.sum(f_row, axis=-1, keepdims=True)
    m_new = jnp.sum(jnp.where(row == last, m_out, 0.0), axis=0, keepdims=True)
    wk = jnp.exp(bl - b_col + i_col - m_new)
    decay = jnp.exp(bl + m_prev - m_new)
    vw = (v.astype(F32) * wk).astype(BF16)
    c_ref[...] = decay * c_ref[...] + lax.dot_general(
        k, vw, (((0,), (0,)), ((), ())), preferred_element_type=F32)
    n_ref[...] = decay * n_ref[...] + jnp.sum(k.astype(F32) * wk, axis=0, keepdims=True)
    m_ref[...] = m_new


def _ml_scan(q, k, v, gates, gates_t, b, l):
    t = b * l
    nc = l // ML_CHUNK
    dh = ML_HEAD_DIM
    ng = 4 * ML_HEADS
    chunk = lambda bb, d, c: bb * nc + c + d * (nc - 1 - 2 * c)
    qspec = pl.BlockSpec((ML_CHUNK, dh), lambda bb, d, h, c: (chunk(bb, d, c), h))
    return pl.pallas_call(
        _ml_scan_kernel,
        grid=(b, 2, ML_HEADS, nc),
        in_specs=[
            qspec, qspec, qspec,
            pl.BlockSpec((ML_CHUNK, ng), lambda bb, d, h, c: (chunk(bb, d, c), 0)),
            pl.BlockSpec((ng, ML_CHUNK), lambda bb, d, h, c: (0, chunk(bb, d, c))),
        ],
        out_specs=pl.BlockSpec((1, ML_CHUNK, dh), lambda bb, d, h, c: (d, chunk(bb, d, c), h)),
        out_shape=jax.ShapeDtypeStruct((2, t, D_INNER), F32),
        scratch_shapes=[pltpu.VMEM((dh, dh), F32), pltpu.VMEM((1, dh), F32), pltpu.VMEM((1, 1), F32)],
        compiler_params=_cparams(("arbitrary", "arbitrary", "arbitrary", "arbitrary")),
        name="mlstm_scan",
    )(q, k, v, gates, gates_t)


def _mlstm_layer(x, norm_w, w_in, conv_w, conv_b, wq, wk, wv, w_gate, b_gate, head_norm, skip,
                 w_out, final_norm=None):
    b, l, _ = x.shape
    e = D_INNER
    x2d = x.reshape(b * l, D_MODEL)
    proj = _norm_matmul(x2d, norm_w, w_in)
    xc = _dwconv3(proj.reshape(b, l, 2 * e), 2 * N_GROUPS, N_GROUPS, conv_w, conv_b, True)
    xc2d = xc.reshape(b * l, e)
    q, k, v, gates, gates_t = _ml_qkv(xc2d, proj, wq, wk, wv, w_gate, b_gate)
    hs = _ml_scan(q, k, v, gates, gates_t, b, l)
    out = _out_matmul([hs[0], hs[1], xc2d], proj, 1, w_out, x2d, final_norm,
                      extra=(head_norm, skip))
    return out.reshape(b, l, D_MODEL)


def _trunk(x, layers, final_norm):
    kinds = (_hyena_layer, _swa_layer, _mlstm_layer, _hyena_layer)
    for i, (fn, params) in enumerate(zip(kinds, layers)):
        x = fn(x, *params, final_norm=final_norm if i == len(kinds) - 1 else None)
    return x


def kernel(x_prompt, x_sample, l0_norm, l0_hy_w_in, l0_hy_conv_w, l0_hy_conv_b, l0_hy_filt_w1, l0_hy_filt_b1, l0_hy_filt_w2, l0_hy_filt_b2, l0_hy_filt_w3, l0_hy_filt_b3, l0_hy_filt_freq, l0_hy_filt_wout, l0_hy_bias_d, l0_hy_w_out, l1_norm, l1_swa_w_in, l1_swa_sink, l1_swa_w_out, l2_norm, l2_ml_w_in, l2_ml_conv_w, l2_ml_conv_b, l2_ml_wq, l2_ml_wk, l2_ml_wv, l2_ml_w_gate, l2_ml_b_gate, l2_ml_head_norm, l2_ml_skip, l2_ml_w_out, l3_norm, l3_hy_w_in, l3_hy_conv_w, l3_hy_conv_b, l3_hy_filt_w1, l3_hy_filt_b1, l3_hy_filt_w2, l3_hy_filt_b2, l3_hy_filt_w3, l3_hy_filt_b3, l3_hy_filt_freq, l3_hy_filt_wout, l3_hy_bias_d, l3_hy_w_out, final_norm):
    layers = [
        (l0_norm, l0_hy_w_in, l0_hy_conv_w, l0_hy_conv_b, l0_hy_filt_w1, l0_hy_filt_b1, l0_hy_filt_w2,
         l0_hy_filt_b2, l0_hy_filt_w3, l0_hy_filt_b3, l0_hy_filt_freq, l0_hy_filt_wout, l0_hy_bias_d,
         l0_hy_w_out),
        (l1_norm, l1_swa_w_in, l1_swa_sink, l1_swa_w_out),
        (l2_norm, l2_ml_w_in, l2_ml_conv_w, l2_ml_conv_b, l2_ml_wq, l2_ml_wk, l2_ml_wv, l2_ml_w_gate,
         l2_ml_b_gate, l2_ml_head_norm, l2_ml_skip, l2_ml_w_out),
        (l3_norm, l3_hy_w_in, l3_hy_conv_w, l3_hy_conv_b, l3_hy_filt_w1, l3_hy_filt_b1, l3_hy_filt_w2,
         l3_hy_filt_b2, l3_hy_filt_w3, l3_hy_filt_b3, l3_hy_filt_freq, l3_hy_filt_wout, l3_hy_bias_d,
         l3_hy_w_out),
    ]
    return (_trunk(x_prompt, layers, final_norm), _trunk(x_sample, layers, final_norm))
```

```python
import cmath
import functools
import math

import jax
import jax.numpy as jnp
from jax import lax
from jax.experimental import pallas as pl
from jax.experimental.pallas import tpu as pltpu

F32 = jnp.float32
BF16 = jnp.bfloat16

D_MODEL = 1024
D_INNER = 2048
RMS_EPS = 1e-6
LN_EPS = 1e-6

LANES = 128
SUBLANES = 8
N_LANE_BLOCKS = D_INNER // LANES
VMEM_LIMIT = 56 * 1024 * 1024

HY_EMB = 33
HY_FILT = 64
HY_MIN_DECAY = math.log(1e-2) / 1.5
HY_MAX_DECAY = math.log(1e-2) / 0.3
FFT_N2 = 128
FFT_N2_RADICES = (8, 4, 4)
FFT_A_ROWS = 2048
FFT_MID_COLS = 4

HEAD_DIM = 128
N_Q_HEADS = D_INNER // HEAD_DIM
N_KV_HEADS = 4
GROUP = N_Q_HEADS // N_KV_HEADS
WINDOW = 128
BLOCK = 128
ROT_DIM = HEAD_DIM // 4
ROPE_THETA = 500000.0
Q_DIM = N_Q_HEADS * HEAD_DIM
KV_DIM = N_KV_HEADS * HEAD_DIM

ML_HEADS = 4
ML_HEAD_DIM = D_INNER // ML_HEADS
ML_QKV_BLOCK = 4
ML_CHUNK = 128


def _cparams(sem):
    return pltpu.CompilerParams(dimension_semantics=sem, vmem_limit_bytes=VMEM_LIMIT)


def _silu(x):
    return x * (1.0 / (1.0 + jnp.exp(-x)))


def _add(a, b):
    if a is None:
        return b
    if b is None:
        return a
    return a + b


def _sub(a, b):
    if b is None:
        return a
    if a is None:
        return -b
    return a - b


def _scale(a, c):
    if a is None or c == 0.0:
        return None
    if c == 1.0:
        return a
    if c == -1.0:
        return -a
    return a * c


def _cadd(x, y):
    return (_add(x[0], y[0]), _add(x[1], y[1]))


def _csub(x, y):
    return (_sub(x[0], y[0]), _sub(x[1], y[1]))


def _snap(v):
    for t in (0.0, 1.0, -1.0):
        if abs(v - t) < 1e-14:
            return t
    return v


def _cmul_const(x, c):
    cr, ci = _snap(c.real), _snap(c.imag)
    re = _sub(_scale(x[0], cr), _scale(x[1], ci))
    im = _add(_scale(x[0], ci), _scale(x[1], cr))
    return (re, im)


def _cmul(x, y):
    xr, xi = x
    yr, yi = y
    if xi is None:
        return (xr * yr, xr * yi)
    return (xr * yr - xi * yi, xr * yi + xi * yr)


def _dense(x):
    re, im = x
    if re is None:
        re = jnp.zeros_like(im)
    if im is None:
        im = jnp.zeros_like(re)
    return (re, im)


def _small_dft(xs, sign):
    r = len(xs)
    if r == 1:
        return list(xs)
    ev = _small_dft(xs[0::2], sign)
    od = _small_dft(xs[1::2], sign)
    out = [None] * r
    for k in range(r // 2):
        t = _cmul_const(od[k], cmath.exp(sign * 2j * math.pi * k / r))
        out[k] = _cadd(ev[k], t)
        out[k + r // 2] = _csub(ev[k], t)
    return out


def _fft_plan(n, radices, sign):
    assert math.prod(radices) == n
    passes = []
    s, ncur = 1, n
    for r in radices:
        m = ncur // r
        bfs = []
        for p in range(m):
            tws = [cmath.exp(sign * 2j * math.pi * p * k / ncur) for k in range(r)]
            for q in range(s):
                ins = [q + s * (p + m * j) for j in range(r)]
                outs = [q + s * (r * p + k) for k in range(r)]
                bfs.append((ins, outs, tws))
        passes.append(bfs)
        ncur, s = m, s * r
    return passes


def _run_fft(plan, sign, load, store, temps):
    npass = len(plan)
    for pi, bfs in enumerate(plan):
        if pi == 0:
            ld = load
        else:
            tr, ti = temps[(pi - 1) % 2]
            ld = lambda i, tr=tr, ti=ti: (tr[i], ti[i])
        if pi == npass - 1:
            st = store
        else:
            tr2, ti2 = temps[pi % 2]

            def st(i, x, tr2=tr2, ti2=ti2):
                tr2[i] = x[0]
                ti2[i] = x[1]
        for ins, outs, tws in bfs:
            ys = _small_dft([ld(i) for i in ins], sign)
            for o, y, tw in zip(outs, ys, tws):
                st(o, _dense(_cmul_const(y, tw)))


def _radices(n):
    out = []
    while n > 1:
        r = 8 if n % 8 == 0 else (4 if n % 4 == 0 else 2)
        out.append(r)
        n //= r
    return tuple(out)


def _norm_matmul_kernel(x_ref, nw_ref, w_ref, o_ref, xn_ref):
    @pl.when(pl.program_id(1) == 0)
    def _():
        x = x_ref[...]
        inv = lax.rsqrt(jnp.mean(x * x, axis=-1, keepdims=True) + RMS_EPS)
        xn_ref[...] = ((x * inv) * nw_ref[...]).astype(BF16)

    o_ref[...] = jnp.dot(xn_ref[...], w_ref[...], preferred_element_type=F32)


def _norm_matmul(x2d, norm_w, w, tm=1024, tn=1024):
    t, d = x2d.shape
    n = w.shape[1]
    tm = min(tm, t)
    tn = min(tn, n)
    return pl.pallas_call(
        _norm_matmul_kernel,
        grid=(t // tm, n // tn),
        in_specs=[
            pl.BlockSpec((tm, d), lambda i, j: (i, 0)),
            pl.BlockSpec((1, d), lambda i, j: (0, 0)),
            pl.BlockSpec((d, tn), lambda i, j: (0, j)),
        ],
        out_specs=pl.BlockSpec((tm, tn), lambda i, j: (i, j)),
        out_shape=jax.ShapeDtypeStruct((t, n), F32),
        scratch_shapes=[pltpu.VMEM((tm, d), BF16)],
        compiler_params=_cparams(("arbitrary", "arbitrary")),
        name="norm_matmul",
    )(x2d, norm_w.reshape(1, d), w.astype(BF16))


HALO = 16


def _norm_matmul_conv_kernel(x_ref, xp_ref, xn_ref, nw_ref, w_ref, cw_ref, cb_ref, *rest,
                             tm, nconv, seq_tiles, act, dual):
    if dual:
        o_ref, oc_ref, xs_ref = rest
    else:
        o_ref, xs_ref = rest
    i = pl.program_id(0)
    j = pl.program_id(1)

    @pl.when(j == 0)
    def _():
        def nrm(x):
            inv = lax.rsqrt(jnp.mean(x * x, axis=-1, keepdims=True) + RMS_EPS)
            return (x * inv) * nw_ref[...]

        first = (i % seq_tiles) == 0
        last = (i % seq_tiles) == seq_tiles - 1
        xs_ref[0:HALO] = jnp.where(first, 0.0, nrm(xp_ref[...])).astype(BF16)
        xs_ref[HALO:HALO + tm] = nrm(x_ref[...]).astype(BF16)
        xs_ref[HALO + tm:] = jnp.where(last, 0.0, nrm(xn_ref[...])).astype(BF16)

    @pl.when(j < nconv)
    def _():
        acc = jnp.dot(xs_ref[...], w_ref[...], preferred_element_type=F32)
        rows = tm + 2 * HALO
        y = (acc * cw_ref[1:2] + pltpu.roll(acc, 1, 0) * cw_ref[0:1]
             + pltpu.roll(acc, rows - 1, 0) * cw_ref[2:3] + cb_ref[...])
        y = y[HALO:HALO + tm]
        if act:
            y = _silu(y)
        if dual:
            oc_ref[...] = y
            o_ref[...] = acc[HALO:HALO + tm]
        else:
            o_ref[...] = y

    @pl.when(j >= nconv)
    def _():
        o_ref[...] = jnp.dot(xs_ref[HALO:HALO + tm], w_ref[...], preferred_element_type=F32)


def _norm_matmul_conv(x2d, norm_w, w, conv_w, conv_b, seq_len, act, dual, tm=1024, tn=1024):
    t, d = x2d.shape
    n = w.shape[1]
    nc = conv_w.shape[1]
    tm = min(tm, seq_len)
    tn = min(tn, nc)
    nconv = nc // tn
    hb = tm // HALO
    nhb = t // HALO
    cj = lambda j: jnp.minimum(j, nconv - 1)
    in_specs = [
        pl.BlockSpec((tm, d), lambda i, j: (i, 0)),
        pl.BlockSpec((HALO, d), lambda i, j: (jnp.maximum(i * hb - 1, 0), 0)),
        pl.BlockSpec((HALO, d), lambda i, j: (jnp.minimum((i + 1) * hb, nhb - 1), 0)),
        pl.BlockSpec((1, d), lambda i, j: (0, 0)),
        pl.BlockSpec((d, tn), lambda i, j: (0, j)),
        pl.BlockSpec((3, tn), lambda i, j: (0, cj(j))),
        pl.BlockSpec((1, tn), lambda i, j: (0, cj(j))),
    ]
    out_specs = pl.BlockSpec((tm, tn), lambda i, j: (i, j))
    out_shape = jax.ShapeDtypeStruct((t, n), F32)
    if dual:
        out_specs = [out_specs, pl.BlockSpec((tm, tn), lambda i, j: (i, cj(j)))]
        out_shape = [out_shape, jax.ShapeDtypeStruct((t, nc), F32)]
    return pl.pallas_call(
        functools.partial(_norm_matmul_conv_kernel, tm=tm, nconv=nconv, seq_tiles=seq_len // tm,
                          act=act, dual=dual),
        grid=(t // tm, n // tn),
        in_specs=in_specs,
        out_specs=out_specs,
        out_shape=out_shape,
        scratch_shapes=[pltpu.VMEM((tm + 2 * HALO, d), BF16)],
        compiler_params=_cparams(("arbitrary", "arbitrary")),
        name="norm_matmul_conv",
    )(x2d, x2d, x2d, norm_w.reshape(1, d), w.astype(BF16), conv_w, conv_b.reshape(1, nc))


def _out_kernel(*refs, mode, final):
    if mode == "mlstm":
        hf_ref, hb_ref, xc_ref, g_ref, hn_ref, sk_ref, w_ref, r_ref = refs[:8]
        rest = refs[8:]
        h = hf_ref[...] + hb_ref[...]
        parts = []
        for hd in range(ML_HEADS):
            seg = h[:, hd * ML_HEAD_DIM:(hd + 1) * ML_HEAD_DIM]
            mu = jnp.mean(seg, axis=-1, keepdims=True)
            cen = seg - mu
            var = jnp.mean(cen * cen, axis=-1, keepdims=True)
            parts.append(cen * lax.rsqrt(var + LN_EPS))
        a = jnp.concatenate(parts, axis=-1) * hn_ref[...] + sk_ref[...] * xc_ref[...]
    else:
        a_ref, g_ref, w_ref, r_ref = refs[:4]
        rest = refs[4:]
        a = a_ref[...]
    if final:
        fn_ref, o_ref = rest
    else:
        (o_ref,) = rest
    y = jnp.dot((a * _silu(g_ref[...])).astype(BF16), w_ref[...], preferred_element_type=F32)
    x = r_ref[...] + y
    if final:
        inv = lax.rsqrt(jnp.mean(x * x, axis=-1, keepdims=True) + RMS_EPS)
        x = (x * inv) * fn_ref[...]
    o_ref[...] = x


def _out_matmul(acts, gate_arr, gate_blk, w, resid, final_norm=None, extra=(), tm=256):
    t = resid.shape[0]
    tm = min(tm, t)
    e = D_INNER
    mode = "mlstm" if len(acts) == 3 else "plain"
    in_specs, args = [], []
    for a in acts:
        if isinstance(a, tuple):
            arr, idx = a
            in_specs.append(pl.BlockSpec((None, tm, e), lambda i, idx=idx: (idx, i, 0)))
            args.append(arr)
        else:
            in_specs.append(pl.BlockSpec((tm, e), lambda i: (i, 0)))
            args.append(a)
    in_specs.append(pl.BlockSpec((tm, e), lambda i: (i, gate_blk)))
    args.append(gate_arr)
    for v in extra:
        in_specs.append(pl.BlockSpec((1, e), lambda i: (0, 0)))
        args.append(v.reshape(1, e))
    in_specs.append(pl.BlockSpec((e, D_MODEL), lambda i: (0, 0)))
    args.append(w.astype(BF16))
    in_specs.append(pl.BlockSpec((tm, D_MODEL), lambda i: (i, 0)))
    args.append(resid)
    if final_norm is not None:
        in_specs.append(pl.BlockSpec((1, D_MODEL), lambda i: (0, 0)))
        args.append(final_norm.reshape(1, D_MODEL))
    return pl.pallas_call(
        functools.partial(_out_kernel, mode=mode, final=final_norm is not None),
        grid=(t // tm,),
        in_specs=in_specs,
        out_specs=pl.BlockSpec((tm, D_MODEL), lambda i: (i, 0)),
        out_shape=jax.ShapeDtypeStruct((t, D_MODEL), F32),
        compiler_params=_cparams(("arbitrary",)),
        name="out_matmul_" + mode,
    )(*args)


def _filter_kernel(feat_ref, t_ref, w1_ref, b1_ref, w2_ref, b2_ref, w3_ref, b3_ref, fr_ref,
                   wo_ref, dl_ref, o_ref):
    hp = lax.Precision.HIGHEST
    fr = fr_ref[...]
    h = jnp.sin(fr * (jnp.dot(feat_ref[...], w1_ref[...], precision=hp,
                              preferred_element_type=F32) + b1_ref[...]))
    h = jnp.sin(fr * (jnp.dot(h, w2_ref[...], precision=hp, preferred_element_type=F32) + b2_ref[...]))
    h = jnp.sin(fr * (jnp.dot(h, w3_ref[...], precision=hp, preferred_element_type=F32) + b3_ref[...]))
    tcol = t_ref[...]
    decay = jnp.exp(-tcol[:, 0:1] * dl_ref[...]) * tcol[:, 1:2]
    for f in range(2):
        k = jnp.dot(h, wo_ref[f], precision=hp, preferred_element_type=F32)
        o_ref[f] = k * decay


def _hyena_two_sided_filters(l, fw1, fb1, fw2, fb2, fw3, fb3, ffreq, fwout, tr=512):
    n = 2 * l
    e = D_INNER
    tr = min(tr, l)
    idx = jnp.arange(n)
    src = jnp.where(idx < l, idx, n - idx)
    valid = (idx != l).astype(F32)
    src = jnp.where(idx == l, 0, src)
    t_all = jnp.linspace(0.0, 1.0, l, dtype=F32)
    bands = (HY_EMB - 1) // 2
    ang = 2.0 * math.pi * jnp.arange(l, dtype=F32)[:, None] / l
    fb = jnp.linspace(1e-4, bands - 1, bands, dtype=F32)[None, :]
    feat = jnp.concatenate([t_all[:, None], jnp.cos(fb * ang), -jnp.sin(fb * ang)], axis=-1)
    feat = jnp.pad(feat, ((0, 0), (0, LANES - HY_EMB)))[src]
    tcol = jnp.stack([t_all[src], valid], axis=-1)
    w1p = jnp.pad(fw1, ((0, LANES - HY_EMB), (0, 0)))
    deltas = jnp.abs(jnp.linspace(HY_MIN_DECAY, HY_MAX_DECAY, e, dtype=F32)).reshape(1, e)
    wo = fwout.reshape(HY_FILT, 4, e).transpose(1, 0, 2)
    nhalf = l // tr
    row = lambda v: v.reshape(1, HY_FILT)
    full = lambda shp: pl.BlockSpec(shp, lambda i: tuple(0 for _ in shp))
    return pl.pallas_call(
        _filter_kernel,
        grid=(n // tr,),
        in_specs=[
            pl.BlockSpec((tr, LANES), lambda i: (i, 0)),
            pl.BlockSpec((tr, 2), lambda i: (i, 0)),
            full((LANES, HY_FILT)), full((1, HY_FILT)),
            full((HY_FILT, HY_FILT)), full((1, HY_FILT)),
            full((HY_FILT, HY_FILT)), full((1, HY_FILT)),
            full((1, HY_FILT)),
            pl.BlockSpec((2, HY_FILT, e), lambda i: (i // nhalf, 0, 0)),
            full((1, e)),
        ],
        out_specs=pl.BlockSpec((2, tr, e), lambda i: (0, i, 0)),
        out_shape=jax.ShapeDtypeStruct((2, n, e), F32),
        compiler_params=_cparams(("arbitrary",)),
        name="hyena_filter",
    )(feat, tcol, w1p, row(fb1), fw2, row(fb2), fw3, row(fb3), row(ffreq), wo, deltas)


def _fft_a_kernel(x_ref, zr_ref, zi_ref, *scratch, n1, nj, packed):
    plan = _fft_plan(n1, _radices(n1), -1)
    temps = [(scratch[0], scratch[1]), (scratch[2], scratch[3])]
    fr, fi = scratch[4], scratch[5]

    def body(j, carry):
        rows = pl.ds(pl.multiple_of(j * SUBLANES, SUBLANES), SUBLANES)

        def load(i):
            if packed:
                if i >= n1 // 2:
                    return (None, None)
                return (x_ref[0, 0, i, rows, :], x_ref[1, 0, i, rows, :])
            return (x_ref[0, i, rows, :], None)

        def store(i, x):
            fr[i] = x[0]
            fi[i] = x[1]

        _run_fft(plan, -1, load, store, temps)
        for a in range(n1 // SUBLANES):
            sl = slice(a * SUBLANES, (a + 1) * SUBLANES)
            zr_ref[0, a, 0, rows] = jnp.swapaxes(fr[sl], 0, 1)
            zi_ref[0, a, 0, rows] = jnp.swapaxes(fi[sl], 0, 1)
        return carry

    lax.fori_loop(0, nj, body, 0)


def _fft_a(x, n1, n2, packed, lane_blk0=0):
    nl = N_LANE_BLOCKS
    bn2 = min(max(FFT_A_ROWS // n1, SUBLANES), n2)
    if packed:
        p = x.shape[1]
        in_spec = pl.BlockSpec((2, 1, n1 // 2, bn2, LANES),
                               lambda pp, lb, j: (0, pp, 0, j, lane_blk0 + lb))
    else:
        p = x.shape[0]
        in_spec = pl.BlockSpec((1, n1, bn2, LANES), lambda pp, lb, j: (pp, 0, j, lb))
    out_spec = pl.BlockSpec((1, n1 // SUBLANES, 1, bn2, SUBLANES, LANES),
                            lambda pp, lb, j: (pp, 0, lb, j, 0, 0))
    shp = jax.ShapeDtypeStruct((p, n1 // SUBLANES, nl, n2, SUBLANES, LANES), F32)
    return pl.pallas_call(
        functools.partial(_fft_a_kernel, n1=n1, nj=bn2 // SUBLANES, packed=packed),
        grid=(p, nl, n2 // bn2),
        in_specs=[in_spec],
        out_specs=[out_spec, out_spec],
        out_shape=[shp, shp],
        scratch_shapes=[pltpu.VMEM((n1, SUBLANES, LANES), F32) for _ in range(6)],
        compiler_params=_cparams(("arbitrary", "arbitrary", "arbitrary")),
        name="fft_a_packed" if packed else "fft_a_real",
    )(x)


def _fft_mid_kernel(*refs, n2, ncol, nl, conv, scale):
    if conv:
        zr_ref, zi_ref, kr_ref, ki_ref, twr_ref, twi_ref, or_ref, oi_ref = refs[:8]
        scratch = refs[8:]
    else:
        zr_ref, zi_ref, twr_ref, twi_ref, or_ref, oi_ref = refs[:6]
        scratch = refs[6:]
    fwd = _fft_plan(n2, FFT_N2_RADICES, -1)
    inv = _fft_plan(n2, FFT_N2_RADICES, +1)
    temps = [(scratch[0], scratch[1]), (scratch[2], scratch[3])]
    sr, si = scratch[4], scratch[5]

    def body(c, carry):
        i = c // nl
        g = c % nl

        def tw(r):
            return (twr_ref[i, r], twi_ref[i, r])

        def load(r):
            return _cmul((zr_ref[0, i, g, r], zi_ref[0, i, g, r]), tw(r))

        if conv:
            def store_spec(r, x):
                y = _cmul(x, (kr_ref[0, i, g, r], ki_ref[0, i, g, r]))
                sr[r] = y[0]
                si[r] = y[1]

            _run_fft(fwd, -1, load, store_spec, temps)

            def store_out(r, x):
                t = tw(r)
                y = _cmul(x, (t[0], -t[1]))
                or_ref[0, i, g, r] = y[0]
                oi_ref[0, i, g, r] = y[1]

            _run_fft(inv, +1, lambda r: (sr[r], si[r]), store_out, temps)
        else:
            def store_spec(r, x):
                or_ref[0, i, g, r] = x[0] * scale
                oi_ref[0, i, g, r] = x[1] * scale

            _run_fft(fwd, -1, load, store_spec, temps)
        return carry

    lax.fori_loop(0, ncol, body, 0)


def _twiddle_table(n1, n2):
    n = n1 * n2
    k1 = jnp.arange(n1).reshape(n1 // SUBLANES, 1, SUBLANES)
    m = (k1 * jnp.arange(n2).reshape(1, n2, 1)) % n
    ang = m.astype(F32) * (-2.0 * math.pi / n)
    shape = (n1 // SUBLANES, n2, SUBLANES, LANES)
    return (jnp.broadcast_to(jnp.cos(ang)[..., None], shape),
            jnp.broadcast_to(jnp.sin(ang)[..., None], shape))


def _fft_mid(zr, zi, tw, filt=None, scale=1.0):
    p, na, nl, n2 = zr.shape[:4]
    bl = min(FFT_MID_COLS, nl)
    zspec = pl.BlockSpec((1, 1, bl, n2, SUBLANES, LANES), lambda i, g, pp: (pp, i, g, 0, 0, 0))
    twspec = pl.BlockSpec((1, n2, SUBLANES, LANES), lambda i, g, pp: (i, 0, 0, 0))
    conv = filt is not None
    in_specs = [zspec, zspec]
    args = [zr, zi]
    if conv:
        fidx = filt[2]
        kspec = pl.BlockSpec((1, 1, bl, n2, SUBLANES, LANES), lambda i, g, pp: (fidx, i, g, 0, 0, 0))
        in_specs += [kspec, kspec]
        args += [filt[0], filt[1]]
    in_specs += [twspec, twspec]
    args += [tw[0], tw[1]]
    shp = jax.ShapeDtypeStruct(zr.shape, F32)
    return pl.pallas_call(
        functools.partial(_fft_mid_kernel, n2=n2, ncol=bl, nl=bl, conv=conv, scale=scale),
        grid=(na, nl // bl, p),
        in_specs=in_specs,
        out_specs=[zspec, zspec],
        out_shape=[shp, shp],
        scratch_shapes=[pltpu.VMEM((n2, SUBLANES, LANES), F32) for _ in range(6)],
        compiler_params=_cparams(("arbitrary", "arbitrary", "arbitrary")),
        name="fft_mid_conv" if conv else "fft_mid_spec",
    )(*args)


def _fft_ainv_kernel(zr_ref, zi_ref, u_ref, x_ref, d_ref, o_ref, *scratch, n1, nj):
    plan = _fft_plan(n1, _radices(n1), +1)
    temps = [(scratch[0], scratch[1]), (scratch[2], scratch[3])]
    lr, li = scratch[4], scratch[5]
    d = d_ref[...]

    def body(j, carry):
        rows = pl.ds(pl.multiple_of(j * SUBLANES, SUBLANES), SUBLANES)
        for a in range(n1 // SUBLANES):
            sl = slice(a * SUBLANES, (a + 1) * SUBLANES)
            lr[sl] = jnp.swapaxes(zr_ref[0, a, 0, rows], 0, 1)
            li[sl] = jnp.swapaxes(zi_ref[0, a, 0, rows], 0, 1)

        def store(i, y):
            if i < n1 // 2:
                o_ref[0, 0, i, rows, :] = x_ref[0, 0, i, rows, :] * (y[0] + u_ref[0, 0, i, rows, :] * d)
                o_ref[1, 0, i, rows, :] = x_ref[1, 0, i, rows, :] * (y[1] + u_ref[1, 0, i, rows, :] * d)

        _run_fft(plan, +1, lambda i: (lr[i], li[i]), store, temps)
        return carry

    lax.fori_loop(0, nj, body, 0)


def _fft_ainv(zr, zi, u, u_blk0, x, x_blk0, d, d_row):
    p, na, nl, n2 = zr.shape[:4]
    n1 = na * SUBLANES
    bn2 = min(max(FFT_A_ROWS // n1, SUBLANES), n2)
    zspec = pl.BlockSpec((1, na, 1, bn2, SUBLANES, LANES), lambda pp, lb, j: (pp, 0, lb, j, 0, 0))

    def sspec(blk0):
        return pl.BlockSpec((2, 1, n1 // 2, bn2, LANES), lambda pp, lb, j: (0, pp, 0, j, blk0 + lb))

    return pl.pallas_call(
        functools.partial(_fft_ainv_kernel, n1=n1, nj=bn2 // SUBLANES),
        grid=(p, nl, n2 // bn2),
        in_specs=[zspec, zspec, sspec(u_blk0), sspec(x_blk0),
                  pl.BlockSpec((1, LANES), lambda pp, lb, j: (0, lb))],
        out_specs=sspec(0),
        out_shape=jax.ShapeDtypeStruct((2, p, n1 // 2, n2, D_INNER), F32),
        scratch_shapes=[pltpu.VMEM((n1, SUBLANES, LANES), F32) for _ in range(6)],
        compiler_params=_cparams(("arbitrary", "arbitrary", "arbitrary")),
        name="fft_a_inv",
    )(zr, zi, u, x, d[d_row].reshape(1, D_INNER))


def _hyena_layer(x, norm_w, w_in, conv_w, conv_b, fw1, fb1, fw2, fb2, fw3, fb3, ffreq, fwout,
                 bias_d, w_out, final_norm=None):
    b, l, _ = x.shape
    e = D_INNER
    n = 2 * l
    n2 = FFT_N2
    n1 = n // n2
    p = b // 2
    nl = N_LANE_BLOCKS
    x2d = x.reshape(b * l, D_MODEL)
    proj = _norm_matmul_conv(x2d, norm_w, w_in, conv_w, conv_b, l, act=False, dual=False)
    pv = proj.reshape(2, p, n1 // 2, n2, 4 * e)

    tw = _twiddle_table(n1, n2)
    kt = _hyena_two_sided_filters(l, fw1, fb1, fw2, fb2, fw3, fb3, ffreq, fwout)
    kr, ki = _fft_a(kt.reshape(2, n1, n2, e), n1, n2, packed=False)
    kfr, kfi = _fft_mid(kr, ki, tw, scale=1.0 / n)

    zr, zi = _fft_a(pv, n1, n2, packed=True)
    zr, zi = _fft_mid(zr, zi, tw, filt=(kfr, kfi, 0))
    z1 = _fft_ainv(zr, zi, pv, 0, pv, nl, bias_d, 0)
    zr, zi = _fft_a(z1, n1, n2, packed=True)
    zr, zi = _fft_mid(zr, zi, tw, filt=(kfr, kfi, 1))
    z2 = _fft_ainv(zr, zi, z1, 0, pv, 2 * nl, bias_d, 1)
    out = _out_matmul([z2.reshape(b * l, e)], proj, 3, w_out, x2d, final_norm)
    return out.reshape(b, l, D_MODEL)


def _swa_kernel(q_ref, kp_ref, kc_ref, kn_ref, vp_ref, vc_ref, vn_ref,
                cq_ref, sq_ref, ckp_ref, skp_ref, ckn_ref, skn_ref, sink_ref, o_ref, *, nb):
    n = pl.program_id(1)
    lane = lax.broadcasted_iota(jnp.int32, (BLOCK, HEAD_DIM), 1)
    half = ROT_DIM // 2

    def rope(x, c, s):
        sw = jnp.where(lane < half, pltpu.roll(x, HEAD_DIM - half, 1), pltpu.roll(x, half, 1))
        return x * c + sw * s

    cq, sq = cq_ref[...], sq_ref[...]
    cks = (ckp_ref[...], cq, ckn_ref[...])
    sks = (skp_ref[...], sq, skn_ref[...])
    qi = lax.broadcasted_iota(jnp.int32, (BLOCK, 3 * BLOCK), 0)
    kj = lax.broadcasted_iota(jnp.int32, (BLOCK, 3 * BLOCK), 1) - BLOCK
    mask = jnp.abs(qi - kj) <= WINDOW
    mask = mask & ((kj >= 0) | (n > 0)) & ((kj < BLOCK) | (n < nb - 1))
    mask4 = jnp.concatenate([mask] * GROUP, axis=0)
    scale = HEAD_DIM ** -0.5
    for h in range(N_KV_HEADS):
        sl = slice(h * HEAD_DIM, (h + 1) * HEAD_DIM)
        kparts = [rope(r[0, :, sl], c, s) for r, c, s in zip((kp_ref, kc_ref, kn_ref), cks, sks)]
        kh = jnp.concatenate(kparts, axis=0).astype(BF16)
        vh = jnp.concatenate([vp_ref[0, :, sl], vc_ref[0, :, sl], vn_ref[0, :, sl]],
                             axis=0).astype(BF16)
        qs = []
        for g in range(GROUP):
            hq = h * GROUP + g
            qs.append(rope(q_ref[0, :, hq * HEAD_DIM:(hq + 1) * HEAD_DIM], cq, sq))
        qh = jnp.concatenate(qs, axis=0).astype(BF16)
        s = lax.dot_general(qh, kh, (((1,), (1,)), ((), ())), preferred_element_type=F32) * scale
        s = jnp.where(mask4, s, -jnp.inf)
        sink = jnp.concatenate(
            [jnp.full((BLOCK, 1), 1.0, F32) * sink_ref[h * GROUP + g] for g in range(GROUP)], axis=0)
        m = jnp.maximum(jnp.max(s, axis=-1, keepdims=True), sink)
        pr = jnp.exp(s - m)
        den = jnp.sum(pr, axis=-1, keepdims=True) + jnp.exp(sink - m)
        o = jnp.dot(pr.astype(BF16), vh, preferred_element_type=F32) * (1.0 / den)
        for g in range(GROUP):
            hq = h * GROUP + g
            o_ref[0, :, hq * HEAD_DIM:(hq + 1) * HEAD_DIM] = o[g * BLOCK:(g + 1) * BLOCK]


def _rope_tables(l):
    half = ROT_DIM // 2
    inv = ROPE_THETA ** (-jnp.arange(half, dtype=F32) / half)
    ang = jnp.arange(l, dtype=F32)[:, None] * inv[None, :]
    cos, sin = jnp.cos(ang), jnp.sin(ang)
    ones = jnp.ones((l, HEAD_DIM - ROT_DIM), F32)
    return (jnp.concatenate([cos, cos, ones], axis=-1),
            jnp.concatenate([-sin, sin, 0.0 * ones], axis=-1))


def _swa_layer(x, norm_w, w_in, sink, w_out, final_norm=None):
    b, l, _ = x.shape
    nb = l // BLOCK
    x2d = x.reshape(b * l, D_MODEL)
    w_perm = jnp.concatenate([w_in[:, :Q_DIM], w_in[:, Q_DIM + 2 * KV_DIM:],
                              w_in[:, Q_DIM:Q_DIM + 2 * KV_DIM]], axis=1)
    proj = _norm_matmul(x2d, norm_w, w_perm)
    p3 = proj.reshape(b, l, 2 * Q_DIM + 2 * KV_DIM)
    ct, st = _rope_tables(l)
    kblk = 2 * Q_DIM // KV_DIM
    prv = lambda bb, i: jnp.maximum(i - 1, 0)
    nxt = lambda bb, i: jnp.minimum(i + 1, nb - 1)
    kv = lambda col, f: pl.BlockSpec((1, BLOCK, KV_DIM), lambda bb, i: (bb, f(bb, i), col))
    cur = lambda bb, i: i
    tab = lambda f: pl.BlockSpec((BLOCK, HEAD_DIM), lambda bb, i: (f(bb, i), 0))
    o = pl.pallas_call(
        functools.partial(_swa_kernel, nb=nb),
        grid=(b, nb),
        in_specs=[
            pl.BlockSpec((1, BLOCK, Q_DIM), lambda bb, i: (bb, i, 0)),
            kv(kblk, prv), kv(kblk, cur), kv(kblk, nxt),
            kv(kblk + 1, prv), kv(kblk + 1, cur), kv(kblk + 1, nxt),
            tab(cur), tab(cur), tab(prv), tab(prv), tab(nxt), tab(nxt),
            pl.BlockSpec(memory_space=pltpu.SMEM),
        ],
        out_specs=pl.BlockSpec((1, BLOCK, Q_DIM), lambda bb, i: (bb, i, 0)),
        out_shape=jax.ShapeDtypeStruct((b, l, Q_DIM), F32),
        compiler_params=_cparams(("arbitrary", "arbitrary")),
        name="swa_attention",
    )(p3, p3, p3, p3, p3, p3, p3, ct, st, ct, st, ct, st, sink)
    out = _out_matmul([o.reshape(b * l, D_INNER)], proj, 1, w_out, x2d, final_norm)
    return out.reshape(b, l, D_MODEL)


def _ml_qkv_kernel(xc_ref, xi_ref, wq_ref, wk_ref, wv_ref, wg_ref, bg_ref, q_ref, k_ref, v_ref,
                   g_ref, gt_ref):
    hp = lax.Precision.HIGHEST
    nt = D_INNER // LANES
    qs, ks, vs = [], [], []
    for j in range(nt):
        sl = slice(j * LANES, (j + 1) * LANES)
        xc = xc_ref[:, sl]
        qs.append(jnp.dot(xc, wq_ref[j], precision=hp, preferred_element_type=F32))
        ks.append(jnp.dot(xc, wk_ref[j], precision=hp, preferred_element_type=F32))
        vs.append(jnp.dot(xi_ref[:, sl], wv_ref[j], precision=hp, preferred_element_type=F32))
    qkv = jnp.concatenate(qs + ks + vs, axis=-1)
    gates = jnp.dot(qkv, wg_ref[...], precision=hp, preferred_element_type=F32) + bg_ref[...]
    g_ref[...] = gates
    gt_ref[...] = gates.T
    e = D_INNER
    q_ref[...] = qkv[:, :e].astype(BF16)
    k_ref[...] = (qkv[:, e:2 * e] * (ML_HEAD_DIM ** -0.5)).astype(BF16)
    v_ref[...] = qkv[:, 2 * e:].astype(BF16)


def _blockdiag_tiles(w):
    per = LANES // ML_QKV_BLOCK
    wt = w.reshape(D_INNER // LANES, per, ML_QKV_BLOCK, ML_QKV_BLOCK)
    eye = jnp.eye(per, dtype=w.dtype)
    return jnp.einsum("tnio,nm->tnimo", wt, eye).reshape(D_INNER // LANES, LANES, LANES)


def _ml_qkv(xc2d, proj, wq, wk, wv, w_gate, b_gate, tm=256):
    t = xc2d.shape[0]
    tm = min(tm, t)
    e = D_INNER
    ng = 4 * ML_HEADS
    nt = e // LANES
    wspec = pl.BlockSpec((nt, LANES, LANES), lambda i: (0, 0, 0))
    act = jax.ShapeDtypeStruct((t, e), BF16)
    return pl.pallas_call(
        _ml_qkv_kernel,
        grid=(t // tm,),
        in_specs=[
            pl.BlockSpec((tm, e), lambda i: (i, 0)),
            pl.BlockSpec((tm, e), lambda i: (i, 0)),
            wspec, wspec, wspec,
            pl.BlockSpec((3 * e, ng), lambda i: (0, 0)),
            pl.BlockSpec((1, ng), lambda i: (0, 0)),
        ],
        out_specs=[pl.BlockSpec((tm, e), lambda i: (i, 0))] * 3
        + [pl.BlockSpec((tm, ng), lambda i: (i, 0)), pl.BlockSpec((ng, tm), lambda i: (0, i))],
        out_shape=[act, act, act, jax.ShapeDtypeStruct((t, ng), F32),
                   jax.ShapeDtypeStruct((ng, t), F32)],
        compiler_params=_cparams(("arbitrary",)),
        name="mlstm_qkv_gates",
    )(xc2d, proj, _blockdiag_tiles(wq), _blockdiag_tiles(wk), _blockdiag_tiles(wv), w_gate,
      b_gate.reshape(1, ng))


def _log_sigmoid(x):
    return jnp.minimum(x, 0.0) - jnp.log(1.0 + jnp.exp(-jnp.abs(x)))


def _ml_scan_kernel(q_ref, k_ref, v_ref, g_ref, gt_ref, o_ref, c_ref, n_ref, m_ref):
    d = pl.program_id(1)
    hd = pl.program_id(2)
    c = pl.program_id(3)
    lc = ML_CHUNK
    ng = 4 * ML_HEADS

    @pl.when(c == 0)
    def _():
        c_ref[...] = jnp.zeros_like(c_ref)
        n_ref[...] = jnp.zeros_like(n_ref)
        m_ref[...] = jnp.zeros_like(m_ref)

    col_i = 2 * ML_HEADS * d + hd
    col_f = col_i + ML_HEADS
    lane = lax.broadcasted_iota(jnp.int32, (lc, ng), 1)
    g = g_ref[...]
    i_col = jnp.sum(jnp.where(lane == col_i, g, 0.0), axis=-1, keepdims=True)
    f_col = _log_sigmoid(jnp.sum(jnp.where(lane == col_f, g, 0.0), axis=-1, keepdims=True))
    sub = lax.broadcasted_iota(jnp.int32, (ng, lc), 0)
    gt = gt_ref[...]
    i_row = jnp.sum(jnp.where(sub == col_i, gt, 0.0), axis=0, keepdims=True)
    f_row = _log_sigmoid(jnp.sum(jnp.where(sub == col_f, gt, 0.0), axis=0, keepdims=True))

    sgn = 1 - 2 * d
    t_idx = lax.broadcasted_iota(jnp.int32, (lc, lc), 0)
    s_idx = lax.broadcasted_iota(jnp.int32, (lc, lc), 1)
    causal = (t_idx - s_idx) * sgn >= 0
    causal_t = (s_idx - t_idx) * sgn >= 0
    b_col = jnp.sum(jnp.where(causal, f_row, 0.0), axis=-1, keepdims=True)
    b_row = jnp.sum(jnp.where(causal_t, f_col, 0.0), axis=0, keepdims=True)
    m_prev = m_ref[...]
    dmat = jnp.where(causal, b_col - b_row + i_row, -jnp.inf)
    inter = b_col + m_prev
    m_out = jnp.maximum(inter, jnp.max(dmat, axis=-1, keepdims=True))
    w_intra = jnp.exp(dmat - m_out)
    w_state = jnp.exp(inter - m_out)

    q = q_ref[...]
    k = k_ref[...]
    v = v_ref[...]
    s = lax.dot_general(q, k, (((1,), (1,)), ((), ())), preferred_element_type=F32) * w_intra
    num = jnp.dot(s.astype(BF16), v, preferred_element_type=F32)
    num = num + w_state * jnp.dot(q, c_ref[...].astype(BF16), preferred_element_type=F32)
    qn = jnp.sum(q.astype(F32) * n_ref[...], axis=-1, keepdims=True)
    den = jnp.sum(s, axis=-1, keepdims=True) + w_state * qn
    o_ref[0] = num / jnp.maximum(jnp.abs(den), jnp.exp(-m_out))

    last = jnp.where(d == 0, lc - 1, 0)
    row = lax.broadcasted_iota(jnp.int32, (lc, 1), 0)
    bl = jnp.sum(f_row, axis=-1, keepdims=True)
    m_new = jnp.sum(jnp.where(row == last, m_out, 0.0), axis=0, keepdims=True)
    wk = jnp.exp(bl - b_col + i_col - m_new)
    decay = jnp.exp(bl + m_prev - m_new)
    vw = (v.astype(F32) * wk).astype(BF16)
    c_ref[...] = decay * c_ref[...] + lax.dot_general(
        k, vw, (((0,), (0,)), ((), ())), preferred_element_type=F32)
    n_ref[...] = decay * n_ref[...] + jnp.sum(k.astype(F32) * wk, axis=0, keepdims=True)
    m_ref[...] = m_new


def _ml_scan(q, k, v, gates, gates_t, b, l):
    t = b * l
    nc = l // ML_CHUNK
    dh = ML_HEAD_DIM
    ng = 4 * ML_HEADS
    chunk = lambda bb, d, c: bb * nc + c + d * (nc - 1 - 2 * c)
    qspec = pl.BlockSpec((ML_CHUNK, dh), lambda bb, d, h, c: (chunk(bb, d, c), h))
    return pl.pallas_call(
        _ml_scan_kernel,
        grid=(b, 2, ML_HEADS, nc),
        in_specs=[
            qspec, qspec, qspec,
            pl.BlockSpec((ML_CHUNK, ng), lambda bb, d, h, c: (chunk(bb, d, c), 0)),
            pl.BlockSpec((ng, ML_CHUNK), lambda bb, d, h, c: (0, chunk(bb, d, c))),
        ],
        out_specs=pl.BlockSpec((1, ML_CHUNK, dh), lambda bb, d, h, c: (d, chunk(bb, d, c), h)),
        out_shape=jax.ShapeDtypeStruct((2, t, D_INNER), F32),
        scratch_shapes=[pltpu.VMEM((dh, dh), F32), pltpu.VMEM((1, dh), F32), pltpu.VMEM((1, 1), F32)],
        compiler_params=_cparams(("arbitrary", "arbitrary", "arbitrary", "arbitrary")),
        name="mlstm_scan",
    )(q, k, v, gates, gates_t)


def _mlstm_layer(x, norm_w, w_in, conv_w, conv_b, wq, wk, wv, w_gate, b_gate, head_norm, skip,
                 w_out, final_norm=None):
    b, l, _ = x.shape
    x2d = x.reshape(b * l, D_MODEL)
    proj, xc = _norm_matmul_conv(x2d, norm_w, w_in, conv_w, conv_b, l, act=True, dual=True)
    q, k, v, gates, gates_t = _ml_qkv(xc, proj, wq, wk, wv, w_gate, b_gate)
    hs = _ml_scan(q, k, v, gates, gates_t, b, l)
    out = _out_matmul([(hs, 0), (hs, 1), xc], proj, 1, w_out, x2d, final_norm,
                      extra=(head_norm, skip))
    return out.reshape(b, l, D_MODEL)


def _trunk(x, layers, final_norm):
    kinds = (_hyena_layer, _swa_layer, _mlstm_layer, _hyena_layer)
    for i, (fn, params) in enumerate(zip(kinds, layers)):
        x = fn(x, *params, final_norm=final_norm if i == len(kinds) - 1 else None)
    return x


def kernel(x_prompt, x_sample, l0_norm, l0_hy_w_in, l0_hy_conv_w, l0_hy_conv_b, l0_hy_filt_w1, l0_hy_filt_b1, l0_hy_filt_w2, l0_hy_filt_b2, l0_hy_filt_w3, l0_hy_filt_b3, l0_hy_filt_freq, l0_hy_filt_wout, l0_hy_bias_d, l0_hy_w_out, l1_norm, l1_swa_w_in, l1_swa_sink, l1_swa_w_out, l2_norm, l2_ml_w_in, l2_ml_conv_w, l2_ml_conv_b, l2_ml_wq, l2_ml_wk, l2_ml_wv, l2_ml_w_gate, l2_ml_b_gate, l2_ml_head_norm, l2_ml_skip, l2_ml_w_out, l3_norm, l3_hy_w_in, l3_hy_conv_w, l3_hy_conv_b, l3_hy_filt_w1, l3_hy_filt_b1, l3_hy_filt_w2, l3_hy_filt_b2, l3_hy_filt_w3, l3_hy_filt_b3, l3_hy_filt_freq, l3_hy_filt_wout, l3_hy_bias_d, l3_hy_w_out, final_norm):
    layers = [
        (l0_norm, l0_hy_w_in, l0_hy_conv_w, l0_hy_conv_b, l0_hy_filt_w1, l0_hy_filt_b1, l0_hy_filt_w2,
         l0_hy_filt_b2, l0_hy_filt_w3, l0_hy_filt_b3, l0_hy_filt_freq, l0_hy_filt_wout, l0_hy_bias_d,
         l0_hy_w_out),
        (l1_norm, l1_swa_w_in, l1_swa_sink, l1_swa_w_out),
        (l2_norm, l2_ml_w_in, l2_ml_conv_w, l2_ml_conv_b, l2_ml_wq, l2_ml_wk, l2_ml_wv, l2_ml_w_gate,
         l2_ml_b_gate, l2_ml_head_norm, l2_ml_skip, l2_ml_w_out),
        (l3_norm, l3_hy_w_in, l3_hy_conv_w, l3_hy_conv_b, l3_hy_filt_w1, l3_hy_filt_b1, l3_hy_filt_w2,
         l3_hy_filt_b2, l3_hy_filt_w3, l3_hy_filt_b3, l3_hy_filt_freq, l3_hy_filt_wout, l3_hy_bias_d,
         l3_hy_w_out),
    ]
    return (_trunk(x_prompt, layers, final_norm), _trunk(x_sample, layers, final_norm))
```

```python
import cmath
import functools
import math

import jax
import jax.numpy as jnp
from jax import lax
from jax.experimental import pallas as pl
from jax.experimental.pallas import tpu as pltpu

F32 = jnp.float32
BF16 = jnp.bfloat16

D_MODEL = 1024
D_INNER = 2048
RMS_EPS = 1e-6
LN_EPS = 1e-6

LANES = 128
SUBLANES = 8
N_LANE_BLOCKS = D_INNER // LANES
VMEM_LIMIT = 56 * 1024 * 1024

HY_EMB = 33
HY_FILT = 64
HY_MIN_DECAY = math.log(1e-2) / 1.5
HY_MAX_DECAY = math.log(1e-2) / 0.3
FFT_N2 = 128
FFT_N2_RADICES = (8, 4, 4)
FFT_A_ROWS = 8192
FFT_MID_COLS = 4

HEAD_DIM = 128
N_Q_HEADS = D_INNER // HEAD_DIM
N_KV_HEADS = 4
GROUP = N_Q_HEADS // N_KV_HEADS
WINDOW = 128
BLOCK = 128
ROT_DIM = HEAD_DIM // 4
ROPE_THETA = 500000.0
Q_DIM = N_Q_HEADS * HEAD_DIM
KV_DIM = N_KV_HEADS * HEAD_DIM

ML_HEADS = 4
ML_HEAD_DIM = D_INNER // ML_HEADS
ML_QKV_BLOCK = 4
ML_CHUNK = 128


def _cparams(sem):
    return pltpu.CompilerParams(dimension_semantics=sem, vmem_limit_bytes=VMEM_LIMIT)


def _silu(x):
    return x * (1.0 / (1.0 + jnp.exp(-x)))


def _split_hi_lo(w):
    hi = w.astype(BF16)
    lo = (w - hi.astype(F32)).astype(BF16)
    return jnp.stack([hi, lo])


def _dot3(x, w_hi, w_lo):
    xh = x.astype(BF16)
    xl = (x - xh.astype(F32)).astype(BF16)
    d = lambda a, b: jnp.dot(a, b, preferred_element_type=F32)
    return d(xh, w_hi) + (d(xl, w_hi) + d(xh, w_lo))


def _add(a, b):
    if a is None:
        return b
    if b is None:
        return a
    return a + b


def _sub(a, b):
    if b is None:
        return a
    if a is None:
        return -b
    return a - b


def _scale(a, c):
    if a is None or c == 0.0:
        return None
    if c == 1.0:
        return a
    if c == -1.0:
        return -a
    return a * c


def _cadd(x, y):
    return (_add(x[0], y[0]), _add(x[1], y[1]))


def _csub(x, y):
    return (_sub(x[0], y[0]), _sub(x[1], y[1]))


def _snap(v):
    for t in (0.0, 1.0, -1.0):
        if abs(v - t) < 1e-14:
            return t
    return v


def _cmul_const(x, c):
    cr, ci = _snap(c.real), _snap(c.imag)
    re = _sub(_scale(x[0], cr), _scale(x[1], ci))
    im = _add(_scale(x[0], ci), _scale(x[1], cr))
    return (re, im)


def _cmul(x, y):
    xr, xi = x
    yr, yi = y
    if xi is None:
        return (xr * yr, xr * yi)
    return (xr * yr - xi * yi, xr * yi + xi * yr)


def _dense(x):
    re, im = x
    if re is None:
        re = jnp.zeros_like(im)
    if im is None:
        im = jnp.zeros_like(re)
    return (re, im)


def _small_dft(xs, sign):
    r = len(xs)
    if r == 1:
        return list(xs)
    ev = _small_dft(xs[0::2], sign)
    od = _small_dft(xs[1::2], sign)
    out = [None] * r
    for k in range(r // 2):
        t = _cmul_const(od[k], cmath.exp(sign * 2j * math.pi * k / r))
        out[k] = _cadd(ev[k], t)
        out[k + r // 2] = _csub(ev[k], t)
    return out


def _fft_plan(n, radices, sign):
    assert math.prod(radices) == n
    passes = []
    s, ncur = 1, n
    for r in radices:
        m = ncur // r
        bfs = []
        for p in range(m):
            tws = [cmath.exp(sign * 2j * math.pi * p * k / ncur) for k in range(r)]
            for q in range(s):
                ins = [q + s * (p + m * j) for j in range(r)]
                outs = [q + s * (r * p + k) for k in range(r)]
                bfs.append((ins, outs, tws))
        passes.append(bfs)
        ncur, s = m, s * r
    return passes


def _run_fft(plan, sign, load, store, temps):
    npass = len(plan)
    for pi, bfs in enumerate(plan):
        if pi == 0:
            ld = load
        else:
            tr, ti = temps[(pi - 1) % 2]
            ld = lambda i, tr=tr, ti=ti: (tr[i], ti[i])
        if pi == npass - 1:
            st = store
        else:
            tr2, ti2 = temps[pi % 2]

            def st(i, x, tr2=tr2, ti2=ti2):
                tr2[i] = x[0]
                ti2[i] = x[1]
        for ins, outs, tws in bfs:
            ys = _small_dft([ld(i) for i in ins], sign)
            for o, y, tw in zip(outs, ys, tws):
                st(o, _dense(_cmul_const(y, tw)))


def _radices(n):
    out = []
    while n > 1:
        r = 8 if n % 8 == 0 else (4 if n % 4 == 0 else 2)
        out.append(r)
        n //= r
    return tuple(out)


def _norm_matmul_kernel(x_ref, nw_ref, w_ref, o_ref, xn_ref):
    @pl.when(pl.program_id(1) == 0)
    def _():
        x = x_ref[...]
        inv = lax.rsqrt(jnp.mean(x * x, axis=-1, keepdims=True) + RMS_EPS)
        xn_ref[...] = ((x * inv) * nw_ref[...]).astype(BF16)

    o_ref[...] = jnp.dot(xn_ref[...], w_ref[...], preferred_element_type=F32)


def _norm_matmul(x2d, norm_w, w, tm=1024, tn=1024):
    t, d = x2d.shape
    n = w.shape[1]
    tm = min(tm, t)
    tn = min(tn, n)
    return pl.pallas_call(
        _norm_matmul_kernel,
        grid=(t // tm, n // tn),
        in_specs=[
            pl.BlockSpec((tm, d), lambda i, j: (i, 0)),
            pl.BlockSpec((1, d), lambda i, j: (0, 0)),
            pl.BlockSpec((d, tn), lambda i, j: (0, j)),
        ],
        out_specs=pl.BlockSpec((tm, tn), lambda i, j: (i, j)),
        out_shape=jax.ShapeDtypeStruct((t, n), F32),
        scratch_shapes=[pltpu.VMEM((tm, d), BF16)],
        compiler_params=_cparams(("arbitrary", "arbitrary")),
        name="norm_matmul",
    )(x2d, norm_w.reshape(1, d), w.astype(BF16))


HALO = 16


def _norm_matmul_conv_kernel(x_ref, xp_ref, xn_ref, nw_ref, w_ref, cw_ref, cb_ref, *rest,
                             tm, nconv, seq_tiles, act, dual):
    if dual:
        o_ref, oc_ref, xs_ref = rest
    else:
        o_ref, xs_ref = rest
    i = pl.program_id(0)
    j = pl.program_id(1)

    @pl.when(j == 0)
    def _():
        def nrm(x):
            inv = lax.rsqrt(jnp.mean(x * x, axis=-1, keepdims=True) + RMS_EPS)
            return (x * inv) * nw_ref[...]

        first = (i % seq_tiles) == 0
        last = (i % seq_tiles) == seq_tiles - 1
        xs_ref[0:HALO] = jnp.where(first, 0.0, nrm(xp_ref[...])).astype(BF16)
        xs_ref[HALO:HALO + tm] = nrm(x_ref[...]).astype(BF16)
        xs_ref[HALO + tm:] = jnp.where(last, 0.0, nrm(xn_ref[...])).astype(BF16)

    @pl.when(j < nconv)
    def _():
        rows = tm + 2 * HALO
        tn = w_ref.shape[1]
        nsplit = 2 if tn % (2 * LANES) == 0 else 1
        for c in range(nsplit):
            cs = slice(c * tn // nsplit, (c + 1) * tn // nsplit)
            acc = jnp.dot(xs_ref[...], w_ref[:, cs], preferred_element_type=F32)
            y = (acc * cw_ref[1:2, cs] + pltpu.roll(acc, 1, 0) * cw_ref[0:1, cs]
                 + pltpu.roll(acc, rows - 1, 0) * cw_ref[2:3, cs] + cb_ref[:, cs])
            y = y[HALO:HALO + tm]
            if act:
                y = _silu(y)
            if dual:
                oc_ref[:, cs] = y
                o_ref[:, cs] = acc[HALO:HALO + tm]
            else:
                o_ref[:, cs] = y

    @pl.when(j >= nconv)
    def _():
        o_ref[...] = jnp.dot(xs_ref[HALO:HALO + tm], w_ref[...], preferred_element_type=F32)


def _norm_matmul_conv(x2d, norm_w, w, conv_w, conv_b, seq_len, act, dual, tm=1024, tn=1024):
    t, d = x2d.shape
    n = w.shape[1]
    nc = conv_w.shape[1]
    tm = min(tm, seq_len)
    tn = min(tn, nc)
    nconv = nc // tn
    hb = tm // HALO
    nhb = t // HALO
    cj = lambda j: jnp.minimum(j, nconv - 1)
    in_specs = [
        pl.BlockSpec((tm, d), lambda i, j: (i, 0)),
        pl.BlockSpec((HALO, d), lambda i, j: (jnp.maximum(i * hb - 1, 0), 0)),
        pl.BlockSpec((HALO, d), lambda i, j: (jnp.minimum((i + 1) * hb, nhb - 1), 0)),
        pl.BlockSpec((1, d), lambda i, j: (0, 0)),
        pl.BlockSpec((d, tn), lambda i, j: (0, j)),
        pl.BlockSpec((3, tn), lambda i, j: (0, cj(j))),
        pl.BlockSpec((1, tn), lambda i, j: (0, cj(j))),
    ]
    out_specs = pl.BlockSpec((tm, tn), lambda i, j: (i, j))
    out_shape = jax.ShapeDtypeStruct((t, n), F32)
    if dual:
        out_specs = [out_specs, pl.BlockSpec((tm, tn), lambda i, j: (i, cj(j)))]
        out_shape = [out_shape, jax.ShapeDtypeStruct((t, nc), F32)]
    return pl.pallas_call(
        functools.partial(_norm_matmul_conv_kernel, tm=tm, nconv=nconv, seq_tiles=seq_len // tm,
                          act=act, dual=dual),
        grid=(t // tm, n // tn),
        in_specs=in_specs,
        out_specs=out_specs,
        out_shape=out_shape,
        scratch_shapes=[pltpu.VMEM((tm + 2 * HALO, d), BF16)],
        compiler_params=_cparams(("arbitrary", "arbitrary")),
        name="norm_matmul_conv",
    )(x2d, x2d, x2d, norm_w.reshape(1, d), w.astype(BF16), conv_w, conv_b.reshape(1, nc))


def _out_kernel(*refs, mode, final):
    if mode == "mlstm":
        hf_ref, hb_ref, xc_ref, g_ref, hn_ref, sk_ref, w_ref, r_ref = refs[:8]
        rest = refs[8:]
        h = hf_ref[...] + hb_ref[...]
        parts = []
        for hd in range(ML_HEADS):
            seg = h[:, hd * ML_HEAD_DIM:(hd + 1) * ML_HEAD_DIM]
            mu = jnp.mean(seg, axis=-1, keepdims=True)
            cen = seg - mu
            var = jnp.mean(cen * cen, axis=-1, keepdims=True)
            parts.append(cen * lax.rsqrt(var + LN_EPS))
        a = jnp.concatenate(parts, axis=-1) * hn_ref[...] + sk_ref[...] * xc_ref[...]
    else:
        a_ref, g_ref, w_ref, r_ref = refs[:4]
        rest = refs[4:]
        a = a_ref[...]
    if final:
        fn_ref, o_ref = rest
    else:
        (o_ref,) = rest
    y = jnp.dot((a * _silu(g_ref[...])).astype(BF16), w_ref[...], preferred_element_type=F32)
    x = r_ref[...] + y
    if final:
        inv = lax.rsqrt(jnp.mean(x * x, axis=-1, keepdims=True) + RMS_EPS)
        x = (x * inv) * fn_ref[...]
    o_ref[...] = x


def _out_matmul(acts, gate_arr, gate_blk, w, resid, final_norm=None, extra=(), tm=256):
    t = resid.shape[0]
    tm = min(tm, t)
    e = D_INNER
    mode = "mlstm" if len(acts) == 3 else "plain"
    in_specs, args = [], []
    for a in acts:
        if isinstance(a, tuple):
            arr, idx = a
            in_specs.append(pl.BlockSpec((None, tm, e), lambda i, idx=idx: (idx, i, 0)))
            args.append(arr)
        else:
            in_specs.append(pl.BlockSpec((tm, e), lambda i: (i, 0)))
            args.append(a)
    in_specs.append(pl.BlockSpec((tm, e), lambda i: (i, gate_blk)))
    args.append(gate_arr)
    for v in extra:
        in_specs.append(pl.BlockSpec((1, e), lambda i: (0, 0)))
        args.append(v.reshape(1, e))
    in_specs.append(pl.BlockSpec((e, D_MODEL), lambda i: (0, 0)))
    args.append(w.astype(BF16))
    in_specs.append(pl.BlockSpec((tm, D_MODEL), lambda i: (i, 0)))
    args.append(resid)
    if final_norm is not None:
        in_specs.append(pl.BlockSpec((1, D_MODEL), lambda i: (0, 0)))
        args.append(final_norm.reshape(1, D_MODEL))
    return pl.pallas_call(
        functools.partial(_out_kernel, mode=mode, final=final_norm is not None),
        grid=(t // tm,),
        in_specs=in_specs,
        out_specs=pl.BlockSpec((tm, D_MODEL), lambda i: (i, 0)),
        out_shape=jax.ShapeDtypeStruct((t, D_MODEL), F32),
        compiler_params=_cparams(("arbitrary",)),
        name="out_matmul_" + mode,
    )(*args)


def _filter_kernel(feat_ref, t_ref, w1_ref, b1_ref, w2_ref, b2_ref, w3_ref, b3_ref, fr_ref,
                   wo_ref, dl_ref, o_ref):
    fr = fr_ref[...]
    h = jnp.sin(fr * (_dot3(feat_ref[...], w1_ref[0], w1_ref[1]) + b1_ref[...]))
    h = jnp.sin(fr * (_dot3(h, w2_ref[0], w2_ref[1]) + b2_ref[...]))
    h = jnp.sin(fr * (_dot3(h, w3_ref[0], w3_ref[1]) + b3_ref[...]))
    tcol = t_ref[...]
    decay = jnp.exp(-tcol[:, 0:1] * dl_ref[...]) * tcol[:, 1:2]
    for f in range(2):
        k = _dot3(h, wo_ref[0, f], wo_ref[1, f])
        o_ref[f] = k * decay


def _hyena_two_sided_filters(l, fw1, fb1, fw2, fb2, fw3, fb3, ffreq, fwout, tr=512):
    n = 2 * l
    e = D_INNER
    tr = min(tr, l)
    idx = jnp.arange(n)
    src = jnp.where(idx < l, idx, n - idx)
    valid = (idx != l).astype(F32)
    src = jnp.where(idx == l, 0, src)
    t_all = jnp.linspace(0.0, 1.0, l, dtype=F32)
    bands = (HY_EMB - 1) // 2
    ang = 2.0 * math.pi * jnp.arange(l, dtype=F32)[:, None] / l
    fb = jnp.linspace(1e-4, bands - 1, bands, dtype=F32)[None, :]
    feat = jnp.concatenate([t_all[:, None], jnp.cos(fb * ang), -jnp.sin(fb * ang)], axis=-1)
    feat = jnp.pad(feat, ((0, 0), (0, LANES - HY_EMB)))[src]
    tcol = jnp.stack([t_all[src], valid], axis=-1)
    w1p = jnp.pad(fw1, ((0, LANES - HY_EMB), (0, 0)))
    deltas = jnp.abs(jnp.linspace(HY_MIN_DECAY, HY_MAX_DECAY, e, dtype=F32)).reshape(1, e)
    wo = _split_hi_lo(fwout.reshape(HY_FILT, 4, e).transpose(1, 0, 2))
    nhalf = l // tr
    row = lambda v: v.reshape(1, HY_FILT)
    full = lambda shp: pl.BlockSpec(shp, lambda i: tuple(0 for _ in shp))
    return pl.pallas_call(
        _filter_kernel,
        grid=(n // tr,),
        in_specs=[
            pl.BlockSpec((tr, LANES), lambda i: (i, 0)),
            pl.BlockSpec((tr, 2), lambda i: (i, 0)),
            full((2, LANES, HY_FILT)), full((1, HY_FILT)),
            full((2, HY_FILT, HY_FILT)), full((1, HY_FILT)),
            full((2, HY_FILT, HY_FILT)), full((1, HY_FILT)),
            full((1, HY_FILT)),
            pl.BlockSpec((2, 2, HY_FILT, e), lambda i: (0, i // nhalf, 0, 0)),
            full((1, e)),
        ],
        out_specs=pl.BlockSpec((2, tr, e), lambda i: (0, i, 0)),
        out_shape=jax.ShapeDtypeStruct((2, n, e), F32),
        compiler_params=_cparams(("arbitrary",)),
        name="hyena_filter",
    )(feat, tcol, _split_hi_lo(w1p), row(fb1), _split_hi_lo(fw2), row(fb2), _split_hi_lo(fw3),
      row(fb3), row(ffreq), wo, deltas)


def _fft_a_kernel(x_ref, zr_ref, zi_ref, *scratch, n1, nj, packed):
    plan = _fft_plan(n1, _radices(n1), -1)
    temps = [(scratch[0], scratch[1]), (scratch[2], scratch[3])]
    fr, fi = scratch[4], scratch[5]

    def body(j, carry):
        rows = pl.ds(pl.multiple_of(j * SUBLANES, SUBLANES), SUBLANES)

        def load(i):
            if packed:
                if i >= n1 // 2:
                    return (None, None)
                return (x_ref[0, 0, i, rows, :], x_ref[1, 0, i, rows, :])
            return (x_ref[0, i, rows, :], None)

        def store(i, x):
            fr[i] = x[0]
            fi[i] = x[1]

        _run_fft(plan, -1, load, store, temps)
        for a in range(n1 // SUBLANES):
            sl = slice(a * SUBLANES, (a + 1) * SUBLANES)
            zr_ref[0, a, 0, rows] = jnp.swapaxes(fr[sl], 0, 1)
            zi_ref[0, a, 0, rows] = jnp.swapaxes(fi[sl], 0, 1)
        return carry

    lax.fori_loop(0, nj, body, 0)


def _fft_a(x, n1, n2, packed, lane_blk0=0):
    nl = N_LANE_BLOCKS
    bn2 = min(max(FFT_A_ROWS // n1, SUBLANES), n2)
    if packed:
        p = x.shape[1]
        in_spec = pl.BlockSpec((2, 1, n1 // 2, bn2, LANES),
                               lambda pp, lb, j: (0, pp, 0, j, lane_blk0 + lb))
    else:
        p = x.shape[0]
        in_spec = pl.BlockSpec((1, n1, bn2, LANES), lambda pp, lb, j: (pp, 0, j, lb))
    out_spec = pl.BlockSpec((1, n1 // SUBLANES, 1, bn2, SUBLANES, LANES),
                            lambda pp, lb, j: (pp, 0, lb, j, 0, 0))
    shp = jax.ShapeDtypeStruct((p, n1 // SUBLANES, nl, n2, SUBLANES, LANES), F32)
    return pl.pallas_call(
        functools.partial(_fft_a_kernel, n1=n1, nj=bn2 // SUBLANES, packed=packed),
        grid=(p, nl, n2 // bn2),
        in_specs=[in_spec],
        out_specs=[out_spec, out_spec],
        out_shape=[shp, shp],
        scratch_shapes=[pltpu.VMEM((n1, SUBLANES, LANES), F32) for _ in range(6)],
        compiler_params=_cparams(("arbitrary", "arbitrary", "arbitrary")),
        name="fft_a_packed" if packed else "fft_a_real",
    )(x)


def _fft_mid_kernel(*refs, n2, ncol, nl, conv, scale):
    if conv:
        zr_ref, zi_ref, kr_ref, ki_ref, twr_ref, twi_ref, or_ref, oi_ref = refs[:8]
        scratch = refs[8:]
    else:
        zr_ref, zi_ref, twr_ref, twi_ref, or_ref, oi_ref = refs[:6]
        scratch = refs[6:]
    fwd = _fft_plan(n2, FFT_N2_RADICES, -1)
    inv = _fft_plan(n2, FFT_N2_RADICES, +1)
    temps = [(scratch[0], scratch[1]), (scratch[2], scratch[3])]
    sr, si = scratch[4], scratch[5]

    def body(c, carry):
        i = c // nl
        g = c % nl

        def tw(r):
            return (twr_ref[i, r], twi_ref[i, r])

        def load(r):
            return _cmul((zr_ref[0, i, g, r], zi_ref[0, i, g, r]), tw(r))

        if conv:
            def store_spec(r, x):
                y = _cmul(x, (kr_ref[0, i, g, r], ki_ref[0, i, g, r]))
                sr[r] = y[0]
                si[r] = y[1]

            _run_fft(fwd, -1, load, store_spec, temps)

            def store_out(r, x):
                t = tw(r)
                y = _cmul(x, (t[0], -t[1]))
                or_ref[0, i, g, r] = y[0]
                oi_ref[0, i, g, r] = y[1]

            _run_fft(inv, +1, lambda r: (sr[r], si[r]), store_out, temps)
        else:
            def store_spec(r, x):
                or_ref[0, i, g, r] = x[0] * scale
                oi_ref[0, i, g, r] = x[1] * scale

            _run_fft(fwd, -1, load, store_spec, temps)
        return carry

    lax.fori_loop(0, ncol, body, 0)


def _twiddle_table(n1, n2):
    n = n1 * n2
    k1 = jnp.arange(n1).reshape(n1 // SUBLANES, 1, SUBLANES)
    m = (k1 * jnp.arange(n2).reshape(1, n2, 1)) % n
    ang = m.astype(F32) * (-2.0 * math.pi / n)
    shape = (n1 // SUBLANES, n2, SUBLANES, LANES)
    return (jnp.broadcast_to(jnp.cos(ang)[..., None], shape),
            jnp.broadcast_to(jnp.sin(ang)[..., None], shape))


def _fft_mid(zr, zi, tw, filt=None, scale=1.0):
    p, na, nl, n2 = zr.shape[:4]
    bl = min(FFT_MID_COLS, nl)
    zspec = pl.BlockSpec((1, 1, bl, n2, SUBLANES, LANES), lambda i, g, pp: (pp, i, g, 0, 0, 0))
    twspec = pl.BlockSpec((1, n2, SUBLANES, LANES), lambda i, g, pp: (i, 0, 0, 0))
    conv = filt is not None
    in_specs = [zspec, zspec]
    args = [zr, zi]
    if conv:
        fidx = filt[2]
        kspec = pl.BlockSpec((1, 1, bl, n2, SUBLANES, LANES), lambda i, g, pp: (fidx, i, g, 0, 0, 0))
        in_specs += [kspec, kspec]
        args += [filt[0], filt[1]]
    in_specs += [twspec, twspec]
    args += [tw[0], tw[1]]
    shp = jax.ShapeDtypeStruct(zr.shape, F32)
    return pl.pallas_call(
        functools.partial(_fft_mid_kernel, n2=n2, ncol=bl, nl=bl, conv=conv, scale=scale),
        grid=(na, nl // bl, p),
        in_specs=in_specs,
        out_specs=[zspec, zspec],
        out_shape=[shp, shp],
        scratch_shapes=[pltpu.VMEM((n2, SUBLANES, LANES), F32) for _ in range(6)],
        compiler_params=_cparams(("arbitrary", "arbitrary", "arbitrary")),
        name="fft_mid_conv" if conv else "fft_mid_spec",
    )(*args)


def _fft_ainv_kernel(zr_ref, zi_ref, u_ref, x_ref, d_ref, o_ref, *scratch, n1, nj):
    plan = _fft_plan(n1, _radices(n1), +1)
    temps = [(scratch[0], scratch[1]), (scratch[2], scratch[3])]
    lr, li = scratch[4], scratch[5]
    d = d_ref[...]

    def body(j, carry):
        rows = pl.ds(pl.multiple_of(j * SUBLANES, SUBLANES), SUBLANES)
        for a in range(n1 // SUBLANES):
            sl = slice(a * SUBLANES, (a + 1) * SUBLANES)
            lr[sl] = jnp.swapaxes(zr_ref[0, a, 0, rows], 0, 1)
            li[sl] = jnp.swapaxes(zi_ref[0, a, 0, rows], 0, 1)

        def store(i, y):
            if i < n1 // 2:
                o_ref[0, 0, i, rows, :] = x_ref[0, 0, i, rows, :] * (y[0] + u_ref[0, 0, i, rows, :] * d)
                o_ref[1, 0, i, rows, :] = x_ref[1, 0, i, rows, :] * (y[1] + u_ref[1, 0, i, rows, :] * d)

        _run_fft(plan, +1, lambda i: (lr[i], li[i]), store, temps)
        return carry

    lax.fori_loop(0, nj, body, 0)


def _fft_ainv(zr, zi, u, u_blk0, x, x_blk0, d, d_row):
    p, na, nl, n2 = zr.shape[:4]
    n1 = na * SUBLANES
    bn2 = min(max(FFT_A_ROWS // n1, SUBLANES), n2)
    zspec = pl.BlockSpec((1, na, 1, bn2, SUBLANES, LANES), lambda pp, lb, j: (pp, 0, lb, j, 0, 0))

    def sspec(blk0):
        return pl.BlockSpec((2, 1, n1 // 2, bn2, LANES), lambda pp, lb, j: (0, pp, 0, j, blk0 + lb))

    return pl.pallas_call(
        functools.partial(_fft_ainv_kernel, n1=n1, nj=bn2 // SUBLANES),
        grid=(p, nl, n2 // bn2),
        in_specs=[zspec, zspec, sspec(u_blk0), sspec(x_blk0),
                  pl.BlockSpec((1, LANES), lambda pp, lb, j: (0, lb))],
        out_specs=sspec(0),
        out_shape=jax.ShapeDtypeStruct((2, p, n1 // 2, n2, D_INNER), F32),
        scratch_shapes=[pltpu.VMEM((n1, SUBLANES, LANES), F32) for _ in range(6)],
        compiler_params=_cparams(("arbitrary", "arbitrary", "arbitrary")),
        name="fft_a_inv",
    )(zr, zi, u, x, d[d_row].reshape(1, D_INNER))


def _hyena_layer(x, norm_w, w_in, conv_w, conv_b, fw1, fb1, fw2, fb2, fw3, fb3, ffreq, fwout,
                 bias_d, w_out, final_norm=None):
    b, l, _ = x.shape
    e = D_INNER
    n = 2 * l
    n2 = FFT_N2
    n1 = n // n2
    p = b // 2
    nl = N_LANE_BLOCKS
    x2d = x.reshape(b * l, D_MODEL)
    proj = _norm_matmul_conv(x2d, norm_w, w_in, conv_w, conv_b, l, act=False, dual=False)
    pv = proj.reshape(2, p, n1 // 2, n2, 4 * e)

    tw = _twiddle_table(n1, n2)
    kt = _hyena_two_sided_filters(l, fw1, fb1, fw2, fb2, fw3, fb3, ffreq, fwout)
    kr, ki = _fft_a(kt.reshape(2, n1, n2, e), n1, n2, packed=False)
    kfr, kfi = _fft_mid(kr, ki, tw, scale=1.0 / n)

    zr, zi = _fft_a(pv, n1, n2, packed=True)
    zr, zi = _fft_mid(zr, zi, tw, filt=(kfr, kfi, 0))
    z1 = _fft_ainv(zr, zi, pv, 0, pv, nl, bias_d, 0)
    zr, zi = _fft_a(z1, n1, n2, packed=True)
    zr, zi = _fft_mid(zr, zi, tw, filt=(kfr, kfi, 1))
    z2 = _fft_ainv(zr, zi, z1, 0, pv, 2 * nl, bias_d, 1)
    out = _out_matmul([z2.reshape(b * l, e)], proj, 3, w_out, x2d, final_norm)
    return out.reshape(b, l, D_MODEL)


def _swa_kernel(q_ref, kp_ref, kc_ref, kn_ref, vp_ref, vc_ref, vn_ref,
                cq_ref, sq_ref, ckp_ref, skp_ref, ckn_ref, skn_ref, sink_ref, o_ref, *, nb):
    n = pl.program_id(1)
    lane = lax.broadcasted_iota(jnp.int32, (BLOCK, HEAD_DIM), 1)
    half = ROT_DIM // 2

    def rope(x, c, s):
        sw = jnp.where(lane < half, pltpu.roll(x, HEAD_DIM - half, 1), pltpu.roll(x, half, 1))
        return x * c + sw * s

    cq, sq = cq_ref[...], sq_ref[...]
    cks = (ckp_ref[...], cq, ckn_ref[...])
    sks = (skp_ref[...], sq, skn_ref[...])
    qi = lax.broadcasted_iota(jnp.int32, (BLOCK, 3 * BLOCK), 0)
    kj = lax.broadcasted_iota(jnp.int32, (BLOCK, 3 * BLOCK), 1) - BLOCK
    mask = jnp.abs(qi - kj) <= WINDOW
    mask = mask & ((kj >= 0) | (n > 0)) & ((kj < BLOCK) | (n < nb - 1))
    mask4 = jnp.concatenate([mask] * GROUP, axis=0)
    scale = HEAD_DIM ** -0.5
    for h in range(N_KV_HEADS):
        sl = slice(h * HEAD_DIM, (h + 1) * HEAD_DIM)
        kparts = [rope(r[0, :, sl], c, s) for r, c, s in zip((kp_ref, kc_ref, kn_ref), cks, sks)]
        kh = jnp.concatenate(kparts, axis=0).astype(BF16)
        vh = jnp.concatenate([vp_ref[0, :, sl], vc_ref[0, :, sl], vn_ref[0, :, sl]],
                             axis=0).astype(BF16)
        qs = []
        for g in range(GROUP):
            hq = h * GROUP + g
            qs.append(rope(q_ref[0, :, hq * HEAD_DIM:(hq + 1) * HEAD_DIM], cq, sq))
        qh = jnp.concatenate(qs, axis=0).astype(BF16)
        s = lax.dot_general(qh, kh, (((1,), (1,)), ((), ())), preferred_element_type=F32) * scale
        s = jnp.where(mask4, s, -jnp.inf)
        sink = jnp.concatenate(
            [jnp.full((BLOCK, 1), 1.0, F32) * sink_ref[h * GROUP + g] for g in range(GROUP)], axis=0)
        m = jnp.maximum(jnp.max(s, axis=-1, keepdims=True), sink)
        pr = jnp.exp(s - m)
        den = jnp.sum(pr, axis=-1, keepdims=True) + jnp.exp(sink - m)
        o = jnp.dot(pr.astype(BF16), vh, preferred_element_type=F32) * (1.0 / den)
        for g in range(GROUP):
            hq = h * GROUP + g
            o_ref[0, :, hq * HEAD_DIM:(hq + 1) * HEAD_DIM] = o[g * BLOCK:(g + 1) * BLOCK]


def _rope_tables(l):
    half = ROT_DIM // 2
    inv = ROPE_THETA ** (-jnp.arange(half, dtype=F32) / half)
    ang = jnp.arange(l, dtype=F32)[:, None] * inv[None, :]
    cos, sin = jnp.cos(ang), jnp.sin(ang)
    ones = jnp.ones((l, HEAD_DIM - ROT_DIM), F32)
    return (jnp.concatenate([cos, cos, ones], axis=-1),
            jnp.concatenate([-sin, sin, 0.0 * ones], axis=-1))


def _swa_layer(x, norm_w, w_in, sink, w_out, final_norm=None):
    b, l, _ = x.shape
    nb = l // BLOCK
    x2d = x.reshape(b * l, D_MODEL)
    w_perm = jnp.concatenate([w_in[:, :Q_DIM], w_in[:, Q_DIM + 2 * KV_DIM:],
                              w_in[:, Q_DIM:Q_DIM + 2 * KV_DIM]], axis=1)
    proj = _norm_matmul(x2d, norm_w, w_perm)
    p3 = proj.reshape(b, l, 2 * Q_DIM + 2 * KV_DIM)
    ct, st = _rope_tables(l)
    kblk = 2 * Q_DIM // KV_DIM
    prv = lambda bb, i: jnp.maximum(i - 1, 0)
    nxt = lambda bb, i: jnp.minimum(i + 1, nb - 1)
    kv = lambda col, f: pl.BlockSpec((1, BLOCK, KV_DIM), lambda bb, i: (bb, f(bb, i), col))
    cur = lambda bb, i: i
    tab = lambda f: pl.BlockSpec((BLOCK, HEAD_DIM), lambda bb, i: (f(bb, i), 0))
    o = pl.pallas_call(
        functools.partial(_swa_kernel, nb=nb),
        grid=(b, nb),
        in_specs=[
            pl.BlockSpec((1, BLOCK, Q_DIM), lambda bb, i: (bb, i, 0)),
            kv(kblk, prv), kv(kblk, cur), kv(kblk, nxt),
            kv(kblk + 1, prv), kv(kblk + 1, cur), kv(kblk + 1, nxt),
            tab(cur), tab(cur), tab(prv), tab(prv), tab(nxt), tab(nxt),
            pl.BlockSpec(memory_space=pltpu.SMEM),
        ],
        out_specs=pl.BlockSpec((1, BLOCK, Q_DIM), lambda bb, i: (bb, i, 0)),
        out_shape=jax.ShapeDtypeStruct((b, l, Q_DIM), F32),
        compiler_params=_cparams(("arbitrary", "arbitrary")),
        name="swa_attention",
    )(p3, p3, p3, p3, p3, p3, p3, ct, st, ct, st, ct, st, sink)
    out = _out_matmul([o.reshape(b * l, D_INNER)], proj, 1, w_out, x2d, final_norm)
    return out.reshape(b, l, D_MODEL)


def _ml_qkv_kernel(xc_ref, xi_ref, wq_ref, wk_ref, wv_ref, wg_ref, bg_ref, q_ref, k_ref, v_ref,
                   g_ref, gt_ref):
    nt = D_INNER // LANES
    qs, ks, vs = [], [], []
    for j in range(nt):
        sl = slice(j * LANES, (j + 1) * LANES)
        xc = xc_ref[:, sl]
        qs.append(_dot3(xc, wq_ref[0, j], wq_ref[1, j]))
        ks.append(_dot3(xc, wk_ref[0, j], wk_ref[1, j]))
        vs.append(_dot3(xi_ref[:, sl], wv_ref[0, j], wv_ref[1, j]))
    qkv = jnp.concatenate(qs + ks + vs, axis=-1)
    gates = _dot3(qkv, wg_ref[0], wg_ref[1]) + bg_ref[...]
    g_ref[...] = gates
    gt_ref[...] = gates.T
    e = D_INNER
    q_ref[...] = qkv[:, :e].astype(BF16)
    k_ref[...] = (qkv[:, e:2 * e] * (ML_HEAD_DIM ** -0.5)).astype(BF16)
    v_ref[...] = qkv[:, 2 * e:].astype(BF16)


def _blockdiag_tiles(w):
    per = LANES // ML_QKV_BLOCK
    wt = w.reshape(D_INNER // LANES, per, ML_QKV_BLOCK, ML_QKV_BLOCK)
    eye = jnp.eye(per, dtype=w.dtype)
    return jnp.einsum("tnio,nm->tnimo", wt, eye).reshape(D_INNER // LANES, LANES, LANES)


def _ml_qkv(xc2d, proj, wq, wk, wv, w_gate, b_gate, tm=256):
    t = xc2d.shape[0]
    tm = min(tm, t)
    e = D_INNER
    ng = 4 * ML_HEADS
    nt = e // LANES
    wspec = pl.BlockSpec((2, nt, LANES, LANES), lambda i: (0, 0, 0, 0))
    act = jax.ShapeDtypeStruct((t, e), BF16)
    tiles = lambda w: _split_hi_lo(_blockdiag_tiles(w))
    return pl.pallas_call(
        _ml_qkv_kernel,
        grid=(t // tm,),
        in_specs=[
            pl.BlockSpec((tm, e), lambda i: (i, 0)),
            pl.BlockSpec((tm, e), lambda i: (i, 0)),
            wspec, wspec, wspec,
            pl.BlockSpec((2, 3 * e, ng), lambda i: (0, 0, 0)),
            pl.BlockSpec((1, ng), lambda i: (0, 0)),
        ],
        out_specs=[pl.BlockSpec((tm, e), lambda i: (i, 0))] * 3
        + [pl.BlockSpec((tm, ng), lambda i: (i, 0)), pl.BlockSpec((ng, tm), lambda i: (0, i))],
        out_shape=[act, act, act, jax.ShapeDtypeStruct((t, ng), F32),
                   jax.ShapeDtypeStruct((ng, t), F32)],
        compiler_params=_cparams(("arbitrary",)),
        name="mlstm_qkv_gates",
    )(xc2d, proj, tiles(wq), tiles(wk), tiles(wv), _split_hi_lo(w_gate), b_gate.reshape(1, ng))


def _log_sigmoid(x):
    return jnp.minimum(x, 0.0) - jnp.log(1.0 + jnp.exp(-jnp.abs(x)))


def _ml_chunk(d, hd, q_ref, k_ref, v_ref, g, gt, o_ref, c_ref, n_ref, m_ref):
    lc = ML_CHUNK
    ng = 4 * ML_HEADS
    dh = ML_HEAD_DIM
    hs = slice(hd * dh, (hd + 1) * dh)
    col_i = 2 * ML_HEADS * d + hd
    col_f = col_i + ML_HEADS
    lane = lax.broadcasted_iota(jnp.int32, (lc, ng), 1)
    i_col = jnp.sum(jnp.where(lane == col_i, g, 0.0), axis=-1, keepdims=True)
    f_col = _log_sigmoid(jnp.sum(jnp.where(lane == col_f, g, 0.0), axis=-1, keepdims=True))
    sub = lax.broadcasted_iota(jnp.int32, (ng, lc), 0)
    i_row = jnp.sum(jnp.where(sub == col_i, gt, 0.0), axis=0, keepdims=True)
    f_row = _log_sigmoid(jnp.sum(jnp.where(sub == col_f, gt, 0.0), axis=0, keepdims=True))

    t_idx = lax.broadcasted_iota(jnp.int32, (lc, lc), 0)
    s_idx = lax.broadcasted_iota(jnp.int32, (lc, lc), 1)
    if d == 0:
        causal = s_idx <= t_idx
        causal_t = t_idx <= s_idx
    else:
        causal = s_idx >= t_idx
        causal_t = t_idx >= s_idx
    b_col = jnp.sum(jnp.where(causal, f_row, 0.0), axis=-1, keepdims=True)
    b_row = jnp.sum(jnp.where(causal_t, f_col, 0.0), axis=0, keepdims=True)
    m_prev = m_ref[d, hd]
    dmat = jnp.where(causal, b_col - b_row + i_row, -jnp.inf)
    inter = b_col + m_prev
    m_out = jnp.maximum(inter, jnp.max(dmat, axis=-1, keepdims=True))
    w_intra = jnp.exp(dmat - m_out)
    w_state = jnp.exp(inter - m_out)

    q = q_ref[:, hs]
    k = k_ref[:, hs]
    v = v_ref[:, hs]
    cst = c_ref[d, hd]
    s = lax.dot_general(q, k, (((1,), (1,)), ((), ())), preferred_element_type=F32) * w_intra
    num = jnp.dot(s.astype(BF16), v, preferred_element_type=F32)
    num = num + w_state * jnp.dot(q, cst.astype(BF16), preferred_element_type=F32)
    qn = jnp.sum(q.astype(F32) * n_ref[d, hd], axis=-1, keepdims=True)
    den = jnp.sum(s, axis=-1, keepdims=True) + w_state * qn
    o_ref[:, hs] = num / jnp.maximum(jnp.abs(den), jnp.exp(-m_out))

    last = lc - 1 if d == 0 else 0
    bl = jnp.sum(f_row, axis=-1, keepdims=True)
    m_new = m_out[last:last + 1]
    wk = jnp.exp(bl - b_col + i_col - m_new)
    decay = jnp.exp(bl + m_prev - m_new)
    vw = (v.astype(F32) * wk).astype(BF16)
    c_ref[d, hd] = decay * cst + lax.dot_general(
        k, vw, (((0,), (0,)), ((), ())), preferred_element_type=F32)
    n_ref[d, hd] = decay * n_ref[d, hd] + jnp.sum(k.astype(F32) * wk, axis=0, keepdims=True)
    m_ref[d, hd] = m_new


def _ml_scan_kernel(qf_ref, kf_ref, vf_ref, gf_ref, gtf_ref, qb_ref, kb_ref, vb_ref, gb_ref, gtb_ref,
                    of_ref, ob_ref, c_ref, n_ref, m_ref):
    @pl.when(pl.program_id(1) == 0)
    def _():
        c_ref[...] = jnp.zeros_like(c_ref)
        n_ref[...] = jnp.zeros_like(n_ref)
        m_ref[...] = jnp.zeros_like(m_ref)

    dirs = ((qf_ref, kf_ref, vf_ref, gf_ref, gtf_ref, of_ref),
            (qb_ref, kb_ref, vb_ref, gb_ref, gtb_ref, ob_ref))
    for d, (q_ref, k_ref, v_ref, g_ref, gt_ref, o_ref) in enumerate(dirs):
        g = g_ref[...]
        gt = gt_ref[...]
        for hd in range(ML_HEADS):
            _ml_chunk(d, hd, q_ref, k_ref, v_ref, g, gt, o_ref, c_ref, n_ref, m_ref)


def _ml_scan(q, k, v, gates, gates_t, b, l):
    t = b * l
    nc = l // ML_CHUNK
    dh = ML_HEAD_DIM
    e = D_INNER
    ng = 4 * ML_HEADS
    fw = lambda bb, c: bb * nc + c
    bw = lambda bb, c: bb * nc + nc - 1 - c

    def specs(ch):
        qspec = pl.BlockSpec((ML_CHUNK, e), lambda bb, c: (ch(bb, c), 0))
        return [qspec, qspec, qspec,
                pl.BlockSpec((ML_CHUNK, ng), lambda bb, c: (ch(bb, c), 0)),
                pl.BlockSpec((ng, ML_CHUNK), lambda bb, c: (0, ch(bb, c)))]

    out = jax.ShapeDtypeStruct((t, e), F32)
    return pl.pallas_call(
        _ml_scan_kernel,
        grid=(b, nc),
        in_specs=specs(fw) + specs(bw),
        out_specs=[pl.BlockSpec((ML_CHUNK, e), lambda bb, c: (fw(bb, c), 0)),
                   pl.BlockSpec((ML_CHUNK, e), lambda bb, c: (bw(bb, c), 0))],
        out_shape=[out, out],
        scratch_shapes=[pltpu.VMEM((2, ML_HEADS, dh, dh), F32), pltpu.VMEM((2, ML_HEADS, 1, dh), F32),
                        pltpu.VMEM((2, ML_HEADS, 1, 1), F32)],
        compiler_params=_cparams(("arbitrary", "arbitrary")),
        name="mlstm_scan",
    )(q, k, v, gates, gates_t, q, k, v, gates, gates_t)


def _mlstm_layer(x, norm_w, w_in, conv_w, conv_b, wq, wk, wv, w_gate, b_gate, head_norm, skip,
                 w_out, final_norm=None):
    b, l, _ = x.shape
    x2d = x.reshape(b * l, D_MODEL)
    proj, xc = _norm_matmul_conv(x2d, norm_w, w_in, conv_w, conv_b, l, act=True, dual=True)
    q, k, v, gates, gates_t = _ml_qkv(xc, proj, wq, wk, wv, w_gate, b_gate)
    h_fwd, h_bwd = _ml_scan(q, k, v, gates, gates_t, b, l)
    out = _out_matmul([h_fwd, h_bwd, xc], proj, 1, w_out, x2d, final_norm,
                      extra=(head_norm, skip))
    return out.reshape(b, l, D_MODEL)


def _trunk(x, layers, final_norm):
    kinds = (_hyena_layer, _swa_layer, _mlstm_layer, _hyena_layer)
    for i, (fn, params) in enumerate(zip(kinds, layers)):
        x = fn(x, *params, final_norm=final_norm if i == len(kinds) - 1 else None)
    return x


def kernel(x_prompt, x_sample, l0_norm, l0_hy_w_in, l0_hy_conv_w, l0_hy_conv_b, l0_hy_filt_w1, l0_hy_filt_b1, l0_hy_filt_w2, l0_hy_filt_b2, l0_hy_filt_w3, l0_hy_filt_b3, l0_hy_filt_freq, l0_hy_filt_wout, l0_hy_bias_d, l0_hy_w_out, l1_norm, l1_swa_w_in, l1_swa_sink, l1_swa_w_out, l2_norm, l2_ml_w_in, l2_ml_conv_w, l2_ml_conv_b, l2_ml_wq, l2_ml_wk, l2_ml_wv, l2_ml_w_gate, l2_ml_b_gate, l2_ml_head_norm, l2_ml_skip, l2_ml_w_out, l3_norm, l3_hy_w_in, l3_hy_conv_w, l3_hy_conv_b, l3_hy_filt_w1, l3_hy_filt_b1, l3_hy_filt_w2, l3_hy_filt_b2, l3_hy_filt_w3, l3_hy_filt_b3, l3_hy_filt_freq, l3_hy_filt_wout, l3_hy_bias_d, l3_hy_w_out, final_norm):
    layers = [
        (l0_norm, l0_hy_w_in, l0_hy_conv_w, l0_hy_conv_b, l0_hy_filt_w1, l0_hy_filt_b1, l0_hy_filt_w2,
         l0_hy_filt_b2, l0_hy_filt_w3, l0_hy_filt_b3, l0_hy_filt_freq, l0_hy_filt_wout, l0_hy_bias_d,
         l0_hy_w_out),
        (l1_norm, l1_swa_w_in, l1_swa_sink, l1_swa_w_out),
        (l2_norm, l2_ml_w_in, l2_ml_conv_w, l2_ml_conv_b, l2_ml_wq, l2_ml_wk, l2_ml_wv, l2_ml_w_gate,
         l2_ml_b_gate, l2_ml_head_norm, l2_ml_skip, l2_ml_w_out),
        (l3_norm, l3_hy_w_in, l3_hy_conv_w, l3_hy_conv_b, l3_hy_filt_w1, l3_hy_filt_b1, l3_hy_filt_w2,
         l3_hy_filt_b2, l3_hy_filt_w3, l3_hy_filt_b3, l3_hy_filt_freq, l3_hy_filt_wout, l3_hy_bias_d,
         l3_hy_w_out),
    ]
    return (_trunk(x_prompt, layers, final_norm), _trunk(x_sample, layers, final_norm))
```

```python
import cmath
import functools
import math

import jax
import jax.numpy as jnp
from jax import lax
from jax.experimental import pallas as pl
from jax.experimental.pallas import tpu as pltpu

F32 = jnp.float32
BF16 = jnp.bfloat16

D_MODEL = 1024
D_INNER = 2048
RMS_EPS = 1e-6
LN_EPS = 1e-6

LANES = 128
SUBLANES = 8
N_LANE_BLOCKS = D_INNER // LANES
VMEM_LIMIT = 56 * 1024 * 1024

HY_EMB = 33
HY_FILT = 64
HY_MIN_DECAY = math.log(1e-2) / 1.5
HY_MAX_DECAY = math.log(1e-2) / 0.3
FFT_N2 = 128
FFT_N2_RADICES = (8, 4, 4)
FFT_A_ROWS = 8192
FFT_MID_COLS = 4

HEAD_DIM = 128
N_Q_HEADS = D_INNER // HEAD_DIM
N_KV_HEADS = 4
GROUP = N_Q_HEADS // N_KV_HEADS
WINDOW = 128
BLOCK = 128
ROT_DIM = HEAD_DIM // 4
ROPE_THETA = 500000.0
Q_DIM = N_Q_HEADS * HEAD_DIM
KV_DIM = N_KV_HEADS * HEAD_DIM

ML_HEADS = 4
ML_HEAD_DIM = D_INNER // ML_HEADS
ML_QKV_BLOCK = 4
ML_CHUNK = 128


def _cparams(sem):
    return pltpu.CompilerParams(dimension_semantics=sem, vmem_limit_bytes=VMEM_LIMIT)


def _silu(x):
    return x * (1.0 / (1.0 + jnp.exp(-x)))


def _split_hi_lo(w):
    hi = w.astype(BF16)
    lo = (w - hi.astype(F32)).astype(BF16)
    return jnp.stack([hi, lo])


def _dot3(x, w_hi, w_lo):
    xh = x.astype(BF16)
    xl = (x - xh.astype(F32)).astype(BF16)
    d = lambda a, b: jnp.dot(a, b, preferred_element_type=F32)
    return d(xh, w_hi) + (d(xl, w_hi) + d(xh, w_lo))


def _add(a, b):
    if a is None:
        return b
    if b is None:
        return a
    return a + b


def _sub(a, b):
    if b is None:
        return a
    if a is None:
        return -b
    return a - b


def _scale(a, c):
    if a is None or c == 0.0:
        return None
    if c == 1.0:
        return a
    if c == -1.0:
        return -a
    return a * c


def _cadd(x, y):
    return (_add(x[0], y[0]), _add(x[1], y[1]))


def _csub(x, y):
    return (_sub(x[0], y[0]), _sub(x[1], y[1]))


def _snap(v):
    for t in (0.0, 1.0, -1.0):
        if abs(v - t) < 1e-14:
            return t
    return v


def _cmul_const(x, c):
    cr, ci = _snap(c.real), _snap(c.imag)
    if cr != 0.0 and abs(abs(cr) - abs(ci)) < 1e-14:
        sr, si = math.copysign(1.0, cr), math.copysign(1.0, ci)
        re = _sub(_scale(x[0], sr), _scale(x[1], si))
        im = _add(_scale(x[0], si), _scale(x[1], sr))
        return (_scale(re, abs(cr)), _scale(im, abs(cr)))
    re = _sub(_scale(x[0], cr), _scale(x[1], ci))
    im = _add(_scale(x[0], ci), _scale(x[1], cr))
    return (re, im)


def _cmul(x, y):
    xr, xi = x
    yr, yi = y
    if xi is None:
        return (xr * yr, xr * yi)
    return (xr * yr - xi * yi, xr * yi + xi * yr)


def _dense(x):
    re, im = x
    if re is None:
        re = jnp.zeros_like(im)
    if im is None:
        im = jnp.zeros_like(re)
    return (re, im)


def _small_dft(xs, sign):
    r = len(xs)
    if r == 1:
        return list(xs)
    ev = _small_dft(xs[0::2], sign)
    od = _small_dft(xs[1::2], sign)
    out = [None] * r
    for k in range(r // 2):
        t = _cmul_const(od[k], cmath.exp(sign * 2j * math.pi * k / r))
        out[k] = _cadd(ev[k], t)
        out[k + r // 2] = _csub(ev[k], t)
    return out


def _fft_plan(n, radices, sign):
    assert math.prod(radices) == n
    passes = []
    s, ncur = 1, n
    for r in radices:
        m = ncur // r
        bfs = []
        for p in range(m):
            tws = [cmath.exp(sign * 2j * math.pi * p * k / ncur) for k in range(r)]
            for q in range(s):
                ins = [q + s * (p + m * j) for j in range(r)]
                outs = [q + s * (r * p + k) for k in range(r)]
                bfs.append((ins, outs, tws))
        passes.append(bfs)
        ncur, s = m, s * r
    return passes


def _run_fft(plan, sign, load, store, temps):
    npass = len(plan)
    for pi, bfs in enumerate(plan):
        if pi == 0:
            ld = load
        else:
            tr, ti = temps[(pi - 1) % 2]
            ld = lambda i, tr=tr, ti=ti: (tr[i], ti[i])
        if pi == npass - 1:
            st = store
        else:
            tr2, ti2 = temps[pi % 2]

            def st(i, x, tr2=tr2, ti2=ti2):
                tr2[i] = x[0]
                ti2[i] = x[1]
        for ins, outs, tws in bfs:
            ys = _small_dft([ld(i) for i in ins], sign)
            for o, y, tw in zip(outs, ys, tws):
                st(o, _dense(_cmul_const(y, tw)))


def _radices(n):
    out = []
    while n > 1:
        r = 8 if n % 8 == 0 else (4 if n % 4 == 0 else 2)
        out.append(r)
        n //= r
    return tuple(out)


def _norm_matmul_kernel(x_ref, nw_ref, w_ref, o_ref, xn_ref):
    @pl.when(pl.program_id(1) == 0)
    def _():
        x = x_ref[...]
        inv = lax.rsqrt(jnp.mean(x * x, axis=-1, keepdims=True) + RMS_EPS)
        xn_ref[...] = ((x * inv) * nw_ref[...]).astype(BF16)

    o_ref[...] = jnp.dot(xn_ref[...], w_ref[...], preferred_element_type=F32)


def _norm_matmul(x2d, norm_w, w, tm=1024, tn=1024):
    t, d = x2d.shape
    n = w.shape[1]
    tm = min(tm, t)
    tn = min(tn, n)
    return pl.pallas_call(
        _norm_matmul_kernel,
        grid=(t // tm, n // tn),
        in_specs=[
            pl.BlockSpec((tm, d), lambda i, j: (i, 0)),
            pl.BlockSpec((1, d), lambda i, j: (0, 0)),
            pl.BlockSpec((d, tn), lambda i, j: (0, j)),
        ],
        out_specs=pl.BlockSpec((tm, tn), lambda i, j: (i, j)),
        out_shape=jax.ShapeDtypeStruct((t, n), F32),
        scratch_shapes=[pltpu.VMEM((tm, d), BF16)],
        compiler_params=_cparams(("arbitrary", "arbitrary")),
        name="norm_matmul",
    )(x2d, norm_w.reshape(1, d), w.astype(BF16))


HALO = 16


def _norm_matmul_conv_kernel(x_ref, xp_ref, xn_ref, nw_ref, w_ref, cw_ref, cb_ref, *rest,
                             tm, nconv, seq_tiles, act, dual):
    if dual:
        o_ref, oc_ref, xs_ref = rest
    else:
        o_ref, xs_ref = rest
    i = pl.program_id(0)
    j = pl.program_id(1)

    @pl.when(j == 0)
    def _():
        def nrm(x):
            inv = lax.rsqrt(jnp.mean(x * x, axis=-1, keepdims=True) + RMS_EPS)
            return (x * inv) * nw_ref[...]

        first = (i % seq_tiles) == 0
        last = (i % seq_tiles) == seq_tiles - 1
        xs_ref[0:HALO] = jnp.where(first, 0.0, nrm(xp_ref[...])).astype(BF16)
        xs_ref[HALO:HALO + tm] = nrm(x_ref[...]).astype(BF16)
        xs_ref[HALO + tm:] = jnp.where(last, 0.0, nrm(xn_ref[...])).astype(BF16)

    @pl.when(j < nconv)
    def _():
        rows = tm + 2 * HALO
        tn = w_ref.shape[1]
        nsplit = 2 if tn % (2 * LANES) == 0 else 1
        for c in range(nsplit):
            cs = slice(c * tn // nsplit, (c + 1) * tn // nsplit)
            acc = jnp.dot(xs_ref[...], w_ref[:, cs], preferred_element_type=F32)
            y = (acc * cw_ref[1:2, cs] + pltpu.roll(acc, 1, 0) * cw_ref[0:1, cs]
                 + pltpu.roll(acc, rows - 1, 0) * cw_ref[2:3, cs] + cb_ref[:, cs])
            y = y[HALO:HALO + tm]
            if act:
                y = _silu(y)
            if dual:
                oc_ref[:, cs] = y
                o_ref[:, cs] = acc[HALO:HALO + tm]
            else:
                o_ref[:, cs] = y

    @pl.when(j >= nconv)
    def _():
        o_ref[...] = jnp.dot(xs_ref[HALO:HALO + tm], w_ref[...], preferred_element_type=F32)


def _norm_matmul_conv(x2d, norm_w, w, conv_w, conv_b, seq_len, act, dual, tm=1024, tn=1024):
    t, d = x2d.shape
    n = w.shape[1]
    nc = conv_w.shape[1]
    tm = min(tm, seq_len)
    tn = min(tn, nc)
    nconv = nc // tn
    hb = tm // HALO
    nhb = t // HALO
    cj = lambda j: jnp.minimum(j, nconv - 1)
    in_specs = [
        pl.BlockSpec((tm, d), lambda i, j: (i, 0)),
        pl.BlockSpec((HALO, d), lambda i, j: (jnp.maximum(i * hb - 1, 0), 0)),
        pl.BlockSpec((HALO, d), lambda i, j: (jnp.minimum((i + 1) * hb, nhb - 1), 0)),
        pl.BlockSpec((1, d), lambda i, j: (0, 0)),
        pl.BlockSpec((d, tn), lambda i, j: (0, j)),
        pl.BlockSpec((3, tn), lambda i, j: (0, cj(j))),
        pl.BlockSpec((1, tn), lambda i, j: (0, cj(j))),
    ]
    out_specs = pl.BlockSpec((tm, tn), lambda i, j: (i, j))
    out_shape = jax.ShapeDtypeStruct((t, n), F32)
    if dual:
        out_specs = [out_specs, pl.BlockSpec((tm, tn), lambda i, j: (i, cj(j)))]
        out_shape = [out_shape, jax.ShapeDtypeStruct((t, nc), F32)]
    return pl.pallas_call(
        functools.partial(_norm_matmul_conv_kernel, tm=tm, nconv=nconv, seq_tiles=seq_len // tm,
                          act=act, dual=dual),
        grid=(t // tm, n // tn),
        in_specs=in_specs,
        out_specs=out_specs,
        out_shape=out_shape,
        scratch_shapes=[pltpu.VMEM((tm + 2 * HALO, d), BF16)],
        compiler_params=_cparams(("arbitrary", "arbitrary")),
        name="norm_matmul_conv",
    )(x2d, x2d, x2d, norm_w.reshape(1, d), w.astype(BF16), conv_w, conv_b.reshape(1, nc))


def _out_kernel(*refs, mode, final):
    if mode == "mlstm":
        hf_ref, hb_ref, xc_ref, g_ref, hn_ref, sk_ref, w_ref, r_ref = refs[:8]
        rest = refs[8:]
        h = hf_ref[...] + hb_ref[...]
        parts = []
        for hd in range(ML_HEADS):
            seg = h[:, hd * ML_HEAD_DIM:(hd + 1) * ML_HEAD_DIM]
            mu = jnp.mean(seg, axis=-1, keepdims=True)
            cen = seg - mu
            var = jnp.mean(cen * cen, axis=-1, keepdims=True)
            parts.append(cen * lax.rsqrt(var + LN_EPS))
        a = jnp.concatenate(parts, axis=-1) * hn_ref[...] + sk_ref[...] * xc_ref[...]
    else:
        a_ref, g_ref, w_ref, r_ref = refs[:4]
        rest = refs[4:]
        a = a_ref[...]
    if final:
        fn_ref, o_ref = rest
    else:
        (o_ref,) = rest
    y = jnp.dot((a * _silu(g_ref[...])).astype(BF16), w_ref[...], preferred_element_type=F32)
    x = r_ref[...] + y
    if final:
        inv = lax.rsqrt(jnp.mean(x * x, axis=-1, keepdims=True) + RMS_EPS)
        x = (x * inv) * fn_ref[...]
    o_ref[...] = x


def _out_matmul(acts, gate_arr, gate_blk, w, resid, final_norm=None, extra=(), tm=512):
    t = resid.shape[0]
    tm = min(tm, t)
    e = D_INNER
    mode = "mlstm" if len(acts) == 3 else "plain"
    in_specs, args = [], []
    for a in acts:
        if isinstance(a, tuple):
            arr, idx = a
            in_specs.append(pl.BlockSpec((None, tm, e), lambda i, idx=idx: (idx, i, 0)))
            args.append(arr)
        else:
            in_specs.append(pl.BlockSpec((tm, e), lambda i: (i, 0)))
            args.append(a)
    in_specs.append(pl.BlockSpec((tm, e), lambda i: (i, gate_blk)))
    args.append(gate_arr)
    for v in extra:
        in_specs.append(pl.BlockSpec((1, e), lambda i: (0, 0)))
        args.append(v.reshape(1, e))
    in_specs.append(pl.BlockSpec((e, D_MODEL), lambda i: (0, 0)))
    args.append(w.astype(BF16))
    in_specs.append(pl.BlockSpec((tm, D_MODEL), lambda i: (i, 0)))
    args.append(resid)
    if final_norm is not None:
        in_specs.append(pl.BlockSpec((1, D_MODEL), lambda i: (0, 0)))
        args.append(final_norm.reshape(1, D_MODEL))
    return pl.pallas_call(
        functools.partial(_out_kernel, mode=mode, final=final_norm is not None),
        grid=(t // tm,),
        in_specs=in_specs,
        out_specs=pl.BlockSpec((tm, D_MODEL), lambda i: (i, 0)),
        out_shape=jax.ShapeDtypeStruct((t, D_MODEL), F32),
        compiler_params=_cparams(("arbitrary",)),
        name="out_matmul_" + mode,
    )(*args)


def _filter_kernel(feat_ref, t_ref, w1_ref, b1_ref, w2_ref, b2_ref, w3_ref, b3_ref, fr_ref,
                   wo_ref, dl_ref, o_ref):
    fr = fr_ref[...]
    h = jnp.sin(fr * (_dot3(feat_ref[...], w1_ref[0], w1_ref[1]) + b1_ref[...]))
    h = jnp.sin(fr * (_dot3(h, w2_ref[0], w2_ref[1]) + b2_ref[...]))
    h = jnp.sin(fr * (_dot3(h, w3_ref[0], w3_ref[1]) + b3_ref[...]))
    tcol = t_ref[...]
    decay = jnp.exp(-tcol[:, 0:1] * dl_ref[...]) * tcol[:, 1:2]
    for f in range(2):
        k = _dot3(h, wo_ref[0, f], wo_ref[1, f])
        o_ref[f] = k * decay


def _hyena_two_sided_filters(l, fw1, fb1, fw2, fb2, fw3, fb3, ffreq, fwout, tr=512):
    n = 2 * l
    e = D_INNER
    tr = min(tr, l)
    idx = jnp.arange(n)
    src = jnp.where(idx < l, idx, n - idx)
    valid = (idx != l).astype(F32)
    src = jnp.where(idx == l, 0, src)
    t_all = jnp.linspace(0.0, 1.0, l, dtype=F32)
    bands = (HY_EMB - 1) // 2
    t_src = t_all[src]
    ang = 2.0 * math.pi * src.astype(F32)[:, None] / l
    fb = jnp.linspace(1e-4, bands - 1, bands, dtype=F32)[None, :]
    feat = jnp.concatenate([t_src[:, None], jnp.cos(fb * ang), -jnp.sin(fb * ang)], axis=-1)
    feat = jnp.pad(feat, ((0, 0), (0, LANES - HY_EMB)))
    tcol = jnp.stack([t_src, valid], axis=-1)
    w1p = jnp.pad(fw1, ((0, LANES - HY_EMB), (0, 0)))
    deltas = jnp.abs(jnp.linspace(HY_MIN_DECAY, HY_MAX_DECAY, e, dtype=F32)).reshape(1, e)
    wo = _split_hi_lo(fwout.reshape(HY_FILT, 4, e).transpose(1, 0, 2))
    nhalf = l // tr
    row = lambda v: v.reshape(1, HY_FILT)
    full = lambda shp: pl.BlockSpec(shp, lambda i: tuple(0 for _ in shp))
    return pl.pallas_call(
        _filter_kernel,
        grid=(n // tr,),
        in_specs=[
            pl.BlockSpec((tr, LANES), lambda i: (i, 0)),
            pl.BlockSpec((tr, 2), lambda i: (i, 0)),
            full((2, LANES, HY_FILT)), full((1, HY_FILT)),
            full((2, HY_FILT, HY_FILT)), full((1, HY_FILT)),
            full((2, HY_FILT, HY_FILT)), full((1, HY_FILT)),
            full((1, HY_FILT)),
            pl.BlockSpec((2, 2, HY_FILT, e), lambda i: (0, i // nhalf, 0, 0)),
            full((1, e)),
        ],
        out_specs=pl.BlockSpec((2, tr, e), lambda i: (0, i, 0)),
        out_shape=jax.ShapeDtypeStruct((2, n, e), F32),
        compiler_params=_cparams(("arbitrary",)),
        name="hyena_filter",
    )(feat, tcol, _split_hi_lo(w1p), row(fb1), _split_hi_lo(fw2), row(fb2), _split_hi_lo(fw3),
      row(fb3), row(ffreq), wo, deltas)


FFT_SCRATCH = 6
FFT_INTERLEAVE = 2


def _interleaved_loop(n, one, scratch):
    u = FFT_INTERLEAVE if n % FFT_INTERLEAVE == 0 else 1

    def body(c, carry):
        for k in range(u):
            one(c * u + k, scratch[k * FFT_SCRATCH:(k + 1) * FFT_SCRATCH])
        return carry

    lax.fori_loop(0, n // u, body, 0)


def _fft_scratch(n):
    return [pltpu.VMEM((n, SUBLANES, LANES), F32) for _ in range(FFT_SCRATCH * FFT_INTERLEAVE)]


def _fft_a_kernel(x_ref, zr_ref, zi_ref, *scratch, n1, nj, packed):
    plan = _fft_plan(n1, _radices(n1), -1)

    def one(j, scr):
        temps = [(scr[0], scr[1]), (scr[2], scr[3])]
        fr, fi = scr[4], scr[5]
        rows = pl.ds(pl.multiple_of(j * SUBLANES, SUBLANES), SUBLANES)

        def load(i):
            if packed:
                if i >= n1 // 2:
                    return (None, None)
                return (x_ref[0, 0, i, rows, :], x_ref[1, 0, i, rows, :])
            return (x_ref[0, i, rows, :], x_ref[1, i, rows, :])

        def store(i, x):
            fr[i] = x[0]
            fi[i] = x[1]

        _run_fft(plan, -1, load, store, temps)
        if packed:
            planes = ((0, fr, fi),)
        else:
            ar, ai, br, bi = scr[0], scr[1], scr[2], scr[3]
            for k in range(n1):
                m = (n1 - k) % n1
                ar[k] = fr[k] + fr[m]
                ai[k] = fi[k] - fi[m]
                br[k] = fi[k] + fi[m]
                bi[k] = fr[m] - fr[k]
            planes = ((0, ar, ai), (1, br, bi))
        for f, pr, pi in planes:
            for a in range(n1 // SUBLANES):
                sl = slice(a * SUBLANES, (a + 1) * SUBLANES)
                zr_ref[f, a, 0, rows] = jnp.swapaxes(pr[sl], 0, 1)
                zi_ref[f, a, 0, rows] = jnp.swapaxes(pi[sl], 0, 1)

    _interleaved_loop(nj, one, scratch)


def _fft_a(x, n1, n2, packed, lane_blk0=0):
    nl = N_LANE_BLOCKS
    if packed:
        p = x.shape[1]
        bn2 = min(max(FFT_A_ROWS // n1, SUBLANES), n2)
        in_spec = pl.BlockSpec((2, 1, n1 // 2, bn2, LANES),
                               lambda pp, lb, j: (0, pp, 0, j, lane_blk0 + lb))
        nf = 1
    else:
        p = 1
        bn2 = min(max(FFT_A_ROWS // (4 * n1), SUBLANES), n2)
        in_spec = pl.BlockSpec((2, n1, bn2, LANES), lambda pp, lb, j: (0, 0, j, lb))
        nf = 2
    out_spec = pl.BlockSpec((nf, n1 // SUBLANES, 1, bn2, SUBLANES, LANES),
                            lambda pp, lb, j: (pp, 0, lb, j, 0, 0))
    shp = jax.ShapeDtypeStruct((p * nf, n1 // SUBLANES, nl, n2, SUBLANES, LANES), F32)
    return pl.pallas_call(
        functools.partial(_fft_a_kernel, n1=n1, nj=bn2 // SUBLANES, packed=packed),
        grid=(p, nl, n2 // bn2),
        in_specs=[in_spec],
        out_specs=[out_spec, out_spec],
        out_shape=[shp, shp],
        scratch_shapes=_fft_scratch(n1),
        compiler_params=_cparams(("arbitrary", "arbitrary", "arbitrary")),
        name="fft_a_packed" if packed else "fft_a_pair",
    )(x)


def _fft_mid_kernel(*refs, n2, ncol, nl, conv, scale):
    if conv:
        zr_ref, zi_ref, kr_ref, ki_ref, twr_ref, twi_ref, or_ref, oi_ref = refs[:8]
        scratch = refs[8:]
    else:
        zr_ref, zi_ref, twr_ref, twi_ref, or_ref, oi_ref = refs[:6]
        scratch = refs[6:]
    fwd = _fft_plan(n2, FFT_N2_RADICES, -1)
    inv = _fft_plan(n2, FFT_N2_RADICES, +1)

    def one(c, scr):
        temps = [(scr[0], scr[1]), (scr[2], scr[3])]
        sr, si = scr[4], scr[5]
        i = c // nl
        g = c % nl

        def tw(r):
            return (twr_ref[i, r], twi_ref[i, r])

        def load(r):
            return _cmul((zr_ref[0, i, g, r], zi_ref[0, i, g, r]), tw(r))

        if conv:
            def store_spec(r, x):
                y = _cmul(x, (kr_ref[0, i, g, r], ki_ref[0, i, g, r]))
                sr[r] = y[0]
                si[r] = y[1]

            _run_fft(fwd, -1, load, store_spec, temps)

            def store_out(r, x):
                t = tw(r)
                y = _cmul(x, (t[0], -t[1]))
                or_ref[0, i, g, r] = y[0]
                oi_ref[0, i, g, r] = y[1]

            _run_fft(inv, +1, lambda r: (sr[r], si[r]), store_out, temps)
        else:
            def store_spec(r, x):
                or_ref[0, i, g, r] = x[0] * scale
                oi_ref[0, i, g, r] = x[1] * scale

            _run_fft(fwd, -1, load, store_spec, temps)

    _interleaved_loop(ncol, one, scratch)


def _twiddle_table(n1, n2):
    n = n1 * n2
    k1 = jnp.arange(n1).reshape(n1 // SUBLANES, 1, SUBLANES)
    m = (k1 * jnp.arange(n2).reshape(1, n2, 1)) % n
    ang = m.astype(F32) * (-2.0 * math.pi / n)
    shape = (n1 // SUBLANES, n2, SUBLANES, LANES)
    return (jnp.broadcast_to(jnp.cos(ang)[..., None], shape),
            jnp.broadcast_to(jnp.sin(ang)[..., None], shape))


def _fft_mid(zr, zi, tw, filt=None, scale=1.0):
    p, na, nl, n2 = zr.shape[:4]
    bl = min(FFT_MID_COLS, nl)
    zspec = pl.BlockSpec((1, 1, bl, n2, SUBLANES, LANES), lambda i, g, pp: (pp, i, g, 0, 0, 0))
    twspec = pl.BlockSpec((1, n2, SUBLANES, LANES), lambda i, g, pp: (i, 0, 0, 0))
    conv = filt is not None
    in_specs = [zspec, zspec]
    args = [zr, zi]
    if conv:
        fidx = filt[2]
        kspec = pl.BlockSpec((1, 1, bl, n2, SUBLANES, LANES), lambda i, g, pp: (fidx, i, g, 0, 0, 0))
        in_specs += [kspec, kspec]
        args += [filt[0], filt[1]]
    in_specs += [twspec, twspec]
    args += [tw[0], tw[1]]
    shp = jax.ShapeDtypeStruct(zr.shape, F32)
    return pl.pallas_call(
        functools.partial(_fft_mid_kernel, n2=n2, ncol=bl, nl=bl, conv=conv, scale=scale),
        grid=(na, nl // bl, p),
        in_specs=in_specs,
        out_specs=[zspec, zspec],
        out_shape=[shp, shp],
        scratch_shapes=_fft_scratch(n2),
        compiler_params=_cparams(("arbitrary", "arbitrary", "arbitrary")),
        name="fft_mid_conv" if conv else "fft_mid_spec",
    )(*args)


def _fft_ainv_kernel(zr_ref, zi_ref, u_ref, x_ref, d_ref, o_ref, *scratch, n1, nj):
    plan = _fft_plan(n1, _radices(n1), +1)
    d = d_ref[...]

    def one(j, scr):
        temps = [(scr[0], scr[1]), (scr[2], scr[3])]
        lr, li = scr[4], scr[5]
        rows = pl.ds(pl.multiple_of(j * SUBLANES, SUBLANES), SUBLANES)
        for a in range(n1 // SUBLANES):
            sl = slice(a * SUBLANES, (a + 1) * SUBLANES)
            lr[sl] = jnp.swapaxes(zr_ref[0, a, 0, rows], 0, 1)
            li[sl] = jnp.swapaxes(zi_ref[0, a, 0, rows], 0, 1)

        def store(i, y):
            if i < n1 // 2:
                o_ref[0, 0, i, rows, :] = x_ref[0, 0, i, rows, :] * (y[0] + u_ref[0, 0, i, rows, :] * d)
                o_ref[1, 0, i, rows, :] = x_ref[1, 0, i, rows, :] * (y[1] + u_ref[1, 0, i, rows, :] * d)

        _run_fft(plan, +1, lambda i: (lr[i], li[i]), store, temps)

    _interleaved_loop(nj, one, scratch)


def _fft_ainv(zr, zi, u, u_blk0, x, x_blk0, d, d_row):
    p, na, nl, n2 = zr.shape[:4]
    n1 = na * SUBLANES
    bn2 = min(max(FFT_A_ROWS // n1, SUBLANES), n2)
    zspec = pl.BlockSpec((1, na, 1, bn2, SUBLANES, LANES), lambda pp, lb, j: (pp, 0, lb, j, 0, 0))

    def sspec(blk0):
        return pl.BlockSpec((2, 1, n1 // 2, bn2, LANES), lambda pp, lb, j: (0, pp, 0, j, blk0 + lb))

    return pl.pallas_call(
        functools.partial(_fft_ainv_kernel, n1=n1, nj=bn2 // SUBLANES),
        grid=(p, nl, n2 // bn2),
        in_specs=[zspec, zspec, sspec(u_blk0), sspec(x_blk0),
                  pl.BlockSpec((1, LANES), lambda pp, lb, j: (0, lb))],
        out_specs=sspec(0),
        out_shape=jax.ShapeDtypeStruct((2, p, n1 // 2, n2, D_INNER), F32),
        scratch_shapes=_fft_scratch(n1),
        compiler_params=_cparams(("arbitrary", "arbitrary", "arbitrary")),
        name="fft_a_inv",
    )(zr, zi, u, x, d[d_row].reshape(1, D_INNER))


def _hyena_layer(x, norm_w, w_in, conv_w, conv_b, fw1, fb1, fw2, fb2, fw3, fb3, ffreq, fwout,
                 bias_d, w_out, final_norm=None):
    b, l, _ = x.shape
    e = D_INNER
    n = 2 * l
    n2 = FFT_N2
    n1 = n // n2
    p = b // 2
    nl = N_LANE_BLOCKS
    x2d = x.reshape(b * l, D_MODEL)
    proj = _norm_matmul_conv(x2d, norm_w, w_in, conv_w, conv_b, l, act=False, dual=False)
    pv = proj.reshape(2, p, n1 // 2, n2, 4 * e)

    tw = _twiddle_table(n1, n2)
    kt = _hyena_two_sided_filters(l, fw1, fb1, fw2, fb2, fw3, fb3, ffreq, fwout)
    kr, ki = _fft_a(kt.reshape(2, n1, n2, e), n1, n2, packed=False)
    kfr, kfi = _fft_mid(kr, ki, tw, scale=0.5 / n)

    zr, zi = _fft_a(pv, n1, n2, packed=True)
    zr, zi = _fft_mid(zr, zi, tw, filt=(kfr, kfi, 0))
    z1 = _fft_ainv(zr, zi, pv, 0, pv, nl, bias_d, 0)
    zr, zi = _fft_a(z1, n1, n2, packed=True)
    zr, zi = _fft_mid(zr, zi, tw, filt=(kfr, kfi, 1))
    z2 = _fft_ainv(zr, zi, z1, 0, pv, 2 * nl, bias_d, 1)
    out = _out_matmul([z2.reshape(b * l, e)], proj, 3, w_out, x2d, final_norm)
    return out.reshape(b, l, D_MODEL)


def _swa_kernel(q_ref, kp_ref, kc_ref, kn_ref, vp_ref, vc_ref, vn_ref,
                cq_ref, sq_ref, ckp_ref, skp_ref, ckn_ref, skn_ref, sink_ref, o_ref, *, nb):
    n = pl.program_id(1)
    lane = lax.broadcasted_iota(jnp.int32, (BLOCK, HEAD_DIM), 1)
    half = ROT_DIM // 2

    def rope(x, c, s):
        sw = jnp.where(lane < half, pltpu.roll(x, HEAD_DIM - half, 1), pltpu.roll(x, half, 1))
        return x * c + sw * s

    cq, sq = cq_ref[...], sq_ref[...]
    cks = (ckp_ref[...], cq, ckn_ref[...])
    sks = (skp_ref[...], sq, skn_ref[...])
    qi = lax.broadcasted_iota(jnp.int32, (BLOCK, 3 * BLOCK), 0)
    kj = lax.broadcasted_iota(jnp.int32, (BLOCK, 3 * BLOCK), 1) - BLOCK
    mask = jnp.abs(qi - kj) <= WINDOW
    mask = mask & ((kj >= 0) | (n > 0)) & ((kj < BLOCK) | (n < nb - 1))
    mask4 = jnp.concatenate([mask] * GROUP, axis=0)
    scale = HEAD_DIM ** -0.5
    for h in range(N_KV_HEADS):
        sl = slice(h * HEAD_DIM, (h + 1) * HEAD_DIM)
        kparts = [rope(r[0, :, sl], c, s) for r, c, s in zip((kp_ref, kc_ref, kn_ref), cks, sks)]
        kh = jnp.concatenate(kparts, axis=0).astype(BF16)
        vh = jnp.concatenate([vp_ref[0, :, sl], vc_ref[0, :, sl], vn_ref[0, :, sl]],
                             axis=0).astype(BF16)
        qs = []
        for g in range(GROUP):
            hq = h * GROUP + g
            qs.append(rope(q_ref[0, :, hq * HEAD_DIM:(hq + 1) * HEAD_DIM], cq, sq))
        qh = jnp.concatenate(qs, axis=0).astype(BF16)
        s = lax.dot_general(qh, kh, (((1,), (1,)), ((), ())), preferred_element_type=F32) * scale
        s = jnp.where(mask4, s, -jnp.inf)
        sink = jnp.concatenate(
            [jnp.full((BLOCK, 1), 1.0, F32) * sink_ref[h * GROUP + g] for g in range(GROUP)], axis=0)
        m = jnp.maximum(jnp.max(s, axis=-1, keepdims=True), sink)
        pr = jnp.exp(s - m)
        den = jnp.sum(pr, axis=-1, keepdims=True) + jnp.exp(sink - m)
        o = jnp.dot(pr.astype(BF16), vh, preferred_element_type=F32) * (1.0 / den)
        for g in range(GROUP):
            hq = h * GROUP + g
            o_ref[0, :, hq * HEAD_DIM:(hq + 1) * HEAD_DIM] = o[g * BLOCK:(g + 1) * BLOCK]


def _rope_tables(l):
    half = ROT_DIM // 2
    inv = ROPE_THETA ** (-jnp.arange(half, dtype=F32) / half)
    ang = jnp.arange(l, dtype=F32)[:, None] * inv[None, :]
    cos, sin = jnp.cos(ang), jnp.sin(ang)
    ones = jnp.ones((l, HEAD_DIM - ROT_DIM), F32)
    return (jnp.concatenate([cos, cos, ones], axis=-1),
            jnp.concatenate([-sin, sin, 0.0 * ones], axis=-1))


def _swa_layer(x, norm_w, w_in, sink, w_out, final_norm=None):
    b, l, _ = x.shape
    nb = l // BLOCK
    x2d = x.reshape(b * l, D_MODEL)
    w_perm = jnp.concatenate([w_in[:, :Q_DIM], w_in[:, Q_DIM + 2 * KV_DIM:],
                              w_in[:, Q_DIM:Q_DIM + 2 * KV_DIM]], axis=1)
    proj = _norm_matmul(x2d, norm_w, w_perm)
    p3 = proj.reshape(b, l, 2 * Q_DIM + 2 * KV_DIM)
    ct, st = _rope_tables(l)
    kblk = 2 * Q_DIM // KV_DIM
    prv = lambda bb, i: jnp.maximum(i - 1, 0)
    nxt = lambda bb, i: jnp.minimum(i + 1, nb - 1)
    kv = lambda col, f: pl.BlockSpec((1, BLOCK, KV_DIM), lambda bb, i: (bb, f(bb, i), col))
    cur = lambda bb, i: i
    tab = lambda f: pl.BlockSpec((BLOCK, HEAD_DIM), lambda bb, i: (f(bb, i), 0))
    o = pl.pallas_call(
        functools.partial(_swa_kernel, nb=nb),
        grid=(b, nb),
        in_specs=[
            pl.BlockSpec((1, BLOCK, Q_DIM), lambda bb, i: (bb, i, 0)),
            kv(kblk, prv), kv(kblk, cur), kv(kblk, nxt),
            kv(kblk + 1, prv), kv(kblk + 1, cur), kv(kblk + 1, nxt),
            tab(cur), tab(cur), tab(prv), tab(prv), tab(nxt), tab(nxt),
            pl.BlockSpec(memory_space=pltpu.SMEM),
        ],
        out_specs=pl.BlockSpec((1, BLOCK, Q_DIM), lambda bb, i: (bb, i, 0)),
        out_shape=jax.ShapeDtypeStruct((b, l, Q_DIM), F32),
        compiler_params=_cparams(("arbitrary", "arbitrary")),
        name="swa_attention",
    )(p3, p3, p3, p3, p3, p3, p3, ct, st, ct, st, ct, st, sink)
    out = _out_matmul([o.reshape(b * l, D_INNER)], proj, 1, w_out, x2d, final_norm)
    return out.reshape(b, l, D_MODEL)


def _ml_qkv_kernel(xc_ref, xi_ref, wq_ref, wk_ref, wv_ref, wg_ref, bg_ref, q_ref, k_ref, v_ref,
                   g_ref, gt_ref):
    nt = D_INNER // LANES
    qs, ks, vs = [], [], []
    for j in range(nt):
        sl = slice(j * LANES, (j + 1) * LANES)
        xc = xc_ref[:, sl]
        qs.append(_dot3(xc, wq_ref[0, j], wq_ref[1, j]))
        ks.append(_dot3(xc, wk_ref[0, j], wk_ref[1, j]))
        vs.append(_dot3(xi_ref[:, sl], wv_ref[0, j], wv_ref[1, j]))
    qkv = jnp.concatenate(qs + ks + vs, axis=-1)
    gates = _dot3(qkv, wg_ref[0], wg_ref[1]) + bg_ref[...]
    g_ref[...] = gates
    gt_ref[...] = gates.T
    e = D_INNER
    q_ref[...] = qkv[:, :e].astype(BF16)
    k_ref[...] = (qkv[:, e:2 * e] * (ML_HEAD_DIM ** -0.5)).astype(BF16)
    v_ref[...] = qkv[:, 2 * e:].astype(BF16)


def _blockdiag_tiles(w):
    per = LANES // ML_QKV_BLOCK
    wt = w.reshape(D_INNER // LANES, per, ML_QKV_BLOCK, ML_QKV_BLOCK)
    eye = jnp.eye(per, dtype=w.dtype)
    return jnp.einsum("tnio,nm->tnimo", wt, eye).reshape(D_INNER // LANES, LANES, LANES)


def _ml_qkv(xc2d, proj, wq, wk, wv, w_gate, b_gate, tm=256):
    t = xc2d.shape[0]
    tm = min(tm, t)
    e = D_INNER
    ng = 4 * ML_HEADS
    nt = e // LANES
    wspec = pl.BlockSpec((2, nt, LANES, LANES), lambda i: (0, 0, 0, 0))
    act = jax.ShapeDtypeStruct((t, e), BF16)
    tiles = lambda w: _split_hi_lo(_blockdiag_tiles(w))
    return pl.pallas_call(
        _ml_qkv_kernel,
        grid=(t // tm,),
        in_specs=[
            pl.BlockSpec((tm, e), lambda i: (i, 0)),
            pl.BlockSpec((tm, e), lambda i: (i, 0)),
            wspec, wspec, wspec,
            pl.BlockSpec((2, 3 * e, ng), lambda i: (0, 0, 0)),
            pl.BlockSpec((1, ng), lambda i: (0, 0)),
        ],
        out_specs=[pl.BlockSpec((tm, e), lambda i: (i, 0))] * 3
        + [pl.BlockSpec((tm, ng), lambda i: (i, 0)), pl.BlockSpec((ng, tm), lambda i: (0, i))],
        out_shape=[act, act, act, jax.ShapeDtypeStruct((t, ng), F32),
                   jax.ShapeDtypeStruct((ng, t), F32)],
        compiler_params=_cparams(("arbitrary",)),
        name="mlstm_qkv_gates",
    )(xc2d, proj, tiles(wq), tiles(wk), tiles(wv), _split_hi_lo(w_gate), b_gate.reshape(1, ng))


def _log_sigmoid(x):
    return jnp.minimum(x, 0.0) - jnp.log(1.0 + jnp.exp(-jnp.abs(x)))


def _ml_chunk(d, hd, q_ref, k_ref, v_ref, g, gt, o_ref, c_ref, n_ref, m_ref):
    lc = ML_CHUNK
    ng = 4 * ML_HEADS
    dh = ML_HEAD_DIM
    hs = slice(hd * dh, (hd + 1) * dh)
    col_i = 2 * ML_HEADS * d + hd
    col_f = col_i + ML_HEADS
    lane = lax.broadcasted_iota(jnp.int32, (lc, ng), 1)
    i_col = jnp.sum(jnp.where(lane == col_i, g, 0.0), axis=-1, keepdims=True)
    f_col = _log_sigmoid(jnp.sum(jnp.where(lane == col_f, g, 0.0), axis=-1, keepdims=True))
    sub = lax.broadcasted_iota(jnp.int32, (ng, lc), 0)
    i_row = jnp.sum(jnp.where(sub == col_i, gt, 0.0), axis=0, keepdims=True)
    f_row = _log_sigmoid(jnp.sum(jnp.where(sub == col_f, gt, 0.0), axis=0, keepdims=True))

    t_idx = lax.broadcasted_iota(jnp.int32, (lc, lc), 0)
    s_idx = lax.broadcasted_iota(jnp.int32, (lc, lc), 1)
    if d == 0:
        causal = s_idx <= t_idx
        causal_t = t_idx <= s_idx
    else:
        causal = s_idx >= t_idx
        causal_t = t_idx >= s_idx
    b_col = jnp.sum(jnp.where(causal, f_row, 0.0), axis=-1, keepdims=True)
    b_row = jnp.sum(jnp.where(causal_t, f_col, 0.0), axis=0, keepdims=True)
    m_prev = m_ref[d, hd]
    dmat = jnp.where(causal, b_col - b_row + i_row, -jnp.inf)
    inter = b_col + m_prev
    m_out = jnp.maximum(inter, jnp.max(dmat, axis=-1, keepdims=True))
    w_intra = jnp.exp(dmat - m_out)
    w_state = jnp.exp(inter - m_out)

    q = q_ref[:, hs]
    k = k_ref[:, hs]
    v = v_ref[:, hs]
    cst = c_ref[d, hd]
    s = lax.dot_general(q, k, (((1,), (1,)), ((), ())), preferred_element_type=F32) * w_intra
    num = jnp.dot(s.astype(BF16), v, preferred_element_type=F32)
    num = num + w_state * jnp.dot(q, cst.astype(BF16), preferred_element_type=F32)
    qn = jnp.sum(q.astype(F32) * n_ref[d, hd], axis=-1, keepdims=True)
    den = jnp.sum(s, axis=-1, keepdims=True) + w_state * qn
    o_ref[:, hs] = num / jnp.maximum(jnp.abs(den), jnp.exp(-m_out))

    last = lc - 1 if d == 0 else 0
    bl = jnp.sum(f_row, axis=-1, keepdims=True)
    m_new = m_out[last:last + 1]
    wk = jnp.exp(bl - b_col + i_col - m_new)
    decay = jnp.exp(bl + m_prev - m_new)
    vw = (v.astype(F32) * wk).astype(BF16)
    c_ref[d, hd] = decay * cst + lax.dot_general(
        k, vw, (((0,), (0,)), ((), ())), preferred_element_type=F32)
    n_ref[d, hd] = decay * n_ref[d, hd] + jnp.sum(k.astype(F32) * wk, axis=0, keepdims=True)
    m_ref[d, hd] = m_new


def _ml_scan_kernel(qf_ref, kf_ref, vf_ref, gf_ref, gtf_ref, qb_ref, kb_ref, vb_ref, gb_ref, gtb_ref,
                    of_ref, ob_ref, c_ref, n_ref, m_ref):
    @pl.when(pl.program_id(1) == 0)
    def _():
        c_ref[...] = jnp.zeros_like(c_ref)
        n_ref[...] = jnp.zeros_like(n_ref)
        m_ref[...] = jnp.zeros_like(m_ref)

    dirs = ((qf_ref, kf_ref, vf_ref, gf_ref, gtf_ref, of_ref),
            (qb_ref, kb_ref, vb_ref, gb_ref, gtb_ref, ob_ref))
    for d, (q_ref, k_ref, v_ref, g_ref, gt_ref, o_ref) in enumerate(dirs):
        g = g_ref[...]
        gt = gt_ref[...]
        for hd in range(ML_HEADS):
            _ml_chunk(d, hd, q_ref, k_ref, v_ref, g, gt, o_ref, c_ref, n_ref, m_ref)


def _ml_scan(q, k, v, gates, gates_t, b, l):
    t = b * l
    nc = l // ML_CHUNK
    dh = ML_HEAD_DIM
    e = D_INNER
    ng = 4 * ML_HEADS
    fw = lambda bb, c: bb * nc + c
    bw = lambda bb, c: bb * nc + nc - 1 - c

    def specs(ch):
        qspec = pl.BlockSpec((ML_CHUNK, e), lambda bb, c: (ch(bb, c), 0))
        return [qspec, qspec, qspec,
                pl.BlockSpec((ML_CHUNK, ng), lambda bb, c: (ch(bb, c), 0)),
                pl.BlockSpec((ng, ML_CHUNK), lambda bb, c: (0, ch(bb, c)))]

    out = jax.ShapeDtypeStruct((t, e), F32)
    return pl.pallas_call(
        _ml_scan_kernel,
        grid=(b, nc),
        in_specs=specs(fw) + specs(bw),
        out_specs=[pl.BlockSpec((ML_CHUNK, e), lambda bb, c: (fw(bb, c), 0)),
                   pl.BlockSpec((ML_CHUNK, e), lambda bb, c: (bw(bb, c), 0))],
        out_shape=[out, out],
        scratch_shapes=[pltpu.VMEM((2, ML_HEADS, dh, dh), F32), pltpu.VMEM((2, ML_HEADS, 1, dh), F32),
                        pltpu.VMEM((2, ML_HEADS, 1, 1), F32)],
        compiler_params=_cparams(("arbitrary", "arbitrary")),
        name="mlstm_scan",
    )(q, k, v, gates, gates_t, q, k, v, gates, gates_t)


def _mlstm_layer(x, norm_w, w_in, conv_w, conv_b, wq, wk, wv, w_gate, b_gate, head_norm, skip,
                 w_out, final_norm=None):
    b, l, _ = x.shape
    x2d = x.reshape(b * l, D_MODEL)
    proj, xc = _norm_matmul_conv(x2d, norm_w, w_in, conv_w, conv_b, l, act=True, dual=True)
    q, k, v, gates, gates_t = _ml_qkv(xc, proj, wq, wk, wv, w_gate, b_gate)
    h_fwd, h_bwd = _ml_scan(q, k, v, gates, gates_t, b, l)
    out = _out_matmul([h_fwd, h_bwd, xc], proj, 1, w_out, x2d, final_norm,
                      extra=(head_norm, skip))
    return out.reshape(b, l, D_MODEL)


def _trunk(x, layers, final_norm):
    kinds = (_hyena_layer, _swa_layer, _mlstm_layer, _hyena_layer)
    for i, (fn, params) in enumerate(zip(kinds, layers)):
        x = fn(x, *params, final_norm=final_norm if i == len(kinds) - 1 else None)
    return x


def kernel(x_prompt, x_sample, l0_norm, l0_hy_w_in, l0_hy_conv_w, l0_hy_conv_b, l0_hy_filt_w1, l0_hy_filt_b1, l0_hy_filt_w2, l0_hy_filt_b2, l0_hy_filt_w3, l0_hy_filt_b3, l0_hy_filt_freq, l0_hy_filt_wout, l0_hy_bias_d, l0_hy_w_out, l1_norm, l1_swa_w_in, l1_swa_sink, l1_swa_w_out, l2_norm, l2_ml_w_in, l2_ml_conv_w, l2_ml_conv_b, l2_ml_wq, l2_ml_wk, l2_ml_wv, l2_ml_w_gate, l2_ml_b_gate, l2_ml_head_norm, l2_ml_skip, l2_ml_w_out, l3_norm, l3_hy_w_in, l3_hy_conv_w, l3_hy_conv_b, l3_hy_filt_w1, l3_hy_filt_b1, l3_hy_filt_w2, l3_hy_filt_b2, l3_hy_filt_w3, l3_hy_filt_b3, l3_hy_filt_freq, l3_hy_filt_wout, l3_hy_bias_d, l3_hy_w_out, final_norm):
    layers = [
        (l0_norm, l0_hy_w_in, l0_hy_conv_w, l0_hy_conv_b, l0_hy_filt_w1, l0_hy_filt_b1, l0_hy_filt_w2,
         l0_hy_filt_b2, l0_hy_filt_w3, l0_hy_filt_b3, l0_hy_filt_freq, l0_hy_filt_wout, l0_hy_bias_d,
         l0_hy_w_out),
        (l1_norm, l1_swa_w_in, l1_swa_sink, l1_swa_w_out),
        (l2_norm, l2_ml_w_in, l2_ml_conv_w, l2_ml_conv_b, l2_ml_wq, l2_ml_wk, l2_ml_wv, l2_ml_w_gate,
         l2_ml_b_gate, l2_ml_head_norm, l2_ml_skip, l2_ml_w_out),
        (l3_norm, l3_hy_w_in, l3_hy_conv_w, l3_hy_conv_b, l3_hy_filt_w1, l3_hy_filt_b1, l3_hy_filt_w2,
         l3_hy_filt_b2, l3_hy_filt_w3, l3_hy_filt_b3, l3_hy_filt_freq, l3_hy_filt_wout, l3_hy_bias_d,
         l3_hy_w_out),
    ]
    return (_trunk(x_prompt, layers, final_norm), _trunk(x_sample, layers, final_norm))
```

```python
import cmath
import functools
import math

import jax
import jax.numpy as jnp
from jax import lax
from jax.experimental import pallas as pl
from jax.experimental.pallas import tpu as pltpu

F32 = jnp.float32
BF16 = jnp.bfloat16

D_MODEL = 1024
D_INNER = 2048
RMS_EPS = 1e-6
LN_EPS = 1e-6

LANES = 128
SUBLANES = 8
N_LANE_BLOCKS = D_INNER // LANES
VMEM_LIMIT = 56 * 1024 * 1024

HY_EMB = 33
HY_FILT = 64
HY_MIN_DECAY = math.log(1e-2) / 1.5
HY_MAX_DECAY = math.log(1e-2) / 0.3
FFT_N2 = 128
FFT_N2_RADICES = (8, 4, 4)
FFT_A_ROWS = 8192
FFT_MID_COLS = 4

HEAD_DIM = 128
N_Q_HEADS = D_INNER // HEAD_DIM
N_KV_HEADS = 4
GROUP = N_Q_HEADS // N_KV_HEADS
WINDOW = 128
BLOCK = 128
ROT_DIM = HEAD_DIM // 4
ROPE_THETA = 500000.0
Q_DIM = N_Q_HEADS * HEAD_DIM
KV_DIM = N_KV_HEADS * HEAD_DIM

ML_HEADS = 4
ML_HEAD_DIM = D_INNER // ML_HEADS
ML_QKV_BLOCK = 4
ML_CHUNK = 128


def _cparams(sem):
    return pltpu.CompilerParams(dimension_semantics=sem, vmem_limit_bytes=VMEM_LIMIT)


def _silu(x):
    return x * (1.0 / (1.0 + jnp.exp(-x)))


def _split_hi_lo(w):
    hi = w.astype(BF16)
    lo = (w - hi.astype(F32)).astype(BF16)
    return jnp.stack([hi, lo])


def _dot3(x, w_hi, w_lo):
    xh = x.astype(BF16)
    xl = (x - xh.astype(F32)).astype(BF16)
    d = lambda a, b: jnp.dot(a, b, preferred_element_type=F32)
    return d(xh, w_hi) + (d(xl, w_hi) + d(xh, w_lo))


def _add(a, b):
    if a is None:
        return b
    if b is None:
        return a
    return a + b


def _sub(a, b):
    if b is None:
        return a
    if a is None:
        return -b
    return a - b


def _scale(a, c):
    if a is None or c == 0.0:
        return None
    if c == 1.0:
        return a
    if c == -1.0:
        return -a
    return a * c


def _cadd(x, y):
    return (_add(x[0], y[0]), _add(x[1], y[1]))


def _csub(x, y):
    return (_sub(x[0], y[0]), _sub(x[1], y[1]))


def _snap(v):
    for t in (0.0, 1.0, -1.0):
        if abs(v - t) < 1e-14:
            return t
    return v


def _cmul_const(x, c):
    cr, ci = _snap(c.real), _snap(c.imag)
    if cr != 0.0 and abs(abs(cr) - abs(ci)) < 1e-14:
        sr, si = math.copysign(1.0, cr), math.copysign(1.0, ci)
        re = _sub(_scale(x[0], sr), _scale(x[1], si))
        im = _add(_scale(x[0], si), _scale(x[1], sr))
        return (_scale(re, abs(cr)), _scale(im, abs(cr)))
    re = _sub(_scale(x[0], cr), _scale(x[1], ci))
    im = _add(_scale(x[0], ci), _scale(x[1], cr))
    return (re, im)


def _cmul(x, y):
    xr, xi = x
    yr, yi = y
    if xi is None:
        return (xr * yr, xr * yi)
    return (xr * yr - xi * yi, xr * yi + xi * yr)


def _dense(x):
    re, im = x
    if re is None:
        re = jnp.zeros_like(im)
    if im is None:
        im = jnp.zeros_like(re)
    return (re, im)


def _small_dft(xs, sign):
    r = len(xs)
    if r == 1:
        return list(xs)
    ev = _small_dft(xs[0::2], sign)
    od = _small_dft(xs[1::2], sign)
    out = [None] * r
    for k in range(r // 2):
        t = _cmul_const(od[k], cmath.exp(sign * 2j * math.pi * k / r))
        out[k] = _cadd(ev[k], t)
        out[k + r // 2] = _csub(ev[k], t)
    return out


def _fft_plan(n, radices, sign):
    assert math.prod(radices) == n
    passes = []
    s, ncur = 1, n
    for r in radices:
        m = ncur // r
        bfs = []
        for p in range(m):
            tws = [cmath.exp(sign * 2j * math.pi * p * k / ncur) for k in range(r)]
            for q in range(s):
                ins = [q + s * (p + m * j) for j in range(r)]
                outs = [q + s * (r * p + k) for k in range(r)]
                bfs.append((ins, outs, tws))
        passes.append(bfs)
        ncur, s = m, s * r
    return passes


def _run_fft(plan, sign, load, store, temps):
    npass = len(plan)
    for pi, bfs in enumerate(plan):
        if pi == 0:
            ld = load
        else:
            tr, ti = temps[(pi - 1) % 2]
            ld = lambda i, tr=tr, ti=ti: (tr[i], ti[i])
        if pi == npass - 1:
            st = store
        else:
            tr2, ti2 = temps[pi % 2]

            def st(i, x, tr2=tr2, ti2=ti2):
                tr2[i] = x[0]
                ti2[i] = x[1]
        for ins, outs, tws in bfs:
            ys = _small_dft([ld(i) for i in ins], sign)
            for o, y, tw in zip(outs, ys, tws):
                st(o, _dense(_cmul_const(y, tw)))


def _radices(n):
    out = []
    while n > 1:
        r = 8 if n % 8 == 0 else (4 if n % 4 == 0 else 2)
        out.append(r)
        n //= r
    return tuple(out)


def _norm_matmul_kernel(x_ref, nw_ref, w_ref, o_ref, xn_ref):
    @pl.when(pl.program_id(1) == 0)
    def _():
        x = x_ref[...]
        inv = lax.rsqrt(jnp.mean(x * x, axis=-1, keepdims=True) + RMS_EPS)
        xn_ref[...] = ((x * inv) * nw_ref[...]).astype(BF16)

    o_ref[...] = jnp.dot(xn_ref[...], w_ref[...], preferred_element_type=F32)


def _norm_matmul(x2d, norm_w, w, tm=2048, tn=1024):
    t, d = x2d.shape
    n = w.shape[1]
    tm = min(tm, t)
    tn = min(tn, n)
    return pl.pallas_call(
        _norm_matmul_kernel,
        grid=(t // tm, n // tn),
        in_specs=[
            pl.BlockSpec((tm, d), lambda i, j: (i, 0)),
            pl.BlockSpec((1, d), lambda i, j: (0, 0)),
            pl.BlockSpec((d, tn), lambda i, j: (0, j)),
        ],
        out_specs=pl.BlockSpec((tm, tn), lambda i, j: (i, j)),
        out_shape=jax.ShapeDtypeStruct((t, n), F32),
        scratch_shapes=[pltpu.VMEM((tm, d), BF16)],
        compiler_params=_cparams(("arbitrary", "arbitrary")),
        name="norm_matmul",
    )(x2d, norm_w.reshape(1, d), w.astype(BF16))


HALO = 16


def _norm_matmul_conv_kernel(x_ref, xp_ref, xn_ref, nw_ref, w_ref, cw_ref, cb_ref, *rest,
                             tm, nconv, seq_tiles, act, dual):
    if dual:
        o_ref, oc_ref, xs_ref = rest
    else:
        o_ref, xs_ref = rest
    i = pl.program_id(0)
    j = pl.program_id(1)

    @pl.when(j == 0)
    def _():
        def nrm(x):
            inv = lax.rsqrt(jnp.mean(x * x, axis=-1, keepdims=True) + RMS_EPS)
            return (x * inv) * nw_ref[...]

        first = (i % seq_tiles) == 0
        last = (i % seq_tiles) == seq_tiles - 1
        xs_ref[0:HALO] = jnp.where(first, 0.0, nrm(xp_ref[...])).astype(BF16)
        xs_ref[HALO:HALO + tm] = nrm(x_ref[...]).astype(BF16)
        xs_ref[HALO + tm:] = jnp.where(last, 0.0, nrm(xn_ref[...])).astype(BF16)

    @pl.when(j < nconv)
    def _():
        rows = tm + 2 * HALO
        tn = w_ref.shape[1]
        nsplit = 4 if tn % (8 * LANES) == 0 else 1
        for c in range(nsplit):
            cs = slice(c * tn // nsplit, (c + 1) * tn // nsplit)
            acc = jnp.dot(xs_ref[...], w_ref[:, cs], preferred_element_type=F32)
            y = (acc * cw_ref[1:2, cs] + pltpu.roll(acc, 1, 0) * cw_ref[0:1, cs]
                 + pltpu.roll(acc, rows - 1, 0) * cw_ref[2:3, cs] + cb_ref[:, cs])
            y = y[HALO:HALO + tm]
            if act:
                y = _silu(y)
            if dual:
                oc_ref[:, cs] = y
                o_ref[:, cs] = acc[HALO:HALO + tm]
            else:
                o_ref[:, cs] = y

    @pl.when(j >= nconv)
    def _():
        o_ref[...] = jnp.dot(xs_ref[HALO:HALO + tm], w_ref[...], preferred_element_type=F32)


def _norm_matmul_conv(x2d, norm_w, w, conv_w, conv_b, seq_len, act, dual, tm=2048, tn=1024):
    t, d = x2d.shape
    n = w.shape[1]
    nc = conv_w.shape[1]
    tm = min(tm, seq_len)
    tn = min(tn, nc)
    nconv = nc // tn
    hb = tm // HALO
    nhb = t // HALO
    cj = lambda j: jnp.minimum(j, nconv - 1)
    in_specs = [
        pl.BlockSpec((tm, d), lambda i, j: (i, 0)),
        pl.BlockSpec((HALO, d), lambda i, j: (jnp.maximum(i * hb - 1, 0), 0)),
        pl.BlockSpec((HALO, d), lambda i, j: (jnp.minimum((i + 1) * hb, nhb - 1), 0)),
        pl.BlockSpec((1, d), lambda i, j: (0, 0)),
        pl.BlockSpec((d, tn), lambda i, j: (0, j)),
        pl.BlockSpec((3, tn), lambda i, j: (0, cj(j))),
        pl.BlockSpec((1, tn), lambda i, j: (0, cj(j))),
    ]
    out_specs = pl.BlockSpec((tm, tn), lambda i, j: (i, j))
    out_shape = jax.ShapeDtypeStruct((t, n), F32)
    if dual:
        out_specs = [out_specs, pl.BlockSpec((tm, tn), lambda i, j: (i, cj(j)))]
        out_shape = [out_shape, jax.ShapeDtypeStruct((t, nc), F32)]
    return pl.pallas_call(
        functools.partial(_norm_matmul_conv_kernel, tm=tm, nconv=nconv, seq_tiles=seq_len // tm,
                          act=act, dual=dual),
        grid=(t // tm, n // tn),
        in_specs=in_specs,
        out_specs=out_specs,
        out_shape=out_shape,
        scratch_shapes=[pltpu.VMEM((tm + 2 * HALO, d), BF16)],
        compiler_params=_cparams(("arbitrary", "arbitrary")),
        name="norm_matmul_conv",
    )(x2d, x2d, x2d, norm_w.reshape(1, d), w.astype(BF16), conv_w, conv_b.reshape(1, nc))


def _out_kernel(*refs, mode, final):
    if mode == "mlstm":
        hf_ref, hb_ref, xc_ref, g_ref, hn_ref, sk_ref, w_ref, r_ref = refs[:8]
        rest = refs[8:]
        h = hf_ref[...] + hb_ref[...]
        parts = []
        for hd in range(ML_HEADS):
            seg = h[:, hd * ML_HEAD_DIM:(hd + 1) * ML_HEAD_DIM]
            mu = jnp.mean(seg, axis=-1, keepdims=True)
            cen = seg - mu
            var = jnp.mean(cen * cen, axis=-1, keepdims=True)
            parts.append(cen * lax.rsqrt(var + LN_EPS))
        a = jnp.concatenate(parts, axis=-1) * hn_ref[...] + sk_ref[...] * xc_ref[...]
    else:
        a_ref, g_ref, w_ref, r_ref = refs[:4]
        rest = refs[4:]
        a = a_ref[...]
    if final:
        fn_ref, o_ref = rest
    else:
        (o_ref,) = rest
    y = jnp.dot((a * _silu(g_ref[...])).astype(BF16), w_ref[...], preferred_element_type=F32)
    x = r_ref[...] + y
    if final:
        inv = lax.rsqrt(jnp.mean(x * x, axis=-1, keepdims=True) + RMS_EPS)
        x = (x * inv) * fn_ref[...]
    o_ref[...] = x


def _out_matmul(acts, gate_arr, gate_blk, w, resid, final_norm=None, extra=(), tm=512):
    t = resid.shape[0]
    tm = min(tm, t)
    e = D_INNER
    mode = "mlstm" if len(acts) == 3 else "plain"
    in_specs, args = [], []
    for a in acts:
        if isinstance(a, tuple):
            arr, idx = a
            in_specs.append(pl.BlockSpec((None, tm, e), lambda i, idx=idx: (idx, i, 0)))
            args.append(arr)
        else:
            in_specs.append(pl.BlockSpec((tm, e), lambda i: (i, 0)))
            args.append(a)
    in_specs.append(pl.BlockSpec((tm, e), lambda i: (i, gate_blk)))
    args.append(gate_arr)
    for v in extra:
        in_specs.append(pl.BlockSpec((1, e), lambda i: (0, 0)))
        args.append(v.reshape(1, e))
    in_specs.append(pl.BlockSpec((e, D_MODEL), lambda i: (0, 0)))
    args.append(w.astype(BF16))
    in_specs.append(pl.BlockSpec((tm, D_MODEL), lambda i: (i, 0)))
    args.append(resid)
    if final_norm is not None:
        in_specs.append(pl.BlockSpec((1, D_MODEL), lambda i: (0, 0)))
        args.append(final_norm.reshape(1, D_MODEL))
    return pl.pallas_call(
        functools.partial(_out_kernel, mode=mode, final=final_norm is not None),
        grid=(t // tm,),
        in_specs=in_specs,
        out_specs=pl.BlockSpec((tm, D_MODEL), lambda i: (i, 0)),
        out_shape=jax.ShapeDtypeStruct((t, D_MODEL), F32),
        compiler_params=_cparams(("arbitrary",)),
        name="out_matmul_" + mode,
    )(*args)


def _filter_kernel(feat_ref, t_ref, w1_ref, b1_ref, w2_ref, b2_ref, w3_ref, b3_ref, fr_ref,
                   wo_ref, dl_ref, o_ref):
    fr = fr_ref[...]
    h = jnp.sin(fr * (_dot3(feat_ref[...], w1_ref[0], w1_ref[1]) + b1_ref[...]))
    h = jnp.sin(fr * (_dot3(h, w2_ref[0], w2_ref[1]) + b2_ref[...]))
    h = jnp.sin(fr * (_dot3(h, w3_ref[0], w3_ref[1]) + b3_ref[...]))
    tcol = t_ref[...]
    decay = jnp.exp(-tcol[:, 0:1] * dl_ref[...]) * tcol[:, 1:2]
    for f in range(2):
        k = _dot3(h, wo_ref[0, f], wo_ref[1, f])
        o_ref[f] = k * decay


def _hyena_two_sided_filters(l, fw1, fb1, fw2, fb2, fw3, fb3, ffreq, fwout, tr=512):
    n = 2 * l
    e = D_INNER
    tr = min(tr, l)
    idx = jnp.arange(n)
    src = jnp.where(idx < l, idx, n - idx)
    valid = (idx != l).astype(F32)
    src = jnp.where(idx == l, 0, src)
    t_all = jnp.linspace(0.0, 1.0, l, dtype=F32)
    bands = (HY_EMB - 1) // 2
    t_src = t_all[src]
    ang = 2.0 * math.pi * src.astype(F32)[:, None] / l
    fb = jnp.linspace(1e-4, bands - 1, bands, dtype=F32)[None, :]
    feat = jnp.concatenate([t_src[:, None], jnp.cos(fb * ang), -jnp.sin(fb * ang)], axis=-1)
    feat = jnp.pad(feat, ((0, 0), (0, LANES - HY_EMB)))
    tcol = jnp.stack([t_src, valid], axis=-1)
    w1p = jnp.pad(fw1, ((0, LANES - HY_EMB), (0, 0)))
    deltas = jnp.abs(jnp.linspace(HY_MIN_DECAY, HY_MAX_DECAY, e, dtype=F32)).reshape(1, e)
    wo = _split_hi_lo(fwout.reshape(HY_FILT, 4, e).transpose(1, 0, 2))
    nhalf = l // tr
    row = lambda v: v.reshape(1, HY_FILT)
    full = lambda shp: pl.BlockSpec(shp, lambda i: tuple(0 for _ in shp))
    return pl.pallas_call(
        _filter_kernel,
        grid=(n // tr,),
        in_specs=[
            pl.BlockSpec((tr, LANES), lambda i: (i, 0)),
            pl.BlockSpec((tr, 2), lambda i: (i, 0)),
            full((2, LANES, HY_FILT)), full((1, HY_FILT)),
            full((2, HY_FILT, HY_FILT)), full((1, HY_FILT)),
            full((2, HY_FILT, HY_FILT)), full((1, HY_FILT)),
            full((1, HY_FILT)),
            pl.BlockSpec((2, 2, HY_FILT, e), lambda i: (0, i // nhalf, 0, 0)),
            full((1, e)),
        ],
        out_specs=pl.BlockSpec((2, tr, e), lambda i: (0, i, 0)),
        out_shape=jax.ShapeDtypeStruct((2, n, e), F32),
        compiler_params=_cparams(("arbitrary",)),
        name="hyena_filter",
    )(feat, tcol, _split_hi_lo(w1p), row(fb1), _split_hi_lo(fw2), row(fb2), _split_hi_lo(fw3),
      row(fb3), row(ffreq), wo, deltas)


FFT_SCRATCH = 6
FFT_INTERLEAVE = 2


def _interleaved_loop(n, one, scratch):
    u = FFT_INTERLEAVE if n % FFT_INTERLEAVE == 0 else 1

    def body(c, carry):
        for k in range(u):
            one(c * u + k, scratch[k * FFT_SCRATCH:(k + 1) * FFT_SCRATCH])
        return carry

    lax.fori_loop(0, n // u, body, 0)


def _fft_scratch(n):
    return [pltpu.VMEM((n, SUBLANES, LANES), F32) for _ in range(FFT_SCRATCH * FFT_INTERLEAVE)]


def _fft_a_kernel(x_ref, zr_ref, zi_ref, *scratch, n1, nj, packed):
    plan = _fft_plan(n1, _radices(n1), -1)

    def one(j, scr):
        temps = [(scr[0], scr[1]), (scr[2], scr[3])]
        fr, fi = scr[4], scr[5]
        rows = pl.ds(pl.multiple_of(j * SUBLANES, SUBLANES), SUBLANES)

        def load(i):
            if packed:
                if i >= n1 // 2:
                    return (None, None)
                return (x_ref[0, 0, i, rows, :], x_ref[1, 0, i, rows, :])
            return (x_ref[0, i, rows, :], x_ref[1, i, rows, :])

        def store(i, x):
            fr[i] = x[0]
            fi[i] = x[1]

        _run_fft(plan, -1, load, store, temps)
        if packed:
            planes = ((0, fr, fi),)
        else:
            ar, ai, br, bi = scr[0], scr[1], scr[2], scr[3]
            for k in range(n1):
                m = (n1 - k) % n1
                ar[k] = fr[k] + fr[m]
                ai[k] = fi[k] - fi[m]
                br[k] = fi[k] + fi[m]
                bi[k] = fr[m] - fr[k]
            planes = ((0, ar, ai), (1, br, bi))
        for f, pr, pi in planes:
            for a in range(n1 // SUBLANES):
                sl = slice(a * SUBLANES, (a + 1) * SUBLANES)
                zr_ref[f, a, 0, rows] = jnp.swapaxes(pr[sl], 0, 1)
                zi_ref[f, a, 0, rows] = jnp.swapaxes(pi[sl], 0, 1)

    _interleaved_loop(nj, one, scratch)


def _fft_a(x, n1, n2, packed, lane_blk0=0):
    nl = N_LANE_BLOCKS
    if packed:
        p = x.shape[1]
        bn2 = min(max(FFT_A_ROWS // n1, SUBLANES), n2)
        in_spec = pl.BlockSpec((2, 1, n1 // 2, bn2, LANES),
                               lambda pp, lb, j: (0, pp, 0, j, lane_blk0 + lb))
        nf = 1
    else:
        p = 1
        bn2 = min(max(FFT_A_ROWS // (2 * n1), SUBLANES), n2)
        in_spec = pl.BlockSpec((2, n1, bn2, LANES), lambda pp, lb, j: (0, 0, j, lb))
        nf = 2
    out_spec = pl.BlockSpec((nf, n1 // SUBLANES, 1, bn2, SUBLANES, LANES),
                            lambda pp, lb, j: (pp, 0, lb, j, 0, 0))
    shp = jax.ShapeDtypeStruct((p * nf, n1 // SUBLANES, nl, n2, SUBLANES, LANES), F32)
    return pl.pallas_call(
        functools.partial(_fft_a_kernel, n1=n1, nj=bn2 // SUBLANES, packed=packed),
        grid=(p, nl, n2 // bn2),
        in_specs=[in_spec],
        out_specs=[out_spec, out_spec],
        out_shape=[shp, shp],
        scratch_shapes=_fft_scratch(n1),
        compiler_params=_cparams(("arbitrary", "arbitrary", "arbitrary")),
        name="fft_a_packed" if packed else "fft_a_pair",
    )(x)


def _fft_mid_kernel(*refs, n2, ncol, nl, conv, scale):
    if conv:
        zr_ref, zi_ref, kr_ref, ki_ref, twr_ref, twi_ref, or_ref, oi_ref = refs[:8]
        scratch = refs[8:]
    else:
        zr_ref, zi_ref, twr_ref, twi_ref, or_ref, oi_ref = refs[:6]
        scratch = refs[6:]
    fwd = _fft_plan(n2, FFT_N2_RADICES, -1)
    inv = _fft_plan(n2, FFT_N2_RADICES, +1)

    def one(c, scr):
        temps = [(scr[0], scr[1]), (scr[2], scr[3])]
        sr, si = scr[4], scr[5]
        i = c // nl
        g = c % nl

        def tw(r):
            return (twr_ref[i, r], twi_ref[i, r])

        def load(r):
            return _cmul((zr_ref[0, i, g, r], zi_ref[0, i, g, r]), tw(r))

        if conv:
            def store_spec(r, x):
                y = _cmul(x, (kr_ref[0, i, g, r], ki_ref[0, i, g, r]))
                sr[r] = y[0]
                si[r] = y[1]

            _run_fft(fwd, -1, load, store_spec, temps)

            def store_out(r, x):
                t = tw(r)
                y = _cmul(x, (t[0], -t[1]))
                or_ref[0, i, g, r] = y[0]
                oi_ref[0, i, g, r] = y[1]

            _run_fft(inv, +1, lambda r: (sr[r], si[r]), store_out, temps)
        else:
            def store_spec(r, x):
                or_ref[0, i, g, r] = x[0] * scale
                oi_ref[0, i, g, r] = x[1] * scale

            _run_fft(fwd, -1, load, store_spec, temps)

    _interleaved_loop(ncol, one, scratch)


def _twiddle_table(n1, n2):
    n = n1 * n2
    k1 = jnp.arange(n1).reshape(n1 // SUBLANES, 1, SUBLANES)
    m = (k1 * jnp.arange(n2).reshape(1, n2, 1)) % n
    ang = m.astype(F32) * (-2.0 * math.pi / n)
    shape = (n1 // SUBLANES, n2, SUBLANES, LANES)
    return (jnp.broadcast_to(jnp.cos(ang)[..., None], shape),
            jnp.broadcast_to(jnp.sin(ang)[..., None], shape))


def _fft_mid(zr, zi, tw, filt=None, scale=1.0):
    p, na, nl, n2 = zr.shape[:4]
    bl = min(FFT_MID_COLS, nl)
    zspec = pl.BlockSpec((1, 1, bl, n2, SUBLANES, LANES), lambda i, g, pp: (pp, i, g, 0, 0, 0))
    twspec = pl.BlockSpec((1, n2, SUBLANES, LANES), lambda i, g, pp: (i, 0, 0, 0))
    conv = filt is not None
    in_specs = [zspec, zspec]
    args = [zr, zi]
    if conv:
        fidx = filt[2]
        kspec = pl.BlockSpec((1, 1, bl, n2, SUBLANES, LANES), lambda i, g, pp: (fidx, i, g, 0, 0, 0))
        in_specs += [kspec, kspec]
        args += [filt[0], filt[1]]
    in_specs += [twspec, twspec]
    args += [tw[0], tw[1]]
    shp = jax.ShapeDtypeStruct(zr.shape, F32)
    return pl.pallas_call(
        functools.partial(_fft_mid_kernel, n2=n2, ncol=bl, nl=bl, conv=conv, scale=scale),
        grid=(na, nl // bl, p),
        in_specs=in_specs,
        out_specs=[zspec, zspec],
        out_shape=[shp, shp],
        scratch_shapes=_fft_scratch(n2),
        compiler_params=_cparams(("arbitrary", "arbitrary", "arbitrary")),
        name="fft_mid_conv" if conv else "fft_mid_spec",
    )(*args)


def _fft_ainv_kernel(zr_ref, zi_ref, u_ref, x_ref, d_ref, o_ref, *scratch, n1, nj):
    plan = _fft_plan(n1, _radices(n1), +1)
    d = d_ref[...]

    def one(j, scr):
        temps = [(scr[0], scr[1]), (scr[2], scr[3])]
        lr, li = scr[4], scr[5]
        rows = pl.ds(pl.multiple_of(j * SUBLANES, SUBLANES), SUBLANES)
        for a in range(n1 // SUBLANES):
            sl = slice(a * SUBLANES, (a + 1) * SUBLANES)
            lr[sl] = jnp.swapaxes(zr_ref[0, a, 0, rows], 0, 1)
            li[sl] = jnp.swapaxes(zi_ref[0, a, 0, rows], 0, 1)

        def store(i, y):
            if i < n1 // 2:
                o_ref[0, 0, i, rows, :] = x_ref[0, 0, i, rows, :] * (y[0] + u_ref[0, 0, i, rows, :] * d)
                o_ref[1, 0, i, rows, :] = x_ref[1, 0, i, rows, :] * (y[1] + u_ref[1, 0, i, rows, :] * d)

        _run_fft(plan, +1, lambda i: (lr[i], li[i]), store, temps)

    _interleaved_loop(nj, one, scratch)


def _fft_ainv(zr, zi, u, u_blk0, x, x_blk0, d, d_row):
    p, na, nl, n2 = zr.shape[:4]
    n1 = na * SUBLANES
    bn2 = min(max(FFT_A_ROWS // n1, SUBLANES), n2)
    zspec = pl.BlockSpec((1, na, 1, bn2, SUBLANES, LANES), lambda pp, lb, j: (pp, 0, lb, j, 0, 0))

    def sspec(blk0):
        return pl.BlockSpec((2, 1, n1 // 2, bn2, LANES), lambda pp, lb, j: (0, pp, 0, j, blk0 + lb))

    return pl.pallas_call(
        functools.partial(_fft_ainv_kernel, n1=n1, nj=bn2 // SUBLANES),
        grid=(p, nl, n2 // bn2),
        in_specs=[zspec, zspec, sspec(u_blk0), sspec(x_blk0),
                  pl.BlockSpec((1, LANES), lambda pp, lb, j: (0, lb))],
        out_specs=sspec(0),
        out_shape=jax.ShapeDtypeStruct((2, p, n1 // 2, n2, D_INNER), F32),
        scratch_shapes=_fft_scratch(n1),
        compiler_params=_cparams(("arbitrary", "arbitrary", "arbitrary")),
        name="fft_a_inv",
    )(zr, zi, u, x, d[d_row].reshape(1, D_INNER))


def _hyena_layer(x, norm_w, w_in, conv_w, conv_b, fw1, fb1, fw2, fb2, fw3, fb3, ffreq, fwout,
                 bias_d, w_out, final_norm=None):
    b, l, _ = x.shape
    e = D_INNER
    n = 2 * l
    n2 = FFT_N2
    n1 = n // n2
    p = b // 2
    nl = N_LANE_BLOCKS
    x2d = x.reshape(b * l, D_MODEL)
    proj = _norm_matmul_conv(x2d, norm_w, w_in, conv_w, conv_b, l, act=False, dual=False)
    pv = proj.reshape(2, p, n1 // 2, n2, 4 * e)

    tw = _twiddle_table(n1, n2)
    kt = _hyena_two_sided_filters(l, fw1, fb1, fw2, fb2, fw3, fb3, ffreq, fwout)
    kr, ki = _fft_a(kt.reshape(2, n1, n2, e), n1, n2, packed=False)
    kfr, kfi = _fft_mid(kr, ki, tw, scale=0.5 / n)

    zr, zi = _fft_a(pv, n1, n2, packed=True)
    zr, zi = _fft_mid(zr, zi, tw, filt=(kfr, kfi, 0))
    z1 = _fft_ainv(zr, zi, pv, 0, pv, nl, bias_d, 0)
    zr, zi = _fft_a(z1, n1, n2, packed=True)
    zr, zi = _fft_mid(zr, zi, tw, filt=(kfr, kfi, 1))
    z2 = _fft_ainv(zr, zi, z1, 0, pv, 2 * nl, bias_d, 1)
    out = _out_matmul([z2.reshape(b * l, e)], proj, 3, w_out, x2d, final_norm)
    return out.reshape(b, l, D_MODEL)


def _swa_kernel(q_ref, kp_ref, kc_ref, kn_ref, vp_ref, vc_ref, vn_ref,
                cq_ref, sq_ref, ckp_ref, skp_ref, ckn_ref, skn_ref, sink_ref, o_ref, *, nb):
    n = pl.program_id(1)
    lane = lax.broadcasted_iota(jnp.int32, (BLOCK, HEAD_DIM), 1)
    half = ROT_DIM // 2

    def rope(x, c, s):
        sw = jnp.where(lane < half, pltpu.roll(x, HEAD_DIM - half, 1), pltpu.roll(x, half, 1))
        return x * c + sw * s

    cq, sq = cq_ref[...], sq_ref[...]
    cks = (ckp_ref[...], cq, ckn_ref[...])
    sks = (skp_ref[...], sq, skn_ref[...])
    qi = lax.broadcasted_iota(jnp.int32, (BLOCK, 3 * BLOCK), 0)
    kj = lax.broadcasted_iota(jnp.int32, (BLOCK, 3 * BLOCK), 1) - BLOCK
    mask = jnp.abs(qi - kj) <= WINDOW
    mask = mask & ((kj >= 0) | (n > 0)) & ((kj < BLOCK) | (n < nb - 1))
    mask4 = jnp.concatenate([mask] * GROUP, axis=0)
    scale = HEAD_DIM ** -0.5
    for h in range(N_KV_HEADS):
        sl = slice(h * HEAD_DIM, (h + 1) * HEAD_DIM)
        kparts = [rope(r[0, :, sl], c, s) for r, c, s in zip((kp_ref, kc_ref, kn_ref), cks, sks)]
        kh = jnp.concatenate(kparts, axis=0).astype(BF16)
        vh = jnp.concatenate([vp_ref[0, :, sl], vc_ref[0, :, sl], vn_ref[0, :, sl]],
                             axis=0).astype(BF16)
        qs = []
        for g in range(GROUP):
            hq = h * GROUP + g
            qs.append(rope(q_ref[0, :, hq * HEAD_DIM:(hq + 1) * HEAD_DIM], cq, sq))
        qh = jnp.concatenate(qs, axis=0).astype(BF16)
        s = lax.dot_general(qh, kh, (((1,), (1,)), ((), ())), preferred_element_type=F32) * scale
        s = jnp.where(mask4, s, -jnp.inf)
        sink = jnp.concatenate(
            [jnp.full((BLOCK, 1), 1.0, F32) * sink_ref[h * GROUP + g] for g in range(GROUP)], axis=0)
        m = jnp.maximum(jnp.max(s, axis=-1, keepdims=True), sink)
        pr = jnp.exp(s - m)
        den = jnp.sum(pr, axis=-1, keepdims=True) + jnp.exp(sink - m)
        o = jnp.dot(pr.astype(BF16), vh, preferred_element_type=F32) * (1.0 / den)
        for g in range(GROUP):
            hq = h * GROUP + g
            o_ref[0, :, hq * HEAD_DIM:(hq + 1) * HEAD_DIM] = o[g * BLOCK:(g + 1) * BLOCK]


def _rope_tables(l):
    half = ROT_DIM // 2
    inv = ROPE_THETA ** (-jnp.arange(half, dtype=F32) / half)
    ang = jnp.arange(l, dtype=F32)[:, None] * inv[None, :]
    cos, sin = jnp.cos(ang), jnp.sin(ang)
    ones = jnp.ones((l, HEAD_DIM - ROT_DIM), F32)
    return (jnp.concatenate([cos, cos, ones], axis=-1),
            jnp.concatenate([-sin, sin, 0.0 * ones], axis=-1))


def _swa_layer(x, norm_w, w_in, sink, w_out, final_norm=None):
    b, l, _ = x.shape
    nb = l // BLOCK
    x2d = x.reshape(b * l, D_MODEL)
    w_perm = jnp.concatenate([w_in[:, :Q_DIM], w_in[:, Q_DIM + 2 * KV_DIM:],
                              w_in[:, Q_DIM:Q_DIM + 2 * KV_DIM]], axis=1)
    proj = _norm_matmul(x2d, norm_w, w_perm)
    p3 = proj.reshape(b, l, 2 * Q_DIM + 2 * KV_DIM)
    ct, st = _rope_tables(l)
    kblk = 2 * Q_DIM // KV_DIM
    prv = lambda bb, i: jnp.maximum(i - 1, 0)
    nxt = lambda bb, i: jnp.minimum(i + 1, nb - 1)
    kv = lambda col, f: pl.BlockSpec((1, BLOCK, KV_DIM), lambda bb, i: (bb, f(bb, i), col))
    cur = lambda bb, i: i
    tab = lambda f: pl.BlockSpec((BLOCK, HEAD_DIM), lambda bb, i: (f(bb, i), 0))
    o = pl.pallas_call(
        functools.partial(_swa_kernel, nb=nb),
        grid=(b, nb),
        in_specs=[
            pl.BlockSpec((1, BLOCK, Q_DIM), lambda bb, i: (bb, i, 0)),
            kv(kblk, prv), kv(kblk, cur), kv(kblk, nxt),
            kv(kblk + 1, prv), kv(kblk + 1, cur), kv(kblk + 1, nxt),
            tab(cur), tab(cur), tab(prv), tab(prv), tab(nxt), tab(nxt),
            pl.BlockSpec(memory_space=pltpu.SMEM),
        ],
        out_specs=pl.BlockSpec((1, BLOCK, Q_DIM), lambda bb, i: (bb, i, 0)),
        out_shape=jax.ShapeDtypeStruct((b, l, Q_DIM), F32),
        compiler_params=_cparams(("arbitrary", "arbitrary")),
        name="swa_attention",
    )(p3, p3, p3, p3, p3, p3, p3, ct, st, ct, st, ct, st, sink)
    out = _out_matmul([o.reshape(b * l, D_INNER)], proj, 1, w_out, x2d, final_norm)
    return out.reshape(b, l, D_MODEL)


def _ml_qkv_kernel(xc_ref, xi_ref, wq_ref, wk_ref, wv_ref, wg_ref, bg_ref, q_ref, k_ref, v_ref,
                   g_ref, gt_ref):
    nt = D_INNER // LANES
    qs, ks, vs = [], [], []
    for j in range(nt):
        sl = slice(j * LANES, (j + 1) * LANES)
        xc = xc_ref[:, sl]
        qs.append(_dot3(xc, wq_ref[0, j], wq_ref[1, j]))
        ks.append(_dot3(xc, wk_ref[0, j], wk_ref[1, j]))
        vs.append(_dot3(xi_ref[:, sl], wv_ref[0, j], wv_ref[1, j]))
    qkv = jnp.concatenate(qs + ks + vs, axis=-1)
    gates = _dot3(qkv, wg_ref[0], wg_ref[1]) + bg_ref[...]
    g_ref[...] = gates
    gt_ref[...] = gates.T
    e = D_INNER
    q_ref[...] = qkv[:, :e].astype(BF16)
    k_ref[...] = (qkv[:, e:2 * e] * (ML_HEAD_DIM ** -0.5)).astype(BF16)
    v_ref[...] = qkv[:, 2 * e:].astype(BF16)


def _blockdiag_tiles(w):
    per = LANES // ML_QKV_BLOCK
    wt = w.reshape(D_INNER // LANES, per, ML_QKV_BLOCK, ML_QKV_BLOCK)
    eye = jnp.eye(per, dtype=w.dtype)
    return jnp.einsum("tnio,nm->tnimo", wt, eye).reshape(D_INNER // LANES, LANES, LANES)


def _ml_qkv(xc2d, proj, wq, wk, wv, w_gate, b_gate, tm=256):
    t = xc2d.shape[0]
    tm = min(tm, t)
    e = D_INNER
    ng = 4 * ML_HEADS
    nt = e // LANES
    wspec = pl.BlockSpec((2, nt, LANES, LANES), lambda i: (0, 0, 0, 0))
    act = jax.ShapeDtypeStruct((t, e), BF16)
    tiles = lambda w: _split_hi_lo(_blockdiag_tiles(w))
    return pl.pallas_call(
        _ml_qkv_kernel,
        grid=(t // tm,),
        in_specs=[
            pl.BlockSpec((tm, e), lambda i: (i, 0)),
            pl.BlockSpec((tm, e), lambda i: (i, 0)),
            wspec, wspec, wspec,
            pl.BlockSpec((2, 3 * e, ng), lambda i: (0, 0, 0)),
            pl.BlockSpec((1, ng), lambda i: (0, 0)),
        ],
        out_specs=[pl.BlockSpec((tm, e), lambda i: (i, 0))] * 3
        + [pl.BlockSpec((tm, ng), lambda i: (i, 0)), pl.BlockSpec((ng, tm), lambda i: (0, i))],
        out_shape=[act, act, act, jax.ShapeDtypeStruct((t, ng), F32),
                   jax.ShapeDtypeStruct((ng, t), F32)],
        compiler_params=_cparams(("arbitrary",)),
        name="mlstm_qkv_gates",
    )(xc2d, proj, tiles(wq), tiles(wk), tiles(wv), _split_hi_lo(w_gate), b_gate.reshape(1, ng))


def _log_sigmoid(x):
    return jnp.minimum(x, 0.0) - jnp.log(1.0 + jnp.exp(-jnp.abs(x)))


def _ml_chunk(d, hd, q_ref, k_ref, v_ref, g, gt, o_ref, c_ref, n_ref, m_ref):
    lc = ML_CHUNK
    ng = 4 * ML_HEADS
    dh = ML_HEAD_DIM
    hs = slice(hd * dh, (hd + 1) * dh)
    col_i = 2 * ML_HEADS * d + hd
    col_f = col_i + ML_HEADS
    lane = lax.broadcasted_iota(jnp.int32, (lc, ng), 1)
    i_col = jnp.sum(jnp.where(lane == col_i, g, 0.0), axis=-1, keepdims=True)
    f_col = _log_sigmoid(jnp.sum(jnp.where(lane == col_f, g, 0.0), axis=-1, keepdims=True))
    sub = lax.broadcasted_iota(jnp.int32, (ng, lc), 0)
    i_row = jnp.sum(jnp.where(sub == col_i, gt, 0.0), axis=0, keepdims=True)
    f_row = _log_sigmoid(jnp.sum(jnp.where(sub == col_f, gt, 0.0), axis=0, keepdims=True))

    t_idx = lax.broadcasted_iota(jnp.int32, (lc, lc), 0)
    s_idx = lax.broadcasted_iota(jnp.int32, (lc, lc), 1)
    if d == 0:
        causal = s_idx <= t_idx
        causal_t = t_idx <= s_idx
    else:
        causal = s_idx >= t_idx
        causal_t = t_idx >= s_idx
    b_col = jnp.sum(jnp.where(causal, f_row, 0.0), axis=-1, keepdims=True)
    b_row = jnp.sum(jnp.where(causal_t, f_col, 0.0), axis=0, keepdims=True)
    m_prev = m_ref[d, hd]
    dmat = jnp.where(causal, b_col - b_row + i_row, -jnp.inf)
    inter = b_col + m_prev
    m_out = jnp.maximum(inter, jnp.max(dmat, axis=-1, keepdims=True))
    w_intra = jnp.exp(dmat - m_out)
    w_state = jnp.exp(inter - m_out)

    q = q_ref[:, hs]
    k = k_ref[:, hs]
    v = v_ref[:, hs]
    cst = c_ref[d, hd]
    s = lax.dot_general(q, k, (((1,), (1,)), ((), ())), preferred_element_type=F32) * w_intra
    num = jnp.dot(s.astype(BF16), v, preferred_element_type=F32)
    num = num + w_state * jnp.dot(q, cst.astype(BF16), preferred_element_type=F32)
    qn = jnp.sum(q.astype(F32) * n_ref[d, hd], axis=-1, keepdims=True)
    den = jnp.sum(s, axis=-1, keepdims=True) + w_state * qn
    o_ref[:, hs] = num / jnp.maximum(jnp.abs(den), jnp.exp(-m_out))

    last = lc - 1 if d == 0 else 0
    bl = jnp.sum(f_row, axis=-1, keepdims=True)
    m_new = m_out[last:last + 1]
    wk = jnp.exp(bl - b_col + i_col - m_new)
    decay = jnp.exp(bl + m_prev - m_new)
    vw = (v.astype(F32) * wk).astype(BF16)
    c_ref[d, hd] = decay * cst + lax.dot_general(
        k, vw, (((0,), (0,)), ((), ())), preferred_element_type=F32)
    n_ref[d, hd] = decay * n_ref[d, hd] + jnp.sum(k.astype(F32) * wk, axis=0, keepdims=True)
    m_ref[d, hd] = m_new


def _ml_scan_kernel(qf_ref, kf_ref, vf_ref, gf_ref, gtf_ref, qb_ref, kb_ref, vb_ref, gb_ref, gtb_ref,
                    of_ref, ob_ref, c_ref, n_ref, m_ref):
    @pl.when(pl.program_id(1) == 0)
    def _():
        c_ref[...] = jnp.zeros_like(c_ref)
        n_ref[...] = jnp.zeros_like(n_ref)
        m_ref[...] = jnp.zeros_like(m_ref)

    dirs = ((qf_ref, kf_ref, vf_ref, gf_ref, gtf_ref, of_ref),
            (qb_ref, kb_ref, vb_ref, gb_ref, gtb_ref, ob_ref))
    for d, (q_ref, k_ref, v_ref, g_ref, gt_ref, o_ref) in enumerate(dirs):
        g = g_ref[...]
        gt = gt_ref[...]
        for hd in range(ML_HEADS):
            _ml_chunk(d, hd, q_ref, k_ref, v_ref, g, gt, o_ref, c_ref, n_ref, m_ref)


def _ml_scan(q, k, v, gates, gates_t, b, l):
    t = b * l
    nc = l // ML_CHUNK
    dh = ML_HEAD_DIM
    e = D_INNER
    ng = 4 * ML_HEADS
    fw = lambda bb, c: bb * nc + c
    bw = lambda bb, c: bb * nc + nc - 1 - c

    def specs(ch):
        qspec = pl.BlockSpec((ML_CHUNK, e), lambda bb, c: (ch(bb, c), 0))
        return [qspec, qspec, qspec,
                pl.BlockSpec((ML_CHUNK, ng), lambda bb, c: (ch(bb, c), 0)),
                pl.BlockSpec((ng, ML_CHUNK), lambda bb, c: (0, ch(bb, c)))]

    out = jax.ShapeDtypeStruct((t, e), F32)
    return pl.pallas_call(
        _ml_scan_kernel,
        grid=(b, nc),
        in_specs=specs(fw) + specs(bw),
        out_specs=[pl.BlockSpec((ML_CHUNK, e), lambda bb, c: (fw(bb, c), 0)),
                   pl.BlockSpec((ML_CHUNK, e), lambda bb, c: (bw(bb, c), 0))],
        out_shape=[out, out],
        scratch_shapes=[pltpu.VMEM((2, ML_HEADS, dh, dh), F32), pltpu.VMEM((2, ML_HEADS, 1, dh), F32),
                        pltpu.VMEM((2, ML_HEADS, 1, 1), F32)],
        compiler_params=_cparams(("arbitrary", "arbitrary")),
        name="mlstm_scan",
    )(q, k, v, gates, gates_t, q, k, v, gates, gates_t)


def _mlstm_layer(x, norm_w, w_in, conv_w, conv_b, wq, wk, wv, w_gate, b_gate, head_norm, skip,
                 w_out, final_norm=None):
    b, l, _ = x.shape
    x2d = x.reshape(b * l, D_MODEL)
    proj, xc = _norm_matmul_conv(x2d, norm_w, w_in, conv_w, conv_b, l, act=True, dual=True, tm=1024)
    q, k, v, gates, gates_t = _ml_qkv(xc, proj, wq, wk, wv, w_gate, b_gate)
    h_fwd, h_bwd = _ml_scan(q, k, v, gates, gates_t, b, l)
    out = _out_matmul([h_fwd, h_bwd, xc], proj, 1, w_out, x2d, final_norm,
                      extra=(head_norm, skip))
    return out.reshape(b, l, D_MODEL)


def _trunk(x, layers, final_norm):
    kinds = (_hyena_layer, _swa_layer, _mlstm_layer, _hyena_layer)
    for i, (fn, params) in enumerate(zip(kinds, layers)):
        x = fn(x, *params, final_norm=final_norm if i == len(kinds) - 1 else None)
    return x


def kernel(x_prompt, x_sample, l0_norm, l0_hy_w_in, l0_hy_conv_w, l0_hy_conv_b, l0_hy_filt_w1, l0_hy_filt_b1, l0_hy_filt_w2, l0_hy_filt_b2, l0_hy_filt_w3, l0_hy_filt_b3, l0_hy_filt_freq, l0_hy_filt_wout, l0_hy_bias_d, l0_hy_w_out, l1_norm, l1_swa_w_in, l1_swa_sink, l1_swa_w_out, l2_norm, l2_ml_w_in, l2_ml_conv_w, l2_ml_conv_b, l2_ml_wq, l2_ml_wk, l2_ml_wv, l2_ml_w_gate, l2_ml_b_gate, l2_ml_head_norm, l2_ml_skip, l2_ml_w_out, l3_norm, l3_hy_w_in, l3_hy_conv_w, l3_hy_conv_b, l3_hy_filt_w1, l3_hy_filt_b1, l3_hy_filt_w2, l3_hy_filt_b2, l3_hy_filt_w3, l3_hy_filt_b3, l3_hy_filt_freq, l3_hy_filt_wout, l3_hy_bias_d, l3_hy_w_out, final_norm):
    layers = [
        (l0_norm, l0_hy_w_in, l0_hy_conv_w, l0_hy_conv_b, l0_hy_filt_w1, l0_hy_filt_b1, l0_hy_filt_w2,
         l0_hy_filt_b2, l0_hy_filt_w3, l0_hy_filt_b3, l0_hy_filt_freq, l0_hy_filt_wout, l0_hy_bias_d,
         l0_hy_w_out),
        (l1_norm, l1_swa_w_in, l1_swa_sink, l1_swa_w_out),
        (l2_norm, l2_ml_w_in, l2_ml_conv_w, l2_ml_conv_b, l2_ml_wq, l2_ml_wk, l2_ml_wv, l2_ml_w_gate,
         l2_ml_b_gate, l2_ml_head_norm, l2_ml_skip, l2_ml_w_out),
        (l3_norm, l3_hy_w_in, l3_hy_conv_w, l3_hy_conv_b, l3_hy_filt_w1, l3_hy_filt_b1, l3_hy_filt_w2,
         l3_hy_filt_b2, l3_hy_filt_w3, l3_hy_filt_b3, l3_hy_filt_freq, l3_hy_filt_wout, l3_hy_bias_d,
         l3_hy_w_out),
    ]
    return (_trunk(x_prompt, layers, final_norm), _trunk(x_sample, layers, final_norm))
```

```python
import cmath
import functools
import math

import jax
import jax.numpy as jnp
from jax import lax
from jax.experimental import pallas as pl
from jax.experimental.pallas import tpu as pltpu

F32 = jnp.float32
BF16 = jnp.bfloat16

D_MODEL = 1024
D_INNER = 2048
RMS_EPS = 1e-6
LN_EPS = 1e-6

LANES = 128
SUBLANES = 8
N_LANE_BLOCKS = D_INNER // LANES
VMEM_LIMIT = 56 * 1024 * 1024

HY_EMB = 33
HY_FILT = 64
HY_MIN_DECAY = math.log(1e-2) / 1.5
HY_MAX_DECAY = math.log(1e-2) / 0.3
FFT_N2 = 128
FFT_N2_RADICES = (8, 4, 4)
FFT_A_ROWS = 8192
FFT_MID_COLS = 4

HEAD_DIM = 128
N_Q_HEADS = D_INNER // HEAD_DIM
N_KV_HEADS = 4
GROUP = N_Q_HEADS // N_KV_HEADS
WINDOW = 128
BLOCK = 128
ROT_DIM = HEAD_DIM // 4
ROPE_THETA = 500000.0
Q_DIM = N_Q_HEADS * HEAD_DIM
KV_DIM = N_KV_HEADS * HEAD_DIM

ML_HEADS = 4
ML_HEAD_DIM = D_INNER // ML_HEADS
ML_QKV_BLOCK = 4
ML_CHUNK = 128


def _cparams(sem):
    return pltpu.CompilerParams(dimension_semantics=sem, vmem_limit_bytes=VMEM_LIMIT)


def _silu(x):
    return x * (1.0 / (1.0 + jnp.exp(-x)))


def _split_hi_lo(w):
    hi = w.astype(BF16)
    lo = (w - hi.astype(F32)).astype(BF16)
    return jnp.stack([hi, lo])


def _dot3(x, w_hi, w_lo):
    xh = x.astype(BF16)
    xl = (x - xh.astype(F32)).astype(BF16)
    d = lambda a, b: jnp.dot(a, b, preferred_element_type=F32)
    return d(xh, w_hi) + (d(xl, w_hi) + d(xh, w_lo))


def _add(a, b):
    if a is None:
        return b
    if b is None:
        return a
    return a + b


def _sub(a, b):
    if b is None:
        return a
    if a is None:
        return -b
    return a - b


def _scale(a, c):
    if a is None or c == 0.0:
        return None
    if c == 1.0:
        return a
    if c == -1.0:
        return -a
    return a * c


def _cadd(x, y):
    return (_add(x[0], y[0]), _add(x[1], y[1]))


def _csub(x, y):
    return (_sub(x[0], y[0]), _sub(x[1], y[1]))


def _snap(v):
    for t in (0.0, 1.0, -1.0):
        if abs(v - t) < 1e-14:
            return t
    return v


def _cmul_const(x, c):
    cr, ci = _snap(c.real), _snap(c.imag)
    if cr != 0.0 and abs(abs(cr) - abs(ci)) < 1e-14:
        sr, si = math.copysign(1.0, cr), math.copysign(1.0, ci)
        re = _sub(_scale(x[0], sr), _scale(x[1], si))
        im = _add(_scale(x[0], si), _scale(x[1], sr))
        return (_scale(re, abs(cr)), _scale(im, abs(cr)))
    re = _sub(_scale(x[0], cr), _scale(x[1], ci))
    im = _add(_scale(x[0], ci), _scale(x[1], cr))
    return (re, im)


def _cmul(x, y):
    xr, xi = x
    yr, yi = y
    if xi is None:
        return (xr * yr, xr * yi)
    return (xr * yr - xi * yi, xr * yi + xi * yr)


def _dense(x):
    re, im = x
    if re is None:
        re = jnp.zeros_like(im)
    if im is None:
        im = jnp.zeros_like(re)
    return (re, im)


def _small_dft(xs, sign):
    r = len(xs)
    if r == 1:
        return list(xs)
    ev = _small_dft(xs[0::2], sign)
    od = _small_dft(xs[1::2], sign)
    out = [None] * r
    for k in range(r // 2):
        t = _cmul_const(od[k], cmath.exp(sign * 2j * math.pi * k / r))
        out[k] = _cadd(ev[k], t)
        out[k + r // 2] = _csub(ev[k], t)
    return out


def _fft_plan(n, radices, sign):
    assert math.prod(radices) == n
    passes = []
    s, ncur = 1, n
    for r in radices:
        m = ncur // r
        bfs = []
        for p in range(m):
            tws = [cmath.exp(sign * 2j * math.pi * p * k / ncur) for k in range(r)]
            for q in range(s):
                ins = [q + s * (p + m * j) for j in range(r)]
                outs = [q + s * (r * p + k) for k in range(r)]
                bfs.append((ins, outs, tws))
        passes.append(bfs)
        ncur, s = m, s * r
    return passes


def _run_fft(plan, sign, load, store, temps):
    npass = len(plan)
    for pi, bfs in enumerate(plan):
        if pi == 0:
            ld = load
        else:
            tr, ti = temps[(pi - 1) % 2]
            ld = lambda i, tr=tr, ti=ti: (tr[i], ti[i])
        if pi == npass - 1:
            st = store
        else:
            tr2, ti2 = temps[pi % 2]

            def st(i, x, tr2=tr2, ti2=ti2):
                tr2[i] = x[0]
                ti2[i] = x[1]
        for ins, outs, tws in bfs:
            ys = _small_dft([ld(i) for i in ins], sign)
            for o, y, tw in zip(outs, ys, tws):
                st(o, _dense(_cmul_const(y, tw)))


def _radices(n):
    out = []
    while n > 1:
        r = 8 if n % 8 == 0 else (4 if n % 4 == 0 else 2)
        out.append(r)
        n //= r
    return tuple(out)


def _norm_matmul_kernel(x_ref, nw_ref, w_ref, o_ref, xn_ref):
    @pl.when(pl.program_id(1) == 0)
    def _():
        x = x_ref[...]
        inv = lax.rsqrt(jnp.mean(x * x, axis=-1, keepdims=True) + RMS_EPS)
        xn_ref[...] = ((x * inv) * nw_ref[...]).astype(BF16)

    o_ref[...] = jnp.dot(xn_ref[...], w_ref[...], preferred_element_type=F32)


def _norm_matmul(x2d, norm_w, w, tm=2048, tn=1024):
    t, d = x2d.shape
    n = w.shape[1]
    tm = min(tm, t)
    tn = min(tn, n)
    return pl.pallas_call(
        _norm_matmul_kernel,
        grid=(t // tm, n // tn),
        in_specs=[
            pl.BlockSpec((tm, d), lambda i, j: (i, 0)),
            pl.BlockSpec((1, d), lambda i, j: (0, 0)),
            pl.BlockSpec((d, tn), lambda i, j: (0, j)),
        ],
        out_specs=pl.BlockSpec((tm, tn), lambda i, j: (i, j)),
        out_shape=jax.ShapeDtypeStruct((t, n), F32),
        scratch_shapes=[pltpu.VMEM((tm, d), BF16)],
        compiler_params=_cparams(("arbitrary", "arbitrary")),
        name="norm_matmul",
    )(x2d, norm_w.reshape(1, d), w.astype(BF16))


HALO = 16


def _norm_matmul_conv_kernel(x_ref, xp_ref, xn_ref, nw_ref, w_ref, cw_ref, cb_ref, *rest,
                             tm, nconv, seq_tiles, act, dual):
    if dual:
        o_ref, oc_ref, xs_ref = rest
    else:
        o_ref, xs_ref = rest
    i = pl.program_id(0)
    j = pl.program_id(1)

    @pl.when(j == 0)
    def _():
        def nrm(x):
            inv = lax.rsqrt(jnp.mean(x * x, axis=-1, keepdims=True) + RMS_EPS)
            return (x * inv) * nw_ref[...]

        first = (i % seq_tiles) == 0
        last = (i % seq_tiles) == seq_tiles - 1
        xs_ref[0:HALO] = jnp.where(first, 0.0, nrm(xp_ref[...])).astype(BF16)
        xs_ref[HALO:HALO + tm] = nrm(x_ref[...]).astype(BF16)
        xs_ref[HALO + tm:] = jnp.where(last, 0.0, nrm(xn_ref[...])).astype(BF16)

    @pl.when(j < nconv)
    def _():
        rows = tm + 2 * HALO
        tn = w_ref.shape[1]
        nsplit = 4 if tn % (8 * LANES) == 0 else 1
        for c in range(nsplit):
            cs = slice(c * tn // nsplit, (c + 1) * tn // nsplit)
            acc = jnp.dot(xs_ref[...], w_ref[:, cs], preferred_element_type=F32)
            y = (acc * cw_ref[1:2, cs] + pltpu.roll(acc, 1, 0) * cw_ref[0:1, cs]
                 + pltpu.roll(acc, rows - 1, 0) * cw_ref[2:3, cs] + cb_ref[:, cs])
            y = y[HALO:HALO + tm]
            if act:
                y = _silu(y)
            if dual:
                oc_ref[:, cs] = y
                o_ref[:, cs] = acc[HALO:HALO + tm]
            else:
                o_ref[:, cs] = y

    @pl.when(j >= nconv)
    def _():
        o_ref[...] = jnp.dot(xs_ref[HALO:HALO + tm], w_ref[...], preferred_element_type=F32)


def _norm_matmul_conv(x2d, norm_w, w, conv_w, conv_b, seq_len, act, dual, tm=2048, tn=1024):
    t, d = x2d.shape
    n = w.shape[1]
    nc = conv_w.shape[1]
    tm = min(tm, seq_len)
    tn = min(tn, nc)
    nconv = nc // tn
    hb = tm // HALO
    nhb = t // HALO
    cj = lambda j: jnp.minimum(j, nconv - 1)
    in_specs = [
        pl.BlockSpec((tm, d), lambda i, j: (i, 0)),
        pl.BlockSpec((HALO, d), lambda i, j: (jnp.maximum(i * hb - 1, 0), 0)),
        pl.BlockSpec((HALO, d), lambda i, j: (jnp.minimum((i + 1) * hb, nhb - 1), 0)),
        pl.BlockSpec((1, d), lambda i, j: (0, 0)),
        pl.BlockSpec((d, tn), lambda i, j: (0, j)),
        pl.BlockSpec((3, tn), lambda i, j: (0, cj(j))),
        pl.BlockSpec((1, tn), lambda i, j: (0, cj(j))),
    ]
    out_specs = pl.BlockSpec((tm, tn), lambda i, j: (i, j))
    out_shape = jax.ShapeDtypeStruct((t, n), F32)
    if dual:
        out_specs = [out_specs, pl.BlockSpec((tm, tn), lambda i, j: (i, cj(j)))]
        out_shape = [out_shape, jax.ShapeDtypeStruct((t, nc), F32)]
    return pl.pallas_call(
        functools.partial(_norm_matmul_conv_kernel, tm=tm, nconv=nconv, seq_tiles=seq_len // tm,
                          act=act, dual=dual),
        grid=(t // tm, n // tn),
        in_specs=in_specs,
        out_specs=out_specs,
        out_shape=out_shape,
        scratch_shapes=[pltpu.VMEM((tm + 2 * HALO, d), BF16)],
        compiler_params=_cparams(("arbitrary", "arbitrary")),
        name="norm_matmul_conv",
    )(x2d, x2d, x2d, norm_w.reshape(1, d), w.astype(BF16), conv_w, conv_b.reshape(1, nc))


def _out_kernel(*refs, mode, final):
    if mode == "mlstm":
        hf_ref, hb_ref, xc_ref, g_ref, hn_ref, sk_ref, w_ref, r_ref = refs[:8]
        rest = refs[8:]
        h = hf_ref[...] + hb_ref[...]
        parts = []
        for hd in range(ML_HEADS):
            seg = h[:, hd * ML_HEAD_DIM:(hd + 1) * ML_HEAD_DIM]
            mu = jnp.mean(seg, axis=-1, keepdims=True)
            cen = seg - mu
            var = jnp.mean(cen * cen, axis=-1, keepdims=True)
            parts.append(cen * lax.rsqrt(var + LN_EPS))
        a = jnp.concatenate(parts, axis=-1) * hn_ref[...] + sk_ref[...] * xc_ref[...]
    else:
        a_ref, g_ref, w_ref, r_ref = refs[:4]
        rest = refs[4:]
        a = a_ref[...]
    if final:
        fn_ref, o_ref = rest
    else:
        (o_ref,) = rest
    y = jnp.dot((a * _silu(g_ref[...])).astype(BF16), w_ref[...], preferred_element_type=F32)
    x = r_ref[...] + y
    if final:
        inv = lax.rsqrt(jnp.mean(x * x, axis=-1, keepdims=True) + RMS_EPS)
        x = (x * inv) * fn_ref[...]
    o_ref[...] = x


def _out_matmul(acts, gate_arr, gate_blk, w, resid, final_norm=None, extra=(), tm=512):
    t = resid.shape[0]
    tm = min(tm, t)
    e = D_INNER
    mode = "mlstm" if len(acts) == 3 else "plain"
    in_specs, args = [], []
    for a in acts:
        if isinstance(a, tuple):
            arr, idx = a
            in_specs.append(pl.BlockSpec((None, tm, e), lambda i, idx=idx: (idx, i, 0)))
            args.append(arr)
        else:
            in_specs.append(pl.BlockSpec((tm, e), lambda i: (i, 0)))
            args.append(a)
    in_specs.append(pl.BlockSpec((tm, e), lambda i: (i, gate_blk)))
    args.append(gate_arr)
    for v in extra:
        in_specs.append(pl.BlockSpec((1, e), lambda i: (0, 0)))
        args.append(v.reshape(1, e))
    in_specs.append(pl.BlockSpec((e, D_MODEL), lambda i: (0, 0)))
    args.append(w.astype(BF16))
    in_specs.append(pl.BlockSpec((tm, D_MODEL), lambda i: (i, 0)))
    args.append(resid)
    if final_norm is not None:
        in_specs.append(pl.BlockSpec((1, D_MODEL), lambda i: (0, 0)))
        args.append(final_norm.reshape(1, D_MODEL))
    return pl.pallas_call(
        functools.partial(_out_kernel, mode=mode, final=final_norm is not None),
        grid=(t // tm,),
        in_specs=in_specs,
        out_specs=pl.BlockSpec((tm, D_MODEL), lambda i: (i, 0)),
        out_shape=jax.ShapeDtypeStruct((t, D_MODEL), F32),
        compiler_params=_cparams(("arbitrary",)),
        name="out_matmul_" + mode,
    )(*args)


def _filter_kernel(feat_ref, t_ref, w1_ref, b1_ref, w2_ref, b2_ref, w3_ref, b3_ref, fr_ref,
                   wo_ref, dl_ref, o_ref):
    fr = fr_ref[...]
    h = jnp.sin(fr * (_dot3(feat_ref[...], w1_ref[0], w1_ref[1]) + b1_ref[...]))
    h = jnp.sin(fr * (_dot3(h, w2_ref[0], w2_ref[1]) + b2_ref[...]))
    h = jnp.sin(fr * (_dot3(h, w3_ref[0], w3_ref[1]) + b3_ref[...]))
    tcol = t_ref[...]
    decay = jnp.exp(-tcol[:, 0:1] * dl_ref[...]) * tcol[:, 1:2]
    for f in range(2):
        k = _dot3(h, wo_ref[0, f], wo_ref[1, f])
        o_ref[f] = k * decay


def _hyena_two_sided_filters(l, fw1, fb1, fw2, fb2, fw3, fb3, ffreq, fwout, tr=512):
    n = 2 * l
    e = D_INNER
    tr = min(tr, l)
    idx = jnp.arange(n)
    src = jnp.where(idx < l, idx, n - idx)
    valid = (idx != l).astype(F32)
    src = jnp.where(idx == l, 0, src)
    t_all = jnp.linspace(0.0, 1.0, l, dtype=F32)
    bands = (HY_EMB - 1) // 2
    t_src = t_all[src]
    ang = 2.0 * math.pi * src.astype(F32)[:, None] / l
    fb = jnp.linspace(1e-4, bands - 1, bands, dtype=F32)[None, :]
    feat = jnp.concatenate([t_src[:, None], jnp.cos(fb * ang), -jnp.sin(fb * ang)], axis=-1)
    feat = jnp.pad(feat, ((0, 0), (0, LANES - HY_EMB)))
    tcol = jnp.stack([t_src, valid], axis=-1)
    w1p = jnp.pad(fw1, ((0, LANES - HY_EMB), (0, 0)))
    deltas = jnp.abs(jnp.linspace(HY_MIN_DECAY, HY_MAX_DECAY, e, dtype=F32)).reshape(1, e)
    wo = _split_hi_lo(fwout.reshape(HY_FILT, 4, e).transpose(1, 0, 2))
    nhalf = l // tr
    row = lambda v: v.reshape(1, HY_FILT)
    full = lambda shp: pl.BlockSpec(shp, lambda i: tuple(0 for _ in shp))
    return pl.pallas_call(
        _filter_kernel,
        grid=(n // tr,),
        in_specs=[
            pl.BlockSpec((tr, LANES), lambda i: (i, 0)),
            pl.BlockSpec((tr, 2), lambda i: (i, 0)),
            full((2, LANES, HY_FILT)), full((1, HY_FILT)),
            full((2, HY_FILT, HY_FILT)), full((1, HY_FILT)),
            full((2, HY_FILT, HY_FILT)), full((1, HY_FILT)),
            full((1, HY_FILT)),
            pl.BlockSpec((2, 2, HY_FILT, e), lambda i: (0, i // nhalf, 0, 0)),
            full((1, e)),
        ],
        out_specs=pl.BlockSpec((2, tr, e), lambda i: (0, i, 0)),
        out_shape=jax.ShapeDtypeStruct((2, n, e), F32),
        compiler_params=_cparams(("arbitrary",)),
        name="hyena_filter",
    )(feat, tcol, _split_hi_lo(w1p), row(fb1), _split_hi_lo(fw2), row(fb2), _split_hi_lo(fw3),
      row(fb3), row(ffreq), wo, deltas)


FFT_SCRATCH = 6
FFT_INTERLEAVE = 2


def _interleaved_loop(n, one, scratch):
    u = FFT_INTERLEAVE if n % FFT_INTERLEAVE == 0 else 1

    def body(c, carry):
        for k in range(u):
            one(c * u + k, scratch[k * FFT_SCRATCH:(k + 1) * FFT_SCRATCH])
        return carry

    lax.fori_loop(0, n // u, body, 0)


def _fft_scratch(n):
    return [pltpu.VMEM((n, SUBLANES, LANES), F32) for _ in range(FFT_SCRATCH * FFT_INTERLEAVE)]


def _fft_a_kernel(x_ref, zr_ref, zi_ref, *scratch, n1, nj, packed):
    plan = _fft_plan(n1, _radices(n1), -1)

    def one(j, scr):
        temps = [(scr[0], scr[1]), (scr[2], scr[3])]
        fr, fi = scr[4], scr[5]
        rows = pl.ds(pl.multiple_of(j * SUBLANES, SUBLANES), SUBLANES)

        def load(i):
            if packed:
                if i >= n1 // 2:
                    return (None, None)
                return (x_ref[0, 0, i, rows, :], x_ref[1, 0, i, rows, :])
            return (x_ref[0, i, rows, :], x_ref[1, i, rows, :])

        def store(i, x):
            fr[i] = x[0]
            fi[i] = x[1]

        _run_fft(plan, -1, load, store, temps)
        if packed:
            planes = ((0, fr, fi),)
        else:
            ar, ai, br, bi = scr[0], scr[1], scr[2], scr[3]
            for k in range(n1):
                m = (n1 - k) % n1
                ar[k] = fr[k] + fr[m]
                ai[k] = fi[k] - fi[m]
                br[k] = fi[k] + fi[m]
                bi[k] = fr[m] - fr[k]
            planes = ((0, ar, ai), (1, br, bi))
        for f, pr, pi in planes:
            for a in range(n1 // SUBLANES):
                sl = slice(a * SUBLANES, (a + 1) * SUBLANES)
                zr_ref[f, a, 0, rows] = jnp.swapaxes(pr[sl], 0, 1)
                zi_ref[f, a, 0, rows] = jnp.swapaxes(pi[sl], 0, 1)

    _interleaved_loop(nj, one, scratch)


def _fft_a(x, n1, n2, packed, lane_blk0=0):
    nl = N_LANE_BLOCKS
    if packed:
        p = x.shape[1]
        bn2 = min(max(FFT_A_ROWS // n1, SUBLANES), n2)
        in_spec = pl.BlockSpec((2, 1, n1 // 2, bn2, LANES),
                               lambda pp, lb, j: (0, pp, 0, j, lane_blk0 + lb))
        nf = 1
    else:
        p = 1
        bn2 = min(max(FFT_A_ROWS // (2 * n1), SUBLANES), n2)
        in_spec = pl.BlockSpec((2, n1, bn2, LANES), lambda pp, lb, j: (0, 0, j, lb))
        nf = 2
    out_spec = pl.BlockSpec((nf, n1 // SUBLANES, 1, bn2, SUBLANES, LANES),
                            lambda pp, lb, j: (pp, 0, lb, j, 0, 0))
    shp = jax.ShapeDtypeStruct((p * nf, n1 // SUBLANES, nl, n2, SUBLANES, LANES), F32)
    return pl.pallas_call(
        functools.partial(_fft_a_kernel, n1=n1, nj=bn2 // SUBLANES, packed=packed),
        grid=(p, nl, n2 // bn2),
        in_specs=[in_spec],
        out_specs=[out_spec, out_spec],
        out_shape=[shp, shp],
        scratch_shapes=_fft_scratch(n1),
        compiler_params=_cparams(("arbitrary", "arbitrary", "arbitrary")),
        name="fft_a_packed" if packed else "fft_a_pair",
    )(x)


def _fft_mid_kernel(*refs, n2, ncol, nl, conv):
    if conv:
        zr_ref, zi_ref, kr_ref, ki_ref, twr_ref, twi_ref, or_ref, oi_ref = refs[:8]
        scratch = refs[8:]
    else:
        zr_ref, zi_ref, twr_ref, twi_ref, or_ref, oi_ref = refs[:6]
        scratch = refs[6:]
    fwd = _fft_plan(n2, FFT_N2_RADICES, -1)
    inv = _fft_plan(n2, FFT_N2_RADICES, +1)

    def one(c, scr):
        temps = [(scr[0], scr[1]), (scr[2], scr[3])]
        sr, si = scr[4], scr[5]
        i = c // nl
        g = c % nl

        def tw(r):
            return (twr_ref[i, r], twi_ref[i, r])

        def load(r):
            return _cmul((zr_ref[0, i, g, r], zi_ref[0, i, g, r]), tw(r))

        if conv:
            def store_spec(r, x):
                y = _cmul(x, (kr_ref[0, i, g, r], ki_ref[0, i, g, r]))
                sr[r] = y[0]
                si[r] = y[1]

            _run_fft(fwd, -1, load, store_spec, temps)

            def store_out(r, x):
                tr, ti = tw(r)
                or_ref[0, i, g, r] = x[0] * tr + x[1] * ti
                oi_ref[0, i, g, r] = x[1] * tr - x[0] * ti

            _run_fft(inv, +1, lambda r: (sr[r], si[r]), store_out, temps)
        else:
            def store_spec(r, x):
                or_ref[0, i, g, r] = x[0]
                oi_ref[0, i, g, r] = x[1]

            _run_fft(fwd, -1, load, store_spec, temps)

    _interleaved_loop(ncol, one, scratch)


def _twiddle_table(n1, n2, scale=1.0):
    n = n1 * n2
    k1 = jnp.arange(n1).reshape(n1 // SUBLANES, 1, SUBLANES)
    m = (k1 * jnp.arange(n2).reshape(1, n2, 1)) % n
    ang = m.astype(F32) * (-2.0 * math.pi / n)
    shape = (n1 // SUBLANES, n2, SUBLANES, LANES)
    return (jnp.broadcast_to((scale * jnp.cos(ang))[..., None], shape),
            jnp.broadcast_to((scale * jnp.sin(ang))[..., None], shape))


def _fft_mid(zr, zi, tw, filt=None):
    p, na, nl, n2 = zr.shape[:4]
    bl = min(FFT_MID_COLS, nl)
    zspec = pl.BlockSpec((1, 1, bl, n2, SUBLANES, LANES), lambda i, g, pp: (pp, i, g, 0, 0, 0))
    twspec = pl.BlockSpec((1, n2, SUBLANES, LANES), lambda i, g, pp: (i, 0, 0, 0))
    conv = filt is not None
    in_specs = [zspec, zspec]
    args = [zr, zi]
    if conv:
        fidx = filt[2]
        kspec = pl.BlockSpec((1, 1, bl, n2, SUBLANES, LANES), lambda i, g, pp: (fidx, i, g, 0, 0, 0))
        in_specs += [kspec, kspec]
        args += [filt[0], filt[1]]
    in_specs += [twspec, twspec]
    args += [tw[0], tw[1]]
    shp = jax.ShapeDtypeStruct(zr.shape, F32)
    return pl.pallas_call(
        functools.partial(_fft_mid_kernel, n2=n2, ncol=bl, nl=bl, conv=conv),
        grid=(na, nl // bl, p),
        in_specs=in_specs,
        out_specs=[zspec, zspec],
        out_shape=[shp, shp],
        scratch_shapes=_fft_scratch(n2),
        compiler_params=_cparams(("arbitrary", "arbitrary", "arbitrary")),
        name="fft_mid_conv" if conv else "fft_mid_spec",
    )(*args)


def _fft_ainv_kernel(zr_ref, zi_ref, u_ref, x_ref, d_ref, o_ref, *scratch, n1, nj):
    plan = _fft_plan(n1, _radices(n1), +1)
    d = d_ref[...]

    def one(j, scr):
        temps = [(scr[0], scr[1]), (scr[2], scr[3])]
        lr, li = scr[4], scr[5]
        rows = pl.ds(pl.multiple_of(j * SUBLANES, SUBLANES), SUBLANES)
        for a in range(n1 // SUBLANES):
            sl = slice(a * SUBLANES, (a + 1) * SUBLANES)
            lr[sl] = jnp.swapaxes(zr_ref[0, a, 0, rows], 0, 1)
            li[sl] = jnp.swapaxes(zi_ref[0, a, 0, rows], 0, 1)

        def store(i, y):
            if i < n1 // 2:
                o_ref[0, 0, i, rows, :] = x_ref[0, 0, i, rows, :] * (y[0] + u_ref[0, 0, i, rows, :] * d)
                o_ref[1, 0, i, rows, :] = x_ref[1, 0, i, rows, :] * (y[1] + u_ref[1, 0, i, rows, :] * d)

        _run_fft(plan, +1, lambda i: (lr[i], li[i]), store, temps)

    _interleaved_loop(nj, one, scratch)


def _fft_ainv(zr, zi, u, u_blk0, x, x_blk0, d, d_row):
    p, na, nl, n2 = zr.shape[:4]
    n1 = na * SUBLANES
    bn2 = min(max(FFT_A_ROWS // n1, SUBLANES), n2)
    zspec = pl.BlockSpec((1, na, 1, bn2, SUBLANES, LANES), lambda pp, lb, j: (pp, 0, lb, j, 0, 0))

    def sspec(blk0):
        return pl.BlockSpec((2, 1, n1 // 2, bn2, LANES), lambda pp, lb, j: (0, pp, 0, j, blk0 + lb))

    return pl.pallas_call(
        functools.partial(_fft_ainv_kernel, n1=n1, nj=bn2 // SUBLANES),
        grid=(p, nl, n2 // bn2),
        in_specs=[zspec, zspec, sspec(u_blk0), sspec(x_blk0),
                  pl.BlockSpec((1, LANES), lambda pp, lb, j: (0, lb))],
        out_specs=sspec(0),
        out_shape=jax.ShapeDtypeStruct((2, p, n1 // 2, n2, D_INNER), F32),
        scratch_shapes=_fft_scratch(n1),
        compiler_params=_cparams(("arbitrary", "arbitrary", "arbitrary")),
        name="fft_a_inv",
    )(zr, zi, u, x, d[d_row].reshape(1, D_INNER))


def _hyena_layer(x, norm_w, w_in, conv_w, conv_b, fw1, fb1, fw2, fb2, fw3, fb3, ffreq, fwout,
                 bias_d, w_out, final_norm=None):
    b, l, _ = x.shape
    e = D_INNER
    n = 2 * l
    n2 = FFT_N2
    n1 = n // n2
    p = b // 2
    nl = N_LANE_BLOCKS
    x2d = x.reshape(b * l, D_MODEL)
    proj = _norm_matmul_conv(x2d, norm_w, w_in, conv_w, conv_b, l, act=False, dual=False)
    pv = proj.reshape(2, p, n1 // 2, n2, 4 * e)

    tw = _twiddle_table(n1, n2)
    kt = _hyena_two_sided_filters(l, fw1, fb1, fw2, fb2, fw3, fb3, ffreq, fwout)
    kr, ki = _fft_a(kt.reshape(2, n1, n2, e), n1, n2, packed=False)
    kfr, kfi = _fft_mid(kr, ki, _twiddle_table(n1, n2, scale=0.5 / n))

    zr, zi = _fft_a(pv, n1, n2, packed=True)
    zr, zi = _fft_mid(zr, zi, tw, filt=(kfr, kfi, 0))
    z1 = _fft_ainv(zr, zi, pv, 0, pv, nl, bias_d, 0)
    zr, zi = _fft_a(z1, n1, n2, packed=True)
    zr, zi = _fft_mid(zr, zi, tw, filt=(kfr, kfi, 1))
    z2 = _fft_ainv(zr, zi, z1, 0, pv, 2 * nl, bias_d, 1)
    out = _out_matmul([z2.reshape(b * l, e)], proj, 3, w_out, x2d, final_norm)
    return out.reshape(b, l, D_MODEL)


def _swa_kernel(q_ref, kp_ref, kc_ref, kn_ref, vp_ref, vc_ref, vn_ref,
                cq_ref, sq_ref, ckp_ref, skp_ref, ckn_ref, skn_ref, sink_ref, o_ref, *, nb):
    n = pl.program_id(1)
    lane = lax.broadcasted_iota(jnp.int32, (BLOCK, HEAD_DIM), 1)
    half = ROT_DIM // 2

    def rope(x, c, s):
        sw = jnp.where(lane < half, pltpu.roll(x, HEAD_DIM - half, 1), pltpu.roll(x, half, 1))
        return x * c + sw * s

    cq, sq = cq_ref[...], sq_ref[...]
    cks = (ckp_ref[...], cq, ckn_ref[...])
    sks = (skp_ref[...], sq, skn_ref[...])
    qi = lax.broadcasted_iota(jnp.int32, (BLOCK, 3 * BLOCK), 0)
    kj = lax.broadcasted_iota(jnp.int32, (BLOCK, 3 * BLOCK), 1) - BLOCK
    mask = jnp.abs(qi - kj) <= WINDOW
    mask = mask & ((kj >= 0) | (n > 0)) & ((kj < BLOCK) | (n < nb - 1))
    mask4 = jnp.concatenate([mask] * GROUP, axis=0)
    scale = HEAD_DIM ** -0.5
    for h in range(N_KV_HEADS):
        sl = slice(h * HEAD_DIM, (h + 1) * HEAD_DIM)
        kparts = [rope(r[0, :, sl], c, s) for r, c, s in zip((kp_ref, kc_ref, kn_ref), cks, sks)]
        kh = jnp.concatenate(kparts, axis=0).astype(BF16)
        vh = jnp.concatenate([vp_ref[0, :, sl], vc_ref[0, :, sl], vn_ref[0, :, sl]],
                             axis=0).astype(BF16)
        qs = []
        for g in range(GROUP):
            hq = h * GROUP + g
            qs.append(rope(q_ref[0, :, hq * HEAD_DIM:(hq + 1) * HEAD_DIM], cq, sq))
        qh = (jnp.concatenate(qs, axis=0) * scale).astype(BF16)
        s = lax.dot_general(qh, kh, (((1,), (1,)), ((), ())), preferred_element_type=F32)
        s = jnp.where(mask4, s, -jnp.inf)
        sink = jnp.concatenate(
            [jnp.full((BLOCK, 1), 1.0, F32) * sink_ref[h * GROUP + g] for g in range(GROUP)], axis=0)
        m = jnp.maximum(jnp.max(s, axis=-1, keepdims=True), sink)
        pr = jnp.exp(s - m)
        den = jnp.sum(pr, axis=-1, keepdims=True) + jnp.exp(sink - m)
        o = jnp.dot(pr.astype(BF16), vh, preferred_element_type=F32) * (1.0 / den)
        for g in range(GROUP):
            hq = h * GROUP + g
            o_ref[0, :, hq * HEAD_DIM:(hq + 1) * HEAD_DIM] = o[g * BLOCK:(g + 1) * BLOCK]


def _rope_tables(l):
    half = ROT_DIM // 2
    inv = ROPE_THETA ** (-jnp.arange(half, dtype=F32) / half)
    ang = jnp.arange(l, dtype=F32)[:, None] * inv[None, :]
    cos, sin = jnp.cos(ang), jnp.sin(ang)
    ones = jnp.ones((l, HEAD_DIM - ROT_DIM), F32)
    return (jnp.concatenate([cos, cos, ones], axis=-1),
            jnp.concatenate([-sin, sin, 0.0 * ones], axis=-1))


def _swa_layer(x, norm_w, w_in, sink, w_out, final_norm=None):
    b, l, _ = x.shape
    nb = l // BLOCK
    x2d = x.reshape(b * l, D_MODEL)
    w_perm = jnp.concatenate([w_in[:, :Q_DIM], w_in[:, Q_DIM + 2 * KV_DIM:],
                              w_in[:, Q_DIM:Q_DIM + 2 * KV_DIM]], axis=1)
    proj = _norm_matmul(x2d, norm_w, w_perm)
    p3 = proj.reshape(b, l, 2 * Q_DIM + 2 * KV_DIM)
    ct, st = _rope_tables(l)
    kblk = 2 * Q_DIM // KV_DIM
    prv = lambda bb, i: jnp.maximum(i - 1, 0)
    nxt = lambda bb, i: jnp.minimum(i + 1, nb - 1)
    kv = lambda col, f: pl.BlockSpec((1, BLOCK, KV_DIM), lambda bb, i: (bb, f(bb, i), col))
    cur = lambda bb, i: i
    tab = lambda f: pl.BlockSpec((BLOCK, HEAD_DIM), lambda bb, i: (f(bb, i), 0))
    o = pl.pallas_call(
        functools.partial(_swa_kernel, nb=nb),
        grid=(b, nb),
        in_specs=[
            pl.BlockSpec((1, BLOCK, Q_DIM), lambda bb, i: (bb, i, 0)),
            kv(kblk, prv), kv(kblk, cur), kv(kblk, nxt),
            kv(kblk + 1, prv), kv(kblk + 1, cur), kv(kblk + 1, nxt),
            tab(cur), tab(cur), tab(prv), tab(prv), tab(nxt), tab(nxt),
            pl.BlockSpec(memory_space=pltpu.SMEM),
        ],
        out_specs=pl.BlockSpec((1, BLOCK, Q_DIM), lambda bb, i: (bb, i, 0)),
        out_shape=jax.ShapeDtypeStruct((b, l, Q_DIM), F32),
        compiler_params=_cparams(("arbitrary", "arbitrary")),
        name="swa_attention",
    )(p3, p3, p3, p3, p3, p3, p3, ct, st, ct, st, ct, st, sink)
    out = _out_matmul([o.reshape(b * l, D_INNER)], proj, 1, w_out, x2d, final_norm)
    return out.reshape(b, l, D_MODEL)


def _ml_qkv_kernel(xc_ref, xi_ref, wq_ref, wk_ref, wv_ref, wg_ref, bg_ref, q_ref, k_ref, v_ref,
                   g_ref, gt_ref):
    nt = D_INNER // BD_TILE
    ng = 4 * ML_HEADS
    qs, ks, vs = [], [], []
    for j in range(nt):
        sl = slice(j * BD_TILE, (j + 1) * BD_TILE)
        xc = xc_ref[:, sl]
        qs.append(_dot3(xc, wq_ref[0, j], wq_ref[1, j]))
        ks.append(_dot3(xc, wk_ref[0, j], wk_ref[1, j]))
        vs.append(_dot3(xi_ref[:, sl], wv_ref[0, j], wv_ref[1, j]))
    qkv = jnp.concatenate(qs + ks + vs, axis=-1)
    xh = qkv.astype(BF16)
    xl = (qkv - xh.astype(F32)).astype(BF16)
    both = jnp.dot(xh, wg_ref[...], preferred_element_type=F32)
    low = jnp.dot(xl, wg_ref[:, :ng], preferred_element_type=F32)
    gates = both[:, :ng] + (both[:, ng:] + low) + bg_ref[...]
    g_ref[...] = gates
    gt_ref[...] = gates.T
    e = D_INNER
    q_ref[...] = qkv[:, :e].astype(BF16)
    k_ref[...] = (qkv[:, e:2 * e] * (ML_HEAD_DIM ** -0.5)).astype(BF16)
    v_ref[...] = qkv[:, 2 * e:].astype(BF16)


BD_TILE = 256


def _blockdiag_tiles(w):
    per = BD_TILE // ML_QKV_BLOCK
    wt = w.reshape(D_INNER // BD_TILE, per, ML_QKV_BLOCK, ML_QKV_BLOCK)
    eye = jnp.eye(per, dtype=w.dtype)
    return jnp.einsum("tnio,nm->tnimo", wt, eye).reshape(D_INNER // BD_TILE, BD_TILE, BD_TILE)


def _ml_qkv(xc2d, proj, wq, wk, wv, w_gate, b_gate, tm=256):
    t = xc2d.shape[0]
    tm = min(tm, t)
    e = D_INNER
    ng = 4 * ML_HEADS
    nt = e // BD_TILE
    wspec = pl.BlockSpec((2, nt, BD_TILE, BD_TILE), lambda i: (0, 0, 0, 0))
    act = jax.ShapeDtypeStruct((t, e), BF16)
    tiles = lambda w: _split_hi_lo(_blockdiag_tiles(w))
    wg = _split_hi_lo(w_gate)
    wg = jnp.concatenate([wg[0], wg[1]], axis=1)
    return pl.pallas_call(
        _ml_qkv_kernel,
        grid=(t // tm,),
        in_specs=[
            pl.BlockSpec((tm, e), lambda i: (i, 0)),
            pl.BlockSpec((tm, e), lambda i: (i, 0)),
            wspec, wspec, wspec,
            pl.BlockSpec((3 * e, 2 * ng), lambda i: (0, 0)),
            pl.BlockSpec((1, ng), lambda i: (0, 0)),
        ],
        out_specs=[pl.BlockSpec((tm, e), lambda i: (i, 0))] * 3
        + [pl.BlockSpec((tm, ng), lambda i: (i, 0)), pl.BlockSpec((ng, tm), lambda i: (0, i))],
        out_shape=[act, act, act, jax.ShapeDtypeStruct((t, ng), F32),
                   jax.ShapeDtypeStruct((ng, t), F32)],
        compiler_params=_cparams(("arbitrary",)),
        name="mlstm_qkv_gates",
    )(xc2d, proj, tiles(wq), tiles(wk), tiles(wv), wg, b_gate.reshape(1, ng))


def _log_sigmoid(x):
    return jnp.minimum(x, 0.0) - jnp.log(1.0 + jnp.exp(-jnp.abs(x)))


def _ml_chunk(d, hd, q_ref, k_ref, v_ref, g, gt, o_ref, c_ref, n_ref, m_ref):
    lc = ML_CHUNK
    ng = 4 * ML_HEADS
    dh = ML_HEAD_DIM
    hs = slice(hd * dh, (hd + 1) * dh)
    col_i = 2 * ML_HEADS * d + hd
    col_f = col_i + ML_HEADS
    lane = lax.broadcasted_iota(jnp.int32, (lc, ng), 1)
    i_col = jnp.sum(jnp.where(lane == col_i, g, 0.0), axis=-1, keepdims=True)
    f_col = _log_sigmoid(jnp.sum(jnp.where(lane == col_f, g, 0.0), axis=-1, keepdims=True))
    sub = lax.broadcasted_iota(jnp.int32, (ng, lc), 0)
    i_row = jnp.sum(jnp.where(sub == col_i, gt, 0.0), axis=0, keepdims=True)
    f_row = _log_sigmoid(jnp.sum(jnp.where(sub == col_f, gt, 0.0), axis=0, keepdims=True))

    t_idx = lax.broadcasted_iota(jnp.int32, (lc, lc), 0)
    s_idx = lax.broadcasted_iota(jnp.int32, (lc, lc), 1)
    if d == 0:
        causal = s_idx <= t_idx
        causal_t = t_idx <= s_idx
    else:
        causal = s_idx >= t_idx
        causal_t = t_idx >= s_idx
    b_col = jnp.sum(jnp.where(causal, f_row, 0.0), axis=-1, keepdims=True)
    b_row = jnp.sum(jnp.where(causal_t, f_col, 0.0), axis=0, keepdims=True)
    m_prev = m_ref[d, hd]
    dmat = jnp.where(causal, b_col - b_row + i_row, -jnp.inf)
    inter = b_col + m_prev
    m_out = jnp.maximum(inter, jnp.max(dmat, axis=-1, keepdims=True))
    w_intra = jnp.exp(dmat - m_out)
    w_state = jnp.exp(inter - m_out)

    q = q_ref[:, hs]
    k = k_ref[:, hs]
    v = v_ref[:, hs]
    cst = c_ref[d, hd]
    s = lax.dot_general(q, k, (((1,), (1,)), ((), ())), preferred_element_type=F32) * w_intra
    num = jnp.dot(s.astype(BF16), v, preferred_element_type=F32)
    num = num + w_state * jnp.dot(q, cst.astype(BF16), preferred_element_type=F32)
    qn = jnp.sum(q.astype(F32) * n_ref[d, hd], axis=-1, keepdims=True)
    den = jnp.sum(s, axis=-1, keepdims=True) + w_state * qn
    o_ref[:, hs] = num / jnp.maximum(jnp.abs(den), jnp.exp(-m_out))

    last = lc - 1 if d == 0 else 0
    bl = jnp.sum(f_row, axis=-1, keepdims=True)
    m_new = m_out[last:last + 1]
    wk = jnp.exp(bl - b_col + i_col - m_new)
    decay = jnp.exp(bl + m_prev - m_new)
    vw = (v.astype(F32) * wk).astype(BF16)
    c_ref[d, hd] = decay * cst + lax.dot_general(
        k, vw, (((0,), (0,)), ((), ())), preferred_element_type=F32)
    n_ref[d, hd] = decay * n_ref[d, hd] + jnp.sum(k.astype(F32) * wk, axis=0, keepdims=True)
    m_ref[d, hd] = m_new


def _ml_scan_kernel(qf_ref, kf_ref, vf_ref, gf_ref, gtf_ref, qb_ref, kb_ref, vb_ref, gb_ref, gtb_ref,
                    of_ref, ob_ref, c_ref, n_ref, m_ref):
    @pl.when(pl.program_id(1) == 0)
    def _():
        c_ref[...] = jnp.zeros_like(c_ref)
        n_ref[...] = jnp.zeros_like(n_ref)
        m_ref[...] = jnp.zeros_like(m_ref)

    dirs = ((qf_ref, kf_ref, vf_ref, gf_ref, gtf_ref, of_ref),
            (qb_ref, kb_ref, vb_ref, gb_ref, gtb_ref, ob_ref))
    for d, (q_ref, k_ref, v_ref, g_ref, gt_ref, o_ref) in enumerate(dirs):
        g = g_ref[...]
        gt = gt_ref[...]
        for hd in range(ML_HEADS):
            _ml_chunk(d, hd, q_ref, k_ref, v_ref, g, gt, o_ref, c_ref, n_ref, m_ref)


def _ml_scan(q, k, v, gates, gates_t, b, l):
    t = b * l
    nc = l // ML_CHUNK
    dh = ML_HEAD_DIM
    e = D_INNER
    ng = 4 * ML_HEADS
    fw = lambda bb, c: bb * nc + c
    bw = lambda bb, c: bb * nc + nc - 1 - c

    def specs(ch):
        qspec = pl.BlockSpec((ML_CHUNK, e), lambda bb, c: (ch(bb, c), 0))
        return [qspec, qspec, qspec,
                pl.BlockSpec((ML_CHUNK, ng), lambda bb, c: (ch(bb, c), 0)),
                pl.BlockSpec((ng, ML_CHUNK), lambda bb, c: (0, ch(bb, c)))]

    out = jax.ShapeDtypeStruct((t, e), F32)
    return pl.pallas_call(
        _ml_scan_kernel,
        grid=(b, nc),
        in_specs=specs(fw) + specs(bw),
        out_specs=[pl.BlockSpec((ML_CHUNK, e), lambda bb, c: (fw(bb, c), 0)),
                   pl.BlockSpec((ML_CHUNK, e), lambda bb, c: (bw(bb, c), 0))],
        out_shape=[out, out],
        scratch_shapes=[pltpu.VMEM((2, ML_HEADS, dh, dh), F32), pltpu.VMEM((2, ML_HEADS, 1, dh), F32),
                        pltpu.VMEM((2, ML_HEADS, 1, 1), F32)],
        compiler_params=_cparams(("arbitrary", "arbitrary")),
        name="mlstm_scan",
    )(q, k, v, gates, gates_t, q, k, v, gates, gates_t)


def _mlstm_layer(x, norm_w, w_in, conv_w, conv_b, wq, wk, wv, w_gate, b_gate, head_norm, skip,
                 w_out, final_norm=None):
    b, l, _ = x.shape
    x2d = x.reshape(b * l, D_MODEL)
    proj, xc = _norm_matmul_conv(x2d, norm_w, w_in, conv_w, conv_b, l, act=True, dual=True, tm=1024)
    q, k, v, gates, gates_t = _ml_qkv(xc, proj, wq, wk, wv, w_gate, b_gate)
    h_fwd, h_bwd = _ml_scan(q, k, v, gates, gates_t, b, l)
    out = _out_matmul([h_fwd, h_bwd, xc], proj, 1, w_out, x2d, final_norm,
                      extra=(head_norm, skip))
    return out.reshape(b, l, D_MODEL)


def _trunk(x, layers, final_norm):
    kinds = (_hyena_layer, _swa_layer, _mlstm_layer, _hyena_layer)
    for i, (fn, params) in enumerate(zip(kinds, layers)):
        x = fn(x, *params, final_norm=final_norm if i == len(kinds) - 1 else None)
    return x


def kernel(x_prompt, x_sample, l0_norm, l0_hy_w_in, l0_hy_conv_w, l0_hy_conv_b, l0_hy_filt_w1, l0_hy_filt_b1, l0_hy_filt_w2, l0_hy_filt_b2, l0_hy_filt_w3, l0_hy_filt_b3, l0_hy_filt_freq, l0_hy_filt_wout, l0_hy_bias_d, l0_hy_w_out, l1_norm, l1_swa_w_in, l1_swa_sink, l1_swa_w_out, l2_norm, l2_ml_w_in, l2_ml_conv_w, l2_ml_conv_b, l2_ml_wq, l2_ml_wk, l2_ml_wv, l2_ml_w_gate, l2_ml_b_gate, l2_ml_head_norm, l2_ml_skip, l2_ml_w_out, l3_norm, l3_hy_w_in, l3_hy_conv_w, l3_hy_conv_b, l3_hy_filt_w1, l3_hy_filt_b1, l3_hy_filt_w2, l3_hy_filt_b2, l3_hy_filt_w3, l3_hy_filt_b3, l3_hy_filt_freq, l3_hy_filt_wout, l3_hy_bias_d, l3_hy_w_out, final_norm):
    layers = [
        (l0_norm, l0_hy_w_in, l0_hy_conv_w, l0_hy_conv_b, l0_hy_filt_w1, l0_hy_filt_b1, l0_hy_filt_w2,
         l0_hy_filt_b2, l0_hy_filt_w3, l0_hy_filt_b3, l0_hy_filt_freq, l0_hy_filt_wout, l0_hy_bias_d,
         l0_hy_w_out),
        (l1_norm, l1_swa_w_in, l1_swa_sink, l1_swa_w_out),
        (l2_norm, l2_ml_w_in, l2_ml_conv_w, l2_ml_conv_b, l2_ml_wq, l2_ml_wk, l2_ml_wv, l2_ml_w_gate,
         l2_ml_b_gate, l2_ml_head_norm, l2_ml_skip, l2_ml_w_out),
        (l3_norm, l3_hy_w_in, l3_hy_conv_w, l3_hy_conv_b, l3_hy_filt_w1, l3_hy_filt_b1, l3_hy_filt_w2,
         l3_hy_filt_b2, l3_hy_filt_w3, l3_hy_filt_b3, l3_hy_filt_freq, l3_hy_filt_wout, l3_hy_bias_d,
         l3_hy_w_out),
    ]
    return (_trunk(x_prompt, layers, final_norm), _trunk(x_sample, layers, final_norm))
```

```python
import cmath
import functools
import math

import jax
import jax.numpy as jnp
from jax import lax
from jax.experimental import pallas as pl
from jax.experimental.pallas import tpu as pltpu

F32 = jnp.float32
BF16 = jnp.bfloat16

D_MODEL = 1024
D_INNER = 2048
RMS_EPS = 1e-6
LN_EPS = 1e-6

LANES = 128
SUBLANES = 8
N_LANE_BLOCKS = D_INNER // LANES
VMEM_LIMIT = 56 * 1024 * 1024

HY_EMB = 33
HY_FILT = 64
HY_MIN_DECAY = math.log(1e-2) / 1.5
HY_MAX_DECAY = math.log(1e-2) / 0.3
FFT_N2 = 128
FFT_N2_RADICES = (8, 4, 4)
FFT_A_ROWS = 8192
FFT_MID_COLS = 4

HEAD_DIM = 128
N_Q_HEADS = D_INNER // HEAD_DIM
N_KV_HEADS = 4
GROUP = N_Q_HEADS // N_KV_HEADS
WINDOW = 128
BLOCK = 128
ROT_DIM = HEAD_DIM // 4
ROPE_THETA = 500000.0
Q_DIM = N_Q_HEADS * HEAD_DIM
KV_DIM = N_KV_HEADS * HEAD_DIM

ML_HEADS = 4
ML_HEAD_DIM = D_INNER // ML_HEADS
ML_QKV_BLOCK = 4
ML_CHUNK = 128


def _cparams(sem):
    return pltpu.CompilerParams(dimension_semantics=sem, vmem_limit_bytes=VMEM_LIMIT)


def _silu(x):
    return x * (1.0 / (1.0 + jnp.exp(-x)))


def _split_hi_lo(w):
    hi = w.astype(BF16)
    lo = (w - hi.astype(F32)).astype(BF16)
    return jnp.stack([hi, lo])


def _dot3(x, w_hi, w_lo):
    xh = x.astype(BF16)
    xl = (x - xh.astype(F32)).astype(BF16)
    d = lambda a, b: jnp.dot(a, b, preferred_element_type=F32)
    return d(xh, w_hi) + (d(xl, w_hi) + d(xh, w_lo))


def _add(a, b):
    if a is None:
        return b
    if b is None:
        return a
    return a + b


def _sub(a, b):
    if b is None:
        return a
    if a is None:
        return -b
    return a - b


def _scale(a, c):
    if a is None or c == 0.0:
        return None
    if c == 1.0:
        return a
    if c == -1.0:
        return -a
    return a * c


def _cadd(x, y):
    return (_add(x[0], y[0]), _add(x[1], y[1]))


def _csub(x, y):
    return (_sub(x[0], y[0]), _sub(x[1], y[1]))


def _snap(v):
    for t in (0.0, 1.0, -1.0):
        if abs(v - t) < 1e-14:
            return t
    return v


def _cmul_const(x, c):
    cr, ci = _snap(c.real), _snap(c.imag)
    if cr != 0.0 and abs(abs(cr) - abs(ci)) < 1e-14:
        sr, si = math.copysign(1.0, cr), math.copysign(1.0, ci)
        re = _sub(_scale(x[0], sr), _scale(x[1], si))
        im = _add(_scale(x[0], si), _scale(x[1], sr))
        return (_scale(re, abs(cr)), _scale(im, abs(cr)))
    re = _sub(_scale(x[0], cr), _scale(x[1], ci))
    im = _add(_scale(x[0], ci), _scale(x[1], cr))
    return (re, im)


def _cmul(x, y):
    xr, xi = x
    yr, yi = y
    if xi is None:
        return (xr * yr, xr * yi)
    return (xr * yr - xi * yi, xr * yi + xi * yr)


def _dense(x):
    re, im = x
    if re is None:
        re = jnp.zeros_like(im)
    if im is None:
        im = jnp.zeros_like(re)
    return (re, im)


def _small_dft(xs, sign):
    r = len(xs)
    if r == 1:
        return list(xs)
    ev = _small_dft(xs[0::2], sign)
    od = _small_dft(xs[1::2], sign)
    out = [None] * r
    for k in range(r // 2):
        t = _cmul_const(od[k], cmath.exp(sign * 2j * math.pi * k / r))
        out[k] = _cadd(ev[k], t)
        out[k + r // 2] = _csub(ev[k], t)
    return out


def _fft_plan(n, radices, sign):
    assert math.prod(radices) == n
    passes = []
    s, ncur = 1, n
    for r in radices:
        m = ncur // r
        bfs = []
        for p in range(m):
            tws = [cmath.exp(sign * 2j * math.pi * p * k / ncur) for k in range(r)]
            for q in range(s):
                ins = [q + s * (p + m * j) for j in range(r)]
                outs = [q + s * (r * p + k) for k in range(r)]
                bfs.append((ins, outs, tws))
        passes.append(bfs)
        ncur, s = m, s * r
    return passes


def _run_fft(plan, sign, load, store, temps):
    npass = len(plan)
    for pi, bfs in enumerate(plan):
        if pi == 0:
            ld = load
        else:
            tr, ti = temps[(pi - 1) % 2]
            ld = lambda i, tr=tr, ti=ti: (tr[i], ti[i])
        if pi == npass - 1:
            st = store
        else:
            tr2, ti2 = temps[pi % 2]

            def st(i, x, tr2=tr2, ti2=ti2):
                tr2[i] = x[0]
                ti2[i] = x[1]
        for ins, outs, tws in bfs:
            ys = _small_dft([ld(i) for i in ins], sign)
            for o, y, tw in zip(outs, ys, tws):
                st(o, _dense(_cmul_const(y, tw)))


def _radices(n):
    out = []
    while n > 1:
        r = 8 if n % 8 == 0 else (4 if n % 4 == 0 else 2)
        out.append(r)
        n //= r
    return tuple(out)


def _norm_matmul_kernel(x_ref, nw_ref, w_ref, o_ref, xn_ref):
    @pl.when(pl.program_id(1) == 0)
    def _():
        x = x_ref[...]
        inv = lax.rsqrt(jnp.mean(x * x, axis=-1, keepdims=True) + RMS_EPS)
        xn_ref[...] = ((x * inv) * nw_ref[...]).astype(BF16)

    o_ref[...] = jnp.dot(xn_ref[...], w_ref[...], preferred_element_type=F32)


def _norm_matmul(x2d, norm_w, w, tm=2048, tn=1024):
    t, d = x2d.shape
    n = w.shape[1]
    tm = min(tm, t)
    tn = min(tn, n)
    return pl.pallas_call(
        _norm_matmul_kernel,
        grid=(t // tm, n // tn),
        in_specs=[
            pl.BlockSpec((tm, d), lambda i, j: (i, 0)),
            pl.BlockSpec((1, d), lambda i, j: (0, 0)),
            pl.BlockSpec((d, tn), lambda i, j: (0, j)),
        ],
        out_specs=pl.BlockSpec((tm, tn), lambda i, j: (i, j)),
        out_shape=jax.ShapeDtypeStruct((t, n), F32),
        scratch_shapes=[pltpu.VMEM((tm, d), BF16)],
        compiler_params=_cparams(("arbitrary", "arbitrary")),
        name="norm_matmul",
    )(x2d, norm_w.reshape(1, d), w.astype(BF16))


HALO = 16


def _norm_matmul_conv_kernel(x_ref, xp_ref, xn_ref, nw_ref, w_ref, cw_ref, cb_ref, *rest,
                             tm, nconv, seq_tiles, act, dual):
    if dual:
        o_ref, oc_ref, xs_ref = rest
    else:
        o_ref, xs_ref = rest
    i = pl.program_id(0)
    j = pl.program_id(1)

    @pl.when(j == 0)
    def _():
        def nrm(x):
            inv = lax.rsqrt(jnp.mean(x * x, axis=-1, keepdims=True) + RMS_EPS)
            return (x * inv) * nw_ref[...]

        first = (i % seq_tiles) == 0
        last = (i % seq_tiles) == seq_tiles - 1
        xs_ref[0:HALO] = jnp.where(first, 0.0, nrm(xp_ref[...])).astype(BF16)
        xs_ref[HALO:HALO + tm] = nrm(x_ref[...]).astype(BF16)
        xs_ref[HALO + tm:] = jnp.where(last, 0.0, nrm(xn_ref[...])).astype(BF16)

    @pl.when(j < nconv)
    def _():
        rows = tm + 2 * HALO
        tn = w_ref.shape[1]
        nsplit = 4 if tn % (8 * LANES) == 0 else 1
        for c in range(nsplit):
            cs = slice(c * tn // nsplit, (c + 1) * tn // nsplit)
            acc = jnp.dot(xs_ref[...], w_ref[:, cs], preferred_element_type=F32)
            y = (acc * cw_ref[1:2, cs] + pltpu.roll(acc, 1, 0) * cw_ref[0:1, cs]
                 + pltpu.roll(acc, rows - 1, 0) * cw_ref[2:3, cs] + cb_ref[:, cs])
            y = y[HALO:HALO + tm]
            if act:
                y = _silu(y)
            if dual:
                oc_ref[:, cs] = y
                o_ref[:, cs] = acc[HALO:HALO + tm]
            else:
                o_ref[:, cs] = y

    @pl.when(j >= nconv)
    def _():
        o_ref[...] = jnp.dot(xs_ref[HALO:HALO + tm], w_ref[...], preferred_element_type=F32)


def _norm_matmul_conv(x2d, norm_w, w, conv_w, conv_b, seq_len, act, dual, tm=2048, tn=1024):
    t, d = x2d.shape
    n = w.shape[1]
    nc = conv_w.shape[1]
    tm = min(tm, seq_len)
    tn = min(tn, nc)
    nconv = nc // tn
    hb = tm // HALO
    nhb = t // HALO
    cj = lambda j: jnp.minimum(j, nconv - 1)
    in_specs = [
        pl.BlockSpec((tm, d), lambda i, j: (i, 0)),
        pl.BlockSpec((HALO, d), lambda i, j: (jnp.maximum(i * hb - 1, 0), 0)),
        pl.BlockSpec((HALO, d), lambda i, j: (jnp.minimum((i + 1) * hb, nhb - 1), 0)),
        pl.BlockSpec((1, d), lambda i, j: (0, 0)),
        pl.BlockSpec((d, tn), lambda i, j: (0, j)),
        pl.BlockSpec((3, tn), lambda i, j: (0, cj(j))),
        pl.BlockSpec((1, tn), lambda i, j: (0, cj(j))),
    ]
    out_specs = pl.BlockSpec((tm, tn), lambda i, j: (i, j))
    out_shape = jax.ShapeDtypeStruct((t, n), F32)
    if dual:
        out_specs = [out_specs, pl.BlockSpec((tm, tn), lambda i, j: (i, cj(j)))]
        out_shape = [out_shape, jax.ShapeDtypeStruct((t, nc), F32)]
    return pl.pallas_call(
        functools.partial(_norm_matmul_conv_kernel, tm=tm, nconv=nconv, seq_tiles=seq_len // tm,
                          act=act, dual=dual),
        grid=(t // tm, n // tn),
        in_specs=in_specs,
        out_specs=out_specs,
        out_shape=out_shape,
        scratch_shapes=[pltpu.VMEM((tm + 2 * HALO, d), BF16)],
        compiler_params=_cparams(("arbitrary", "arbitrary")),
        name="norm_matmul_conv",
    )(x2d, x2d, x2d, norm_w.reshape(1, d), w.astype(BF16), conv_w, conv_b.reshape(1, nc))


def _out_kernel(*refs, mode, final):
    if mode == "mlstm":
        hf_ref, hb_ref, xc_ref, g_ref, hn_ref, sk_ref, w_ref, r_ref = refs[:8]
        rest = refs[8:]
        h = hf_ref[...] + hb_ref[...]
        parts = []
        for hd in range(ML_HEADS):
            seg = h[:, hd * ML_HEAD_DIM:(hd + 1) * ML_HEAD_DIM]
            mu = jnp.mean(seg, axis=-1, keepdims=True)
            cen = seg - mu
            var = jnp.mean(cen * cen, axis=-1, keepdims=True)
            parts.append(cen * lax.rsqrt(var + LN_EPS))
        a = jnp.concatenate(parts, axis=-1) * hn_ref[...] + sk_ref[...] * xc_ref[...]
    else:
        a_ref, g_ref, w_ref, r_ref = refs[:4]
        rest = refs[4:]
        a = a_ref[...]
    if final:
        fn_ref, o_ref = rest
    else:
        (o_ref,) = rest
    y = jnp.dot((a * _silu(g_ref[...])).astype(BF16), w_ref[...], preferred_element_type=F32)
    x = r_ref[...] + y
    if final:
        inv = lax.rsqrt(jnp.mean(x * x, axis=-1, keepdims=True) + RMS_EPS)
        x = (x * inv) * fn_ref[...]
    o_ref[...] = x


def _out_matmul(acts, gate_arr, gate_blk, w, resid, final_norm=None, extra=(), tm=512):
    t = resid.shape[0]
    tm = min(tm, t)
    e = D_INNER
    mode = "mlstm" if len(acts) == 3 else "plain"
    in_specs, args = [], []
    for a in acts:
        if isinstance(a, tuple):
            arr, idx = a
            in_specs.append(pl.BlockSpec((None, tm, e), lambda i, idx=idx: (idx, i, 0)))
            args.append(arr)
        else:
            in_specs.append(pl.BlockSpec((tm, e), lambda i: (i, 0)))
            args.append(a)
    in_specs.append(pl.BlockSpec((tm, e), lambda i: (i, gate_blk)))
    args.append(gate_arr)
    for v in extra:
        in_specs.append(pl.BlockSpec((1, e), lambda i: (0, 0)))
        args.append(v.reshape(1, e))
    in_specs.append(pl.BlockSpec((e, D_MODEL), lambda i: (0, 0)))
    args.append(w.astype(BF16))
    in_specs.append(pl.BlockSpec((tm, D_MODEL), lambda i: (i, 0)))
    args.append(resid)
    if final_norm is not None:
        in_specs.append(pl.BlockSpec((1, D_MODEL), lambda i: (0, 0)))
        args.append(final_norm.reshape(1, D_MODEL))
    return pl.pallas_call(
        functools.partial(_out_kernel, mode=mode, final=final_norm is not None),
        grid=(t // tm,),
        in_specs=in_specs,
        out_specs=pl.BlockSpec((tm, D_MODEL), lambda i: (i, 0)),
        out_shape=jax.ShapeDtypeStruct((t, D_MODEL), F32),
        compiler_params=_cparams(("arbitrary",)),
        name="out_matmul_" + mode,
    )(*args)


def _filter_kernel(feat_ref, t_ref, w1_ref, b1_ref, w2_ref, b2_ref, w3_ref, b3_ref, fr_ref,
                   wo_ref, dl_ref, o_ref):
    fr = fr_ref[...]
    h = jnp.sin(fr * (_dot3(feat_ref[...], w1_ref[0], w1_ref[1]) + b1_ref[...]))
    h = jnp.sin(fr * (_dot3(h, w2_ref[0], w2_ref[1]) + b2_ref[...]))
    h = jnp.sin(fr * (_dot3(h, w3_ref[0], w3_ref[1]) + b3_ref[...]))
    tcol = t_ref[...]
    decay = jnp.exp(-tcol[:, 0:1] * dl_ref[...]) * tcol[:, 1:2]
    for f in range(2):
        k = _dot3(h, wo_ref[0, f], wo_ref[1, f])
        o_ref[f] = k * decay


def _hyena_two_sided_filters(l, fw1, fb1, fw2, fb2, fw3, fb3, ffreq, fwout, tr=1024):
    n = 2 * l
    e = D_INNER
    tr = min(tr, l)
    idx = jnp.arange(n)
    src = jnp.where(idx < l, idx, n - idx)
    valid = (idx != l).astype(F32)
    src = jnp.where(idx == l, 0, src)
    t_all = jnp.linspace(0.0, 1.0, l, dtype=F32)
    bands = (HY_EMB - 1) // 2
    t_src = t_all[src]
    ang = 2.0 * math.pi * src.astype(F32)[:, None] / l
    fb = jnp.linspace(1e-4, bands - 1, bands, dtype=F32)[None, :]
    feat = jnp.concatenate([t_src[:, None], jnp.cos(fb * ang), -jnp.sin(fb * ang)], axis=-1)
    feat = jnp.pad(feat, ((0, 0), (0, LANES - HY_EMB)))
    tcol = jnp.stack([t_src, valid], axis=-1)
    w1p = jnp.pad(fw1, ((0, LANES - HY_EMB), (0, 0)))
    deltas = jnp.abs(jnp.linspace(HY_MIN_DECAY, HY_MAX_DECAY, e, dtype=F32)).reshape(1, e)
    wo = _split_hi_lo(fwout.reshape(HY_FILT, 4, e).transpose(1, 0, 2))
    nhalf = l // tr
    row = lambda v: v.reshape(1, HY_FILT)
    full = lambda shp: pl.BlockSpec(shp, lambda i: tuple(0 for _ in shp))
    return pl.pallas_call(
        _filter_kernel,
        grid=(n // tr,),
        in_specs=[
            pl.BlockSpec((tr, LANES), lambda i: (i, 0)),
            pl.BlockSpec((tr, 2), lambda i: (i, 0)),
            full((2, LANES, HY_FILT)), full((1, HY_FILT)),
            full((2, HY_FILT, HY_FILT)), full((1, HY_FILT)),
            full((2, HY_FILT, HY_FILT)), full((1, HY_FILT)),
            full((1, HY_FILT)),
            pl.BlockSpec((2, 2, HY_FILT, e), lambda i: (0, i // nhalf, 0, 0)),
            full((1, e)),
        ],
        out_specs=pl.BlockSpec((2, tr, e), lambda i: (0, i, 0)),
        out_shape=jax.ShapeDtypeStruct((2, n, e), F32),
        compiler_params=_cparams(("arbitrary",)),
        name="hyena_filter",
    )(feat, tcol, _split_hi_lo(w1p), row(fb1), _split_hi_lo(fw2), row(fb2), _split_hi_lo(fw3),
      row(fb3), row(ffreq), wo, deltas)


FFT_SCRATCH = 6
FFT_INTERLEAVE = 2


def _interleaved_loop(n, one, scratch):
    u = FFT_INTERLEAVE if n % FFT_INTERLEAVE == 0 else 1

    def body(c, carry):
        for k in range(u):
            one(c * u + k, scratch[k * FFT_SCRATCH:(k + 1) * FFT_SCRATCH])
        return carry

    lax.fori_loop(0, n // u, body, 0)


def _fft_scratch(n):
    return [pltpu.VMEM((n, SUBLANES, LANES), F32) for _ in range(FFT_SCRATCH * FFT_INTERLEAVE)]


def _fft_a_kernel(x_ref, zr_ref, zi_ref, *scratch, n1, nj, packed):
    plan = _fft_plan(n1, _radices(n1), -1)

    def one(j, scr):
        temps = [(scr[0], scr[1]), (scr[2], scr[3])]
        fr, fi = scr[4], scr[5]
        rows = pl.ds(pl.multiple_of(j * SUBLANES, SUBLANES), SUBLANES)

        def load(i):
            if packed:
                if i >= n1 // 2:
                    return (None, None)
                return (x_ref[0, 0, i, rows, :], x_ref[1, 0, i, rows, :])
            return (x_ref[0, i, rows, :], x_ref[1, i, rows, :])

        def store(i, x):
            fr[i] = x[0]
            fi[i] = x[1]

        _run_fft(plan, -1, load, store, temps)
        if packed:
            planes = ((0, fr, fi),)
        else:
            ar, ai, br, bi = scr[0], scr[1], scr[2], scr[3]
            for k in range(n1):
                m = (n1 - k) % n1
                ar[k] = fr[k] + fr[m]
                ai[k] = fi[k] - fi[m]
                br[k] = fi[k] + fi[m]
                bi[k] = fr[m] - fr[k]
            planes = ((0, ar, ai), (1, br, bi))
        for f, pr, pi in planes:
            for a in range(n1 // SUBLANES):
                sl = slice(a * SUBLANES, (a + 1) * SUBLANES)
                zr_ref[f, a, 0, rows] = jnp.swapaxes(pr[sl], 0, 1)
                zi_ref[f, a, 0, rows] = jnp.swapaxes(pi[sl], 0, 1)

    _interleaved_loop(nj, one, scratch)


def _fft_a(x, n1, n2, packed, lane_blk0=0):
    nl = N_LANE_BLOCKS
    if packed:
        p = x.shape[1]
        bn2 = min(max(FFT_A_ROWS // n1, SUBLANES), n2)
        in_spec = pl.BlockSpec((2, 1, n1 // 2, bn2, LANES),
                               lambda pp, lb, j: (0, pp, 0, j, lane_blk0 + lb))
        nf = 1
    else:
        p = 1
        bn2 = min(max(FFT_A_ROWS // (2 * n1), SUBLANES), n2)
        in_spec = pl.BlockSpec((2, n1, bn2, LANES), lambda pp, lb, j: (0, 0, j, lb))
        nf = 2
    out_spec = pl.BlockSpec((nf, n1 // SUBLANES, 1, bn2, SUBLANES, LANES),
                            lambda pp, lb, j: (pp, 0, lb, j, 0, 0))
    shp = jax.ShapeDtypeStruct((p * nf, n1 // SUBLANES, nl, n2, SUBLANES, LANES), F32)
    return pl.pallas_call(
        functools.partial(_fft_a_kernel, n1=n1, nj=bn2 // SUBLANES, packed=packed),
        grid=(p, nl, n2 // bn2),
        in_specs=[in_spec],
        out_specs=[out_spec, out_spec],
        out_shape=[shp, shp],
        scratch_shapes=_fft_scratch(n1),
        compiler_params=_cparams(("arbitrary", "arbitrary", "arbitrary")),
        name="fft_a_packed" if packed else "fft_a_pair",
    )(x)


def _fft_mid_kernel(*refs, n2, ncol, nl, conv):
    if conv:
        zr_ref, zi_ref, kr_ref, ki_ref, twr_ref, twi_ref, or_ref, oi_ref = refs[:8]
        scratch = refs[8:]
    else:
        zr_ref, zi_ref, twr_ref, twi_ref, or_ref, oi_ref = refs[:6]
        scratch = refs[6:]
    fwd = _fft_plan(n2, FFT_N2_RADICES, -1)
    inv = _fft_plan(n2, FFT_N2_RADICES, +1)

    def one(c, scr):
        temps = [(scr[0], scr[1]), (scr[2], scr[3])]
        sr, si = scr[4], scr[5]
        i = c // nl
        g = c % nl

        def tw(r):
            return (twr_ref[i, r], twi_ref[i, r])

        def load(r):
            return _cmul((zr_ref[0, i, g, r], zi_ref[0, i, g, r]), tw(r))

        if conv:
            def store_spec(r, x):
                y = _cmul(x, (kr_ref[0, i, g, r], ki_ref[0, i, g, r]))
                sr[r] = y[0]
                si[r] = y[1]

            _run_fft(fwd, -1, load, store_spec, temps)

            def store_out(r, x):
                tr, ti = tw(r)
                or_ref[0, i, g, r] = x[0] * tr + x[1] * ti
                oi_ref[0, i, g, r] = x[1] * tr - x[0] * ti

            _run_fft(inv, +1, lambda r: (sr[r], si[r]), store_out, temps)
        else:
            def store_spec(r, x):
                or_ref[0, i, g, r] = x[0]
                oi_ref[0, i, g, r] = x[1]

            _run_fft(fwd, -1, load, store_spec, temps)

    _interleaved_loop(ncol, one, scratch)


def _twiddle_table(n1, n2, scale=1.0):
    n = n1 * n2
    k1 = jnp.arange(n1).reshape(n1 // SUBLANES, 1, SUBLANES)
    m = (k1 * jnp.arange(n2).reshape(1, n2, 1)) % n
    ang = m.astype(F32) * (-2.0 * math.pi / n)
    shape = (n1 // SUBLANES, n2, SUBLANES, LANES)
    return (jnp.broadcast_to((scale * jnp.cos(ang))[..., None], shape),
            jnp.broadcast_to((scale * jnp.sin(ang))[..., None], shape))


def _fft_mid(zr, zi, tw, filt=None):
    p, na, nl, n2 = zr.shape[:4]
    conv = filt is not None
    bl = min(FFT_MID_COLS if conv else 2 * FFT_MID_COLS, nl)
    zspec = pl.BlockSpec((1, 1, bl, n2, SUBLANES, LANES), lambda i, g, pp: (pp, i, g, 0, 0, 0))
    twspec = pl.BlockSpec((1, n2, SUBLANES, LANES), lambda i, g, pp: (i, 0, 0, 0))
    in_specs = [zspec, zspec]
    args = [zr, zi]
    if conv:
        fidx = filt[2]
        kspec = pl.BlockSpec((1, 1, bl, n2, SUBLANES, LANES), lambda i, g, pp: (fidx, i, g, 0, 0, 0))
        in_specs += [kspec, kspec]
        args += [filt[0], filt[1]]
    in_specs += [twspec, twspec]
    args += [tw[0], tw[1]]
    shp = jax.ShapeDtypeStruct(zr.shape, F32)
    return pl.pallas_call(
        functools.partial(_fft_mid_kernel, n2=n2, ncol=bl, nl=bl, conv=conv),
        grid=(na, nl // bl, p),
        in_specs=in_specs,
        out_specs=[zspec, zspec],
        out_shape=[shp, shp],
        scratch_shapes=_fft_scratch(n2),
        compiler_params=_cparams(("arbitrary", "arbitrary", "arbitrary")),
        name="fft_mid_conv" if conv else "fft_mid_spec",
    )(*args)


def _fft_ainv_kernel(zr_ref, zi_ref, u_ref, x_ref, d_ref, o_ref, *scratch, n1, nj):
    plan = _fft_plan(n1, _radices(n1), +1)
    d = d_ref[...]

    def one(j, scr):
        temps = [(scr[0], scr[1]), (scr[2], scr[3])]
        lr, li = scr[4], scr[5]
        rows = pl.ds(pl.multiple_of(j * SUBLANES, SUBLANES), SUBLANES)
        for a in range(n1 // SUBLANES):
            sl = slice(a * SUBLANES, (a + 1) * SUBLANES)
            lr[sl] = jnp.swapaxes(zr_ref[0, a, 0, rows], 0, 1)
            li[sl] = jnp.swapaxes(zi_ref[0, a, 0, rows], 0, 1)

        def store(i, y):
            if i < n1 // 2:
                o_ref[0, 0, i, rows, :] = x_ref[0, 0, i, rows, :] * (y[0] + u_ref[0, 0, i, rows, :] * d)
                o_ref[1, 0, i, rows, :] = x_ref[1, 0, i, rows, :] * (y[1] + u_ref[1, 0, i, rows, :] * d)

        _run_fft(plan, +1, lambda i: (lr[i], li[i]), store, temps)

    _interleaved_loop(nj, one, scratch)


def _fft_ainv(zr, zi, u, u_blk0, x, x_blk0, d, d_row):
    p, na, nl, n2 = zr.shape[:4]
    n1 = na * SUBLANES
    bn2 = min(max(FFT_A_ROWS // n1, SUBLANES), n2)
    zspec = pl.BlockSpec((1, na, 1, bn2, SUBLANES, LANES), lambda pp, lb, j: (pp, 0, lb, j, 0, 0))

    def sspec(blk0):
        return pl.BlockSpec((2, 1, n1 // 2, bn2, LANES), lambda pp, lb, j: (0, pp, 0, j, blk0 + lb))

    return pl.pallas_call(
        functools.partial(_fft_ainv_kernel, n1=n1, nj=bn2 // SUBLANES),
        grid=(p, nl, n2 // bn2),
        in_specs=[zspec, zspec, sspec(u_blk0), sspec(x_blk0),
                  pl.BlockSpec((1, LANES), lambda pp, lb, j: (0, lb))],
        out_specs=sspec(0),
        out_shape=jax.ShapeDtypeStruct((2, p, n1 // 2, n2, D_INNER), F32),
        scratch_shapes=_fft_scratch(n1),
        compiler_params=_cparams(("arbitrary", "arbitrary", "arbitrary")),
        name="fft_a_inv",
    )(zr, zi, u, x, d[d_row].reshape(1, D_INNER))


def _hyena_layer(x, norm_w, w_in, conv_w, conv_b, fw1, fb1, fw2, fb2, fw3, fb3, ffreq, fwout,
                 bias_d, w_out, final_norm=None):
    b, l, _ = x.shape
    e = D_INNER
    n = 2 * l
    n2 = FFT_N2
    n1 = n // n2
    p = b // 2
    nl = N_LANE_BLOCKS
    x2d = x.reshape(b * l, D_MODEL)
    proj = _norm_matmul_conv(x2d, norm_w, w_in, conv_w, conv_b, l, act=False, dual=False)
    pv = proj.reshape(2, p, n1 // 2, n2, 4 * e)

    tw = _twiddle_table(n1, n2)
    kt = _hyena_two_sided_filters(l, fw1, fb1, fw2, fb2, fw3, fb3, ffreq, fwout)
    kr, ki = _fft_a(kt.reshape(2, n1, n2, e), n1, n2, packed=False)
    kfr, kfi = _fft_mid(kr, ki, _twiddle_table(n1, n2, scale=0.5 / n))

    zr, zi = _fft_a(pv, n1, n2, packed=True)
    zr, zi = _fft_mid(zr, zi, tw, filt=(kfr, kfi, 0))
    z1 = _fft_ainv(zr, zi, pv, 0, pv, nl, bias_d, 0)
    zr, zi = _fft_a(z1, n1, n2, packed=True)
    zr, zi = _fft_mid(zr, zi, tw, filt=(kfr, kfi, 1))
    z2 = _fft_ainv(zr, zi, z1, 0, pv, 2 * nl, bias_d, 1)
    out = _out_matmul([z2.reshape(b * l, e)], proj, 3, w_out, x2d, final_norm)
    return out.reshape(b, l, D_MODEL)


def _swa_kernel(q_ref, kp_ref, kc_ref, kn_ref, vp_ref, vc_ref, vn_ref,
                cq_ref, sq_ref, ckp_ref, skp_ref, ckn_ref, skn_ref, sink_ref, o_ref, *, nb):
    n = pl.program_id(1)
    lane = lax.broadcasted_iota(jnp.int32, (BLOCK, HEAD_DIM), 1)
    half = ROT_DIM // 2

    def rope(x, c, s):
        sw = jnp.where(lane < half, pltpu.roll(x, HEAD_DIM - half, 1), pltpu.roll(x, half, 1))
        return x * c + sw * s

    cq, sq = cq_ref[...], sq_ref[...]
    cks = (ckp_ref[...], cq, ckn_ref[...])
    sks = (skp_ref[...], sq, skn_ref[...])
    qi = lax.broadcasted_iota(jnp.int32, (BLOCK, 3 * BLOCK), 0)
    kj = lax.broadcasted_iota(jnp.int32, (BLOCK, 3 * BLOCK), 1) - BLOCK
    mask = jnp.abs(qi - kj) <= WINDOW
    mask = mask & ((kj >= 0) | (n > 0)) & ((kj < BLOCK) | (n < nb - 1))
    mask4 = jnp.concatenate([mask] * GROUP, axis=0)
    scale = HEAD_DIM ** -0.5
    for h in range(N_KV_HEADS):
        sl = slice(h * HEAD_DIM, (h + 1) * HEAD_DIM)
        kparts = [rope(r[0, :, sl], c, s) for r, c, s in zip((kp_ref, kc_ref, kn_ref), cks, sks)]
        kh = jnp.concatenate(kparts, axis=0).astype(BF16)
        vh = jnp.concatenate([vp_ref[0, :, sl], vc_ref[0, :, sl], vn_ref[0, :, sl]],
                             axis=0).astype(BF16)
        qs = []
        for g in range(GROUP):
            hq = h * GROUP + g
            qs.append(rope(q_ref[0, :, hq * HEAD_DIM:(hq + 1) * HEAD_DIM], cq, sq))
        qh = (jnp.concatenate(qs, axis=0) * scale).astype(BF16)
        s = lax.dot_general(qh, kh, (((1,), (1,)), ((), ())), preferred_element_type=F32)
        s = jnp.where(mask4, s, -jnp.inf)
        sink = jnp.concatenate(
            [jnp.full((BLOCK, 1), 1.0, F32) * sink_ref[h * GROUP + g] for g in range(GROUP)], axis=0)
        m = jnp.maximum(jnp.max(s, axis=-1, keepdims=True), sink)
        pr = jnp.exp(s - m)
        den = jnp.sum(pr, axis=-1, keepdims=True) + jnp.exp(sink - m)
        o = jnp.dot(pr.astype(BF16), vh, preferred_element_type=F32) * (1.0 / den)
        for g in range(GROUP):
            hq = h * GROUP + g
            o_ref[0, :, hq * HEAD_DIM:(hq + 1) * HEAD_DIM] = o[g * BLOCK:(g + 1) * BLOCK]


def _rope_tables(l):
    half = ROT_DIM // 2
    inv = ROPE_THETA ** (-jnp.arange(half, dtype=F32) / half)
    ang = jnp.arange(l, dtype=F32)[:, None] * inv[None, :]
    cos, sin = jnp.cos(ang), jnp.sin(ang)
    ones = jnp.ones((l, HEAD_DIM - ROT_DIM), F32)
    return (jnp.concatenate([cos, cos, ones], axis=-1),
            jnp.concatenate([-sin, sin, 0.0 * ones], axis=-1))


def _swa_layer(x, norm_w, w_in, sink, w_out, final_norm=None):
    b, l, _ = x.shape
    nb = l // BLOCK
    x2d = x.reshape(b * l, D_MODEL)
    w_perm = jnp.concatenate([w_in[:, :Q_DIM], w_in[:, Q_DIM + 2 * KV_DIM:],
                              w_in[:, Q_DIM:Q_DIM + 2 * KV_DIM]], axis=1)
    proj = _norm_matmul(x2d, norm_w, w_perm)
    p3 = proj.reshape(b, l, 2 * Q_DIM + 2 * KV_DIM)
    ct, st = _rope_tables(l)
    kblk = 2 * Q_DIM // KV_DIM
    prv = lambda bb, i: jnp.maximum(i - 1, 0)
    nxt = lambda bb, i: jnp.minimum(i + 1, nb - 1)
    kv = lambda col, f: pl.BlockSpec((1, BLOCK, KV_DIM), lambda bb, i: (bb, f(bb, i), col))
    cur = lambda bb, i: i
    tab = lambda f: pl.BlockSpec((BLOCK, HEAD_DIM), lambda bb, i: (f(bb, i), 0))
    o = pl.pallas_call(
        functools.partial(_swa_kernel, nb=nb),
        grid=(b, nb),
        in_specs=[
            pl.BlockSpec((1, BLOCK, Q_DIM), lambda bb, i: (bb, i, 0)),
            kv(kblk, prv), kv(kblk, cur), kv(kblk, nxt),
            kv(kblk + 1, prv), kv(kblk + 1, cur), kv(kblk + 1, nxt),
            tab(cur), tab(cur), tab(prv), tab(prv), tab(nxt), tab(nxt),
            pl.BlockSpec(memory_space=pltpu.SMEM),
        ],
        out_specs=pl.BlockSpec((1, BLOCK, Q_DIM), lambda bb, i: (bb, i, 0)),
        out_shape=jax.ShapeDtypeStruct((b, l, Q_DIM), F32),
        compiler_params=_cparams(("arbitrary", "arbitrary")),
        name="swa_attention",
    )(p3, p3, p3, p3, p3, p3, p3, ct, st, ct, st, ct, st, sink)
    out = _out_matmul([o.reshape(b * l, D_INNER)], proj, 1, w_out, x2d, final_norm)
    return out.reshape(b, l, D_MODEL)


def _ml_qkv_kernel(xc_ref, xi_ref, wq_ref, wk_ref, wv_ref, wg_ref, bg_ref, q_ref, k_ref, v_ref,
                   g_ref, gt_ref):
    nt = D_INNER // BD_TILE
    ng = 4 * ML_HEADS
    qs, ks, vs = [], [], []
    for j in range(nt):
        sl = slice(j * BD_TILE, (j + 1) * BD_TILE)
        xc = xc_ref[:, sl]
        qs.append(_dot3(xc, wq_ref[0, j], wq_ref[1, j]))
        ks.append(_dot3(xc, wk_ref[0, j], wk_ref[1, j]))
        vs.append(_dot3(xi_ref[:, sl], wv_ref[0, j], wv_ref[1, j]))
    qkv = jnp.concatenate(qs + ks + vs, axis=-1)
    xh = qkv.astype(BF16)
    xl = (qkv - xh.astype(F32)).astype(BF16)
    both = jnp.dot(xh, wg_ref[...], preferred_element_type=F32)
    low = jnp.dot(xl, wg_ref[:, :ng], preferred_element_type=F32)
    gates = both[:, :ng] + (both[:, ng:] + low) + bg_ref[...]
    g_ref[...] = gates
    gt_ref[...] = gates.T
    e = D_INNER
    q_ref[...] = qkv[:, :e].astype(BF16)
    k_ref[...] = (qkv[:, e:2 * e] * (ML_HEAD_DIM ** -0.5)).astype(BF16)
    v_ref[...] = qkv[:, 2 * e:].astype(BF16)


BD_TILE = 256


def _blockdiag_tiles(w):
    per = BD_TILE // ML_QKV_BLOCK
    wt = w.reshape(D_INNER // BD_TILE, per, ML_QKV_BLOCK, ML_QKV_BLOCK)
    eye = jnp.eye(per, dtype=w.dtype)
    return jnp.einsum("tnio,nm->tnimo", wt, eye).reshape(D_INNER // BD_TILE, BD_TILE, BD_TILE)


def _ml_qkv(xc2d, proj, wq, wk, wv, w_gate, b_gate, tm=256):
    t = xc2d.shape[0]
    tm = min(tm, t)
    e = D_INNER
    ng = 4 * ML_HEADS
    nt = e // BD_TILE
    wspec = pl.BlockSpec((2, nt, BD_TILE, BD_TILE), lambda i: (0, 0, 0, 0))
    act = jax.ShapeDtypeStruct((t, e), BF16)
    tiles = lambda w: _split_hi_lo(_blockdiag_tiles(w))
    wg = _split_hi_lo(w_gate)
    wg = jnp.concatenate([wg[0], wg[1]], axis=1)
    return pl.pallas_call(
        _ml_qkv_kernel,
        grid=(t // tm,),
        in_specs=[
            pl.BlockSpec((tm, e), lambda i: (i, 0)),
            pl.BlockSpec((tm, e), lambda i: (i, 0)),
            wspec, wspec, wspec,
            pl.BlockSpec((3 * e, 2 * ng), lambda i: (0, 0)),
            pl.BlockSpec((1, ng), lambda i: (0, 0)),
        ],
        out_specs=[pl.BlockSpec((tm, e), lambda i: (i, 0))] * 3
        + [pl.BlockSpec((tm, ng), lambda i: (i, 0)), pl.BlockSpec((ng, tm), lambda i: (0, i))],
        out_shape=[act, act, act, jax.ShapeDtypeStruct((t, ng), F32),
                   jax.ShapeDtypeStruct((ng, t), F32)],
        compiler_params=_cparams(("arbitrary",)),
        name="mlstm_qkv_gates",
    )(xc2d, proj, tiles(wq), tiles(wk), tiles(wv), wg, b_gate.reshape(1, ng))


def _log_sigmoid(x):
    return jnp.minimum(x, 0.0) - jnp.log(1.0 + jnp.exp(-jnp.abs(x)))


def _ml_chunk(d, hd, q_ref, k_ref, v_ref, g, gt, o_ref, c_ref, n_ref, m_ref):
    lc = ML_CHUNK
    ng = 4 * ML_HEADS
    dh = ML_HEAD_DIM
    hs = slice(hd * dh, (hd + 1) * dh)
    col_i = 2 * ML_HEADS * d + hd
    col_f = col_i + ML_HEADS
    lane = lax.broadcasted_iota(jnp.int32, (lc, ng), 1)
    i_col = jnp.sum(jnp.where(lane == col_i, g, 0.0), axis=-1, keepdims=True)
    f_col = _log_sigmoid(jnp.sum(jnp.where(lane == col_f, g, 0.0), axis=-1, keepdims=True))
    sub = lax.broadcasted_iota(jnp.int32, (ng, lc), 0)
    i_row = jnp.sum(jnp.where(sub == col_i, gt, 0.0), axis=0, keepdims=True)
    f_row = _log_sigmoid(jnp.sum(jnp.where(sub == col_f, gt, 0.0), axis=0, keepdims=True))

    t_idx = lax.broadcasted_iota(jnp.int32, (lc, lc), 0)
    s_idx = lax.broadcasted_iota(jnp.int32, (lc, lc), 1)
    if d == 0:
        causal = s_idx <= t_idx
        causal_t = t_idx <= s_idx
    else:
        causal = s_idx >= t_idx
        causal_t = t_idx >= s_idx
    b_col = jnp.sum(jnp.where(causal, f_row, 0.0), axis=-1, keepdims=True)
    b_row = jnp.sum(jnp.where(causal_t, f_col, 0.0), axis=0, keepdims=True)
    m_prev = m_ref[d, hd]
    dmat = jnp.where(causal, b_col - b_row + i_row, -jnp.inf)
    inter = b_col + m_prev
    m_out = jnp.maximum(inter, jnp.max(dmat, axis=-1, keepdims=True))
    w_intra = jnp.exp(dmat - m_out)
    w_state = jnp.exp(inter - m_out)

    q = q_ref[:, hs]
    k = k_ref[:, hs]
    v = v_ref[:, hs]
    cst = c_ref[d, hd]
    s = lax.dot_general(q, k, (((1,), (1,)), ((), ())), preferred_element_type=F32) * w_intra
    num = jnp.dot(s.astype(BF16), v, preferred_element_type=F32)
    num = num + w_state * jnp.dot(q, cst.astype(BF16), preferred_element_type=F32)
    qn = jnp.sum(q.astype(F32) * n_ref[d, hd], axis=-1, keepdims=True)
    den = jnp.sum(s, axis=-1, keepdims=True) + w_state * qn
    o_ref[:, hs] = num / jnp.maximum(jnp.abs(den), jnp.exp(-m_out))

    last = lc - 1 if d == 0 else 0
    bl = jnp.sum(f_row, axis=-1, keepdims=True)
    m_new = m_out[last:last + 1]
    wk = jnp.exp(bl - b_col + i_col - m_new)
    decay = jnp.exp(bl + m_prev - m_new)
    vw = (v.astype(F32) * wk).astype(BF16)
    c_ref[d, hd] = decay * cst + lax.dot_general(
        k, vw, (((0,), (0,)), ((), ())), preferred_element_type=F32)
    n_ref[d, hd] = decay * n_ref[d, hd] + jnp.sum(k.astype(F32) * wk, axis=0, keepdims=True)
    m_ref[d, hd] = m_new


def _ml_scan_kernel(qf_ref, kf_ref, vf_ref, gf_ref, gtf_ref, qb_ref, kb_ref, vb_ref, gb_ref, gtb_ref,
                    of_ref, ob_ref, c_ref, n_ref, m_ref):
    @pl.when(pl.program_id(1) == 0)
    def _():
        c_ref[...] = jnp.zeros_like(c_ref)
        n_ref[...] = jnp.zeros_like(n_ref)
        m_ref[...] = jnp.zeros_like(m_ref)

    dirs = ((qf_ref, kf_ref, vf_ref, gf_ref, gtf_ref, of_ref),
            (qb_ref, kb_ref, vb_ref, gb_ref, gtb_ref, ob_ref))
    for d, (q_ref, k_ref, v_ref, g_ref, gt_ref, o_ref) in enumerate(dirs):
        g = g_ref[...]
        gt = gt_ref[...]
        for hd in range(ML_HEADS):
            _ml_chunk(d, hd, q_ref, k_ref, v_ref, g, gt, o_ref, c_ref, n_ref, m_ref)


def _ml_scan(q, k, v, gates, gates_t, b, l):
    t = b * l
    nc = l // ML_CHUNK
    dh = ML_HEAD_DIM
    e = D_INNER
    ng = 4 * ML_HEADS
    fw = lambda bb, c: bb * nc + c
    bw = lambda bb, c: bb * nc + nc - 1 - c

    def specs(ch):
        qspec = pl.BlockSpec((ML_CHUNK, e), lambda bb, c: (ch(bb, c), 0))
        return [qspec, qspec, qspec,
                pl.BlockSpec((ML_CHUNK, ng), lambda bb, c: (ch(bb, c), 0)),
                pl.BlockSpec((ng, ML_CHUNK), lambda bb, c: (0, ch(bb, c)))]

    out = jax.ShapeDtypeStruct((t, e), F32)
    return pl.pallas_call(
        _ml_scan_kernel,
        grid=(b, nc),
        in_specs=specs(fw) + specs(bw),
        out_specs=[pl.BlockSpec((ML_CHUNK, e), lambda bb, c: (fw(bb, c), 0)),
                   pl.BlockSpec((ML_CHUNK, e), lambda bb, c: (bw(bb, c), 0))],
        out_shape=[out, out],
        scratch_shapes=[pltpu.VMEM((2, ML_HEADS, dh, dh), F32), pltpu.VMEM((2, ML_HEADS, 1, dh), F32),
                        pltpu.VMEM((2, ML_HEADS, 1, 1), F32)],
        compiler_params=_cparams(("arbitrary", "arbitrary")),
        name="mlstm_scan",
    )(q, k, v, gates, gates_t, q, k, v, gates, gates_t)


def _mlstm_layer(x, norm_w, w_in, conv_w, conv_b, wq, wk, wv, w_gate, b_gate, head_norm, skip,
                 w_out, final_norm=None):
    b, l, _ = x.shape
    x2d = x.reshape(b * l, D_MODEL)
    proj, xc = _norm_matmul_conv(x2d, norm_w, w_in, conv_w, conv_b, l, act=True, dual=True, tm=1024)
    q, k, v, gates, gates_t = _ml_qkv(xc, proj, wq, wk, wv, w_gate, b_gate)
    h_fwd, h_bwd = _ml_scan(q, k, v, gates, gates_t, b, l)
    out = _out_matmul([h_fwd, h_bwd, xc], proj, 1, w_out, x2d, final_norm,
                      extra=(head_norm, skip))
    return out.reshape(b, l, D_MODEL)


def _trunk(x, layers, final_norm):
    kinds = (_hyena_layer, _swa_layer, _mlstm_layer, _hyena_layer)
    for i, (fn, params) in enumerate(zip(kinds, layers)):
        x = fn(x, *params, final_norm=final_norm if i == len(kinds) - 1 else None)
    return x


def kernel(x_prompt, x_sample, l0_norm, l0_hy_w_in, l0_hy_conv_w, l0_hy_conv_b, l0_hy_filt_w1, l0_hy_filt_b1, l0_hy_filt_w2, l0_hy_filt_b2, l0_hy_filt_w3, l0_hy_filt_b3, l0_hy_filt_freq, l0_hy_filt_wout, l0_hy_bias_d, l0_hy_w_out, l1_norm, l1_swa_w_in, l1_swa_sink, l1_swa_w_out, l2_norm, l2_ml_w_in, l2_ml_conv_w, l2_ml_conv_b, l2_ml_wq, l2_ml_wk, l2_ml_wv, l2_ml_w_gate, l2_ml_b_gate, l2_ml_head_norm, l2_ml_skip, l2_ml_w_out, l3_norm, l3_hy_w_in, l3_hy_conv_w, l3_hy_conv_b, l3_hy_filt_w1, l3_hy_filt_b1, l3_hy_filt_w2, l3_hy_filt_b2, l3_hy_filt_w3, l3_hy_filt_b3, l3_hy_filt_freq, l3_hy_filt_wout, l3_hy_bias_d, l3_hy_w_out, final_norm):
    layers = [
        (l0_norm, l0_hy_w_in, l0_hy_conv_w, l0_hy_conv_b, l0_hy_filt_w1, l0_hy_filt_b1, l0_hy_filt_w2,
         l0_hy_filt_b2, l0_hy_filt_w3, l0_hy_filt_b3, l0_hy_filt_freq, l0_hy_filt_wout, l0_hy_bias_d,
         l0_hy_w_out),
        (l1_norm, l1_swa_w_in, l1_swa_sink, l1_swa_w_out),
        (l2_norm, l2_ml_w_in, l2_ml_conv_w, l2_ml_conv_b, l2_ml_wq, l2_ml_wk, l2_ml_wv, l2_ml_w_gate,
         l2_ml_b_gate, l2_ml_head_norm, l2_ml_skip, l2_ml_w_out),
        (l3_norm, l3_hy_w_in, l3_hy_conv_w, l3_hy_conv_b, l3_hy_filt_w1, l3_hy_filt_b1, l3_hy_filt_w2,
         l3_hy_filt_b2, l3_hy_filt_w3, l3_hy_filt_b3, l3_hy_filt_freq, l3_hy_filt_wout, l3_hy_bias_d,
         l3_hy_w_out),
    ]
    return (_trunk(x_prompt, layers, final_norm), _trunk(x_sample, layers, final_norm))
```

```python
import cmath
import functools
import math

import jax
import jax.numpy as jnp
from jax import lax
from jax.experimental import pallas as pl
from jax.experimental.pallas import tpu as pltpu

F32 = jnp.float32
BF16 = jnp.bfloat16

D_MODEL = 1024
D_INNER = 2048
RMS_EPS = 1e-6
LN_EPS = 1e-6

LANES = 128
SUBLANES = 8
N_LANE_BLOCKS = D_INNER // LANES
VMEM_LIMIT = 56 * 1024 * 1024

HY_EMB = 33
HY_FILT = 64
HY_MIN_DECAY = math.log(1e-2) / 1.5
HY_MAX_DECAY = math.log(1e-2) / 0.3
FFT_N2 = 128
FFT_N2_RADICES = (8, 4, 4)
FFT_A_ROWS = 8192
FFT_MID_COLS = 4

HEAD_DIM = 128
N_Q_HEADS = D_INNER // HEAD_DIM
N_KV_HEADS = 4
GROUP = N_Q_HEADS // N_KV_HEADS
WINDOW = 128
BLOCK = 128
ROT_DIM = HEAD_DIM // 4
ROPE_THETA = 500000.0
Q_DIM = N_Q_HEADS * HEAD_DIM
KV_DIM = N_KV_HEADS * HEAD_DIM

ML_HEADS = 4
ML_HEAD_DIM = D_INNER // ML_HEADS
ML_QKV_BLOCK = 4
ML_CHUNK = 128


def _cparams(sem):
    return pltpu.CompilerParams(dimension_semantics=sem, vmem_limit_bytes=VMEM_LIMIT)


def _silu(x):
    return x * (1.0 / (1.0 + jnp.exp(-x)))


def _split_hi_lo(w):
    hi = w.astype(BF16)
    lo = (w - hi.astype(F32)).astype(BF16)
    return jnp.stack([hi, lo])


def _dot3(x, w_hi, w_lo):
    xh = x.astype(BF16)
    xl = (x - xh.astype(F32)).astype(BF16)
    d = lambda a, b: jnp.dot(a, b, preferred_element_type=F32)
    return d(xh, w_hi) + (d(xl, w_hi) + d(xh, w_lo))


def _add(a, b):
    if a is None:
        return b
    if b is None:
        return a
    return a + b


def _sub(a, b):
    if b is None:
        return a
    if a is None:
        return -b
    return a - b


def _scale(a, c):
    if a is None or c == 0.0:
        return None
    if c == 1.0:
        return a
    if c == -1.0:
        return -a
    return a * c


def _cadd(x, y):
    return (_add(x[0], y[0]), _add(x[1], y[1]))


def _csub(x, y):
    return (_sub(x[0], y[0]), _sub(x[1], y[1]))


def _snap(v):
    for t in (0.0, 1.0, -1.0):
        if abs(v - t) < 1e-14:
            return t
    return v


def _cmul_const(x, c):
    cr, ci = _snap(c.real), _snap(c.imag)
    if cr != 0.0 and abs(abs(cr) - abs(ci)) < 1e-14:
        sr, si = math.copysign(1.0, cr), math.copysign(1.0, ci)
        re = _sub(_scale(x[0], sr), _scale(x[1], si))
        im = _add(_scale(x[0], si), _scale(x[1], sr))
        return (_scale(re, abs(cr)), _scale(im, abs(cr)))
    re = _sub(_scale(x[0], cr), _scale(x[1], ci))
    im = _add(_scale(x[0], ci), _scale(x[1], cr))
    return (re, im)


def _cmul(x, y):
    xr, xi = x
    yr, yi = y
    if xi is None:
        return (xr * yr, xr * yi)
    return (xr * yr - xi * yi, xr * yi + xi * yr)


def _dense(x):
    re, im = x
    if re is None:
        re = jnp.zeros_like(im)
    if im is None:
        im = jnp.zeros_like(re)
    return (re, im)


def _small_dft(xs, sign):
    r = len(xs)
    if r == 1:
        return list(xs)
    ev = _small_dft(xs[0::2], sign)
    od = _small_dft(xs[1::2], sign)
    out = [None] * r
    for k in range(r // 2):
        t = _cmul_const(od[k], cmath.exp(sign * 2j * math.pi * k / r))
        out[k] = _cadd(ev[k], t)
        out[k + r // 2] = _csub(ev[k], t)
    return out


def _fft_plan(n, radices, sign):
    assert math.prod(radices) == n
    passes = []
    s, ncur = 1, n
    for r in radices:
        m = ncur // r
        bfs = []
        for p in range(m):
            tws = [cmath.exp(sign * 2j * math.pi * p * k / ncur) for k in range(r)]
            for q in range(s):
                ins = [q + s * (p + m * j) for j in range(r)]
                outs = [q + s * (r * p + k) for k in range(r)]
                bfs.append((ins, outs, tws))
        passes.append(bfs)
        ncur, s = m, s * r
    return passes


def _run_fft(plan, sign, load, store, temps):
    npass = len(plan)
    for pi, bfs in enumerate(plan):
        if pi == 0:
            ld = load
        else:
            tr, ti = temps[(pi - 1) % 2]
            ld = lambda i, tr=tr, ti=ti: (tr[i], ti[i])
        if pi == npass - 1:
            st = store
        else:
            tr2, ti2 = temps[pi % 2]

            def st(i, x, tr2=tr2, ti2=ti2):
                tr2[i] = x[0]
                ti2[i] = x[1]
        for ins, outs, tws in bfs:
            ys = _small_dft([ld(i) for i in ins], sign)
            for o, y, tw in zip(outs, ys, tws):
                st(o, _dense(_cmul_const(y, tw)))


def _radices(n):
    out = []
    while n > 1:
        r = 8 if n % 8 == 0 else (4 if n % 4 == 0 else 2)
        out.append(r)
        n //= r
    return tuple(out)


def _norm_matmul_kernel(x_ref, nw_ref, w_ref, o_ref, xn_ref):
    @pl.when(pl.program_id(1) == 0)
    def _():
        x = x_ref[...]
        inv = lax.rsqrt(jnp.mean(x * x, axis=-1, keepdims=True) + RMS_EPS)
        xn_ref[...] = ((x * inv) * nw_ref[...]).astype(BF16)

    o_ref[...] = jnp.dot(xn_ref[...], w_ref[...], preferred_element_type=F32)


def _norm_matmul(x2d, norm_w, w, tm=2048, tn=1024):
    t, d = x2d.shape
    n = w.shape[1]
    tm = min(tm, t)
    tn = min(tn, n)
    return pl.pallas_call(
        _norm_matmul_kernel,
        grid=(t // tm, n // tn),
        in_specs=[
            pl.BlockSpec((tm, d), lambda i, j: (i, 0)),
            pl.BlockSpec((1, d), lambda i, j: (0, 0)),
            pl.BlockSpec((d, tn), lambda i, j: (0, j)),
        ],
        out_specs=pl.BlockSpec((tm, tn), lambda i, j: (i, j)),
        out_shape=jax.ShapeDtypeStruct((t, n), F32),
        scratch_shapes=[pltpu.VMEM((tm, d), BF16)],
        compiler_params=_cparams(("arbitrary", "arbitrary")),
        name="norm_matmul",
    )(x2d, norm_w.reshape(1, d), w.astype(BF16))


HALO = 16


def _norm_matmul_conv_kernel(x_ref, xp_ref, xn_ref, nw_ref, w_ref, cw_ref, cb_ref, *rest,
                             tm, nconv, seq_tiles, act, dual):
    if dual:
        o_ref, oc_ref, xs_ref = rest
    else:
        o_ref, xs_ref = rest
    i = pl.program_id(0)
    j = pl.program_id(1)

    @pl.when(j == 0)
    def _():
        def nrm(x):
            inv = lax.rsqrt(jnp.mean(x * x, axis=-1, keepdims=True) + RMS_EPS)
            return (x * inv) * nw_ref[...]

        first = (i % seq_tiles) == 0
        last = (i % seq_tiles) == seq_tiles - 1
        xs_ref[0:HALO] = jnp.where(first, 0.0, nrm(xp_ref[...])).astype(BF16)
        xs_ref[HALO:HALO + tm] = nrm(x_ref[...]).astype(BF16)
        xs_ref[HALO + tm:] = jnp.where(last, 0.0, nrm(xn_ref[...])).astype(BF16)

    @pl.when(j < nconv)
    def _():
        rows = tm + 2 * HALO
        tn = w_ref.shape[1]
        nsplit = 4 if tn % (8 * LANES) == 0 else 1
        for c in range(nsplit):
            cs = slice(c * tn // nsplit, (c + 1) * tn // nsplit)
            acc = jnp.dot(xs_ref[...], w_ref[:, cs], preferred_element_type=F32)
            y = (acc * cw_ref[1:2, cs] + pltpu.roll(acc, 1, 0) * cw_ref[0:1, cs]
                 + pltpu.roll(acc, rows - 1, 0) * cw_ref[2:3, cs] + cb_ref[:, cs])
            y = y[HALO:HALO + tm]
            if act:
                y = _silu(y)
            if dual:
                oc_ref[:, cs] = y
                o_ref[:, cs] = acc[HALO:HALO + tm]
            else:
                o_ref[:, cs] = y

    @pl.when(j >= nconv)
    def _():
        o_ref[...] = jnp.dot(xs_ref[HALO:HALO + tm], w_ref[...], preferred_element_type=F32)


def _norm_matmul_conv(x2d, norm_w, w, conv_w, conv_b, seq_len, act, dual, tm=2048, tn=1024):
    t, d = x2d.shape
    n = w.shape[1]
    nc = conv_w.shape[1]
    tm = min(tm, seq_len)
    tn = min(tn, nc)
    nconv = nc // tn
    hb = tm // HALO
    nhb = t // HALO
    cj = lambda j: jnp.minimum(j, nconv - 1)
    in_specs = [
        pl.BlockSpec((tm, d), lambda i, j: (i, 0)),
        pl.BlockSpec((HALO, d), lambda i, j: (jnp.maximum(i * hb - 1, 0), 0)),
        pl.BlockSpec((HALO, d), lambda i, j: (jnp.minimum((i + 1) * hb, nhb - 1), 0)),
        pl.BlockSpec((1, d), lambda i, j: (0, 0)),
        pl.BlockSpec((d, tn), lambda i, j: (0, j)),
        pl.BlockSpec((3, tn), lambda i, j: (0, cj(j))),
        pl.BlockSpec((1, tn), lambda i, j: (0, cj(j))),
    ]
    out_specs = pl.BlockSpec((tm, tn), lambda i, j: (i, j))
    out_shape = jax.ShapeDtypeStruct((t, n), F32)
    if dual:
        out_specs = [out_specs, pl.BlockSpec((tm, tn), lambda i, j: (i, cj(j)))]
        out_shape = [out_shape, jax.ShapeDtypeStruct((t, nc), F32)]
    return pl.pallas_call(
        functools.partial(_norm_matmul_conv_kernel, tm=tm, nconv=nconv, seq_tiles=seq_len // tm,
                          act=act, dual=dual),
        grid=(t // tm, n // tn),
        in_specs=in_specs,
        out_specs=out_specs,
        out_shape=out_shape,
        scratch_shapes=[pltpu.VMEM((tm + 2 * HALO, d), BF16)],
        compiler_params=_cparams(("arbitrary", "arbitrary")),
        name="norm_matmul_conv",
    )(x2d, x2d, x2d, norm_w.reshape(1, d), w.astype(BF16), conv_w, conv_b.reshape(1, nc))


def _out_kernel(*refs, mode, final):
    if mode == "mlstm":
        hf_ref, hb_ref, xc_ref, g_ref, hn_ref, sk_ref, w_ref, r_ref = refs[:8]
        rest = refs[8:]
        h = hf_ref[...] + hb_ref[...]
        parts = []
        for hd in range(ML_HEADS):
            seg = h[:, hd * ML_HEAD_DIM:(hd + 1) * ML_HEAD_DIM]
            mu = jnp.mean(seg, axis=-1, keepdims=True)
            cen = seg - mu
            var = jnp.mean(cen * cen, axis=-1, keepdims=True)
            parts.append(cen * lax.rsqrt(var + LN_EPS))
        a = jnp.concatenate(parts, axis=-1) * hn_ref[...] + sk_ref[...] * xc_ref[...]
    else:
        a_ref, g_ref, w_ref, r_ref = refs[:4]
        rest = refs[4:]
        a = a_ref[...]
    if final:
        fn_ref, o_ref = rest
    else:
        (o_ref,) = rest
    y = jnp.dot((a * _silu(g_ref[...])).astype(BF16), w_ref[...], preferred_element_type=F32)
    x = r_ref[...] + y
    if final:
        inv = lax.rsqrt(jnp.mean(x * x, axis=-1, keepdims=True) + RMS_EPS)
        x = (x * inv) * fn_ref[...]
    o_ref[...] = x


def _out_matmul(acts, gate_arr, gate_blk, w, resid, final_norm=None, extra=(), tm=512):
    t = resid.shape[0]
    tm = min(tm, t)
    e = D_INNER
    mode = "mlstm" if len(acts) == 3 else "plain"
    in_specs, args = [], []
    for a in acts:
        if isinstance(a, tuple):
            arr, idx = a
            in_specs.append(pl.BlockSpec((None, tm, e), lambda i, idx=idx: (idx, i, 0)))
            args.append(arr)
        else:
            in_specs.append(pl.BlockSpec((tm, e), lambda i: (i, 0)))
            args.append(a)
    in_specs.append(pl.BlockSpec((tm, e), lambda i: (i, gate_blk)))
    args.append(gate_arr)
    for v in extra:
        in_specs.append(pl.BlockSpec((1, e), lambda i: (0, 0)))
        args.append(v.reshape(1, e))
    in_specs.append(pl.BlockSpec((e, D_MODEL), lambda i: (0, 0)))
    args.append(w.astype(BF16))
    in_specs.append(pl.BlockSpec((tm, D_MODEL), lambda i: (i, 0)))
    args.append(resid)
    if final_norm is not None:
        in_specs.append(pl.BlockSpec((1, D_MODEL), lambda i: (0, 0)))
        args.append(final_norm.reshape(1, D_MODEL))
    return pl.pallas_call(
        functools.partial(_out_kernel, mode=mode, final=final_norm is not None),
        grid=(t // tm,),
        in_specs=in_specs,
        out_specs=pl.BlockSpec((tm, D_MODEL), lambda i: (i, 0)),
        out_shape=jax.ShapeDtypeStruct((t, D_MODEL), F32),
        compiler_params=_cparams(("arbitrary",)),
        name="out_matmul_" + mode,
    )(*args)


def _filter_kernel(feat_ref, t_ref, w1_ref, b1_ref, w2_ref, b2_ref, w3_ref, b3_ref, fr_ref,
                   wo_ref, dl_ref, o_ref):
    fr = fr_ref[...]
    h = jnp.sin(fr * (_dot3(feat_ref[...], w1_ref[0], w1_ref[1]) + b1_ref[...]))
    h = jnp.sin(fr * (_dot3(h, w2_ref[0], w2_ref[1]) + b2_ref[...]))
    h = jnp.sin(fr * (_dot3(h, w3_ref[0], w3_ref[1]) + b3_ref[...]))
    tcol = t_ref[...]
    decay = jnp.exp(-tcol[:, 0:1] * dl_ref[...]) * tcol[:, 1:2]
    for f in range(2):
        k = _dot3(h, wo_ref[0, f], wo_ref[1, f])
        o_ref[f] = k * decay


def _hyena_two_sided_filters(l, fw1, fb1, fw2, fb2, fw3, fb3, ffreq, fwout, tr=1024):
    n = 2 * l
    e = D_INNER
    tr = min(tr, l)
    idx = jnp.arange(n)
    src = jnp.where(idx < l, idx, n - idx)
    valid = (idx != l).astype(F32)
    src = jnp.where(idx == l, 0, src)
    t_all = jnp.linspace(0.0, 1.0, l, dtype=F32)
    bands = (HY_EMB - 1) // 2
    t_src = t_all[src]
    ang = 2.0 * math.pi * src.astype(F32)[:, None] / l
    fb = jnp.linspace(1e-4, bands - 1, bands, dtype=F32)[None, :]
    feat = jnp.concatenate([t_src[:, None], jnp.cos(fb * ang), -jnp.sin(fb * ang)], axis=-1)
    feat = jnp.pad(feat, ((0, 0), (0, LANES - HY_EMB)))
    tcol = jnp.stack([t_src, valid], axis=-1)
    w1p = jnp.pad(fw1, ((0, LANES - HY_EMB), (0, 0)))
    deltas = jnp.abs(jnp.linspace(HY_MIN_DECAY, HY_MAX_DECAY, e, dtype=F32)).reshape(1, e)
    wo = _split_hi_lo(fwout.reshape(HY_FILT, 4, e).transpose(1, 0, 2))
    nhalf = l // tr
    row = lambda v: v.reshape(1, HY_FILT)
    full = lambda shp: pl.BlockSpec(shp, lambda i: tuple(0 for _ in shp))
    return pl.pallas_call(
        _filter_kernel,
        grid=(n // tr,),
        in_specs=[
            pl.BlockSpec((tr, LANES), lambda i: (i, 0)),
            pl.BlockSpec((tr, 2), lambda i: (i, 0)),
            full((2, LANES, HY_FILT)), full((1, HY_FILT)),
            full((2, HY_FILT, HY_FILT)), full((1, HY_FILT)),
            full((2, HY_FILT, HY_FILT)), full((1, HY_FILT)),
            full((1, HY_FILT)),
            pl.BlockSpec((2, 2, HY_FILT, e), lambda i: (0, i // nhalf, 0, 0)),
            full((1, e)),
        ],
        out_specs=pl.BlockSpec((2, tr, e), lambda i: (0, i, 0)),
        out_shape=jax.ShapeDtypeStruct((2, n, e), F32),
        compiler_params=_cparams(("arbitrary",)),
        name="hyena_filter",
    )(feat, tcol, _split_hi_lo(w1p), row(fb1), _split_hi_lo(fw2), row(fb2), _split_hi_lo(fw3),
      row(fb3), row(ffreq), wo, deltas)


FFT_SCRATCH = 6
FFT_INTERLEAVE = 2


def _interleaved_loop(n, one, scratch):
    u = FFT_INTERLEAVE if n % FFT_INTERLEAVE == 0 else 1

    def body(c, carry):
        for k in range(u):
            one(c * u + k, scratch[k * FFT_SCRATCH:(k + 1) * FFT_SCRATCH])
        return carry

    lax.fori_loop(0, n // u, body, 0)


def _fft_scratch(n):
    return [pltpu.VMEM((n, SUBLANES, LANES), F32) for _ in range(FFT_SCRATCH * FFT_INTERLEAVE)]


def _fft_a_kernel(x_ref, zr_ref, zi_ref, *scratch, n1, nj, packed):
    plan = _fft_plan(n1, _radices(n1), -1)

    def one(j, scr):
        temps = [(scr[0], scr[1]), (scr[2], scr[3])]
        fr, fi = scr[4], scr[5]
        rows = pl.ds(pl.multiple_of(j * SUBLANES, SUBLANES), SUBLANES)

        def load(i):
            if packed:
                if i >= n1 // 2:
                    return (None, None)
                return (x_ref[0, 0, i, rows, :], x_ref[1, 0, i, rows, :])
            return (x_ref[0, i, rows, :], x_ref[1, i, rows, :])

        def store(i, x):
            fr[i] = x[0]
            fi[i] = x[1]

        _run_fft(plan, -1, load, store, temps)
        if packed:
            planes = ((0, fr, fi),)
        else:
            ar, ai, br, bi = scr[0], scr[1], scr[2], scr[3]
            for k in range(n1):
                m = (n1 - k) % n1
                ar[k] = fr[k] + fr[m]
                ai[k] = fi[k] - fi[m]
                br[k] = fi[k] + fi[m]
                bi[k] = fr[m] - fr[k]
            planes = ((0, ar, ai), (1, br, bi))
        for f, pr, pi in planes:
            for a in range(n1 // SUBLANES):
                sl = slice(a * SUBLANES, (a + 1) * SUBLANES)
                zr_ref[f, a, 0, rows] = jnp.swapaxes(pr[sl], 0, 1)
                zi_ref[f, a, 0, rows] = jnp.swapaxes(pi[sl], 0, 1)

    _interleaved_loop(nj, one, scratch)


def _fft_a(x, n1, n2, packed, lane_blk0=0):
    nl = N_LANE_BLOCKS
    if packed:
        p = x.shape[1]
        bn2 = min(max(FFT_A_ROWS // n1, SUBLANES), n2)
        in_spec = pl.BlockSpec((2, 1, n1 // 2, bn2, LANES),
                               lambda pp, lb, j: (0, pp, 0, j, lane_blk0 + lb))
        nf = 1
    else:
        p = 1
        bn2 = min(max(FFT_A_ROWS // (2 * n1), SUBLANES), n2)
        in_spec = pl.BlockSpec((2, n1, bn2, LANES), lambda pp, lb, j: (0, 0, j, lb))
        nf = 2
    out_spec = pl.BlockSpec((nf, n1 // SUBLANES, 1, bn2, SUBLANES, LANES),
                            lambda pp, lb, j: (pp, 0, lb, j, 0, 0))
    shp = jax.ShapeDtypeStruct((p * nf, n1 // SUBLANES, nl, n2, SUBLANES, LANES), F32)
    return pl.pallas_call(
        functools.partial(_fft_a_kernel, n1=n1, nj=bn2 // SUBLANES, packed=packed),
        grid=(p, nl, n2 // bn2),
        in_specs=[in_spec],
        out_specs=[out_spec, out_spec],
        out_shape=[shp, shp],
        scratch_shapes=_fft_scratch(n1),
        compiler_params=_cparams(("arbitrary", "arbitrary", "arbitrary")),
        name="fft_a_packed" if packed else "fft_a_pair",
    )(x)


def _fft_mid_kernel(*refs, n2, ncol, nl, conv):
    if conv:
        zr_ref, zi_ref, kr_ref, ki_ref, twr_ref, twi_ref, or_ref = refs[:7]
        scratch = refs[7:]
    else:
        zr_ref, zi_ref, twr_ref, twi_ref, or_ref, oi_ref = refs[:6]
        scratch = refs[6:]
    fwd = _fft_plan(n2, FFT_N2_RADICES, -1)
    inv = _fft_plan(n2, FFT_N2_RADICES, +1)

    def one(c, scr):
        temps = [(scr[0], scr[1]), (scr[2], scr[3])]
        sr, si = scr[4], scr[5]
        i = c // nl
        g = c % nl

        def tw(r):
            return (twr_ref[i, r], twi_ref[i, r])

        def load(r):
            return _cmul((zr_ref[0, i, g, r], zi_ref[0, i, g, r]), tw(r))

        if conv:
            def store_spec(r, x):
                y = _cmul(x, (kr_ref[0, i, g, r], ki_ref[0, i, g, r]))
                sr[r] = y[0]
                si[r] = y[1]

            _run_fft(fwd, -1, load, store_spec, temps)

            def store_out(r, x):
                tr, ti = tw(r)
                yr = x[0] * tr + x[1] * ti
                yi = x[1] * tr - x[0] * ti
                or_ref[0, i, g, r] = jnp.concatenate([yr, yi], axis=0).astype(BF16)

            _run_fft(inv, +1, lambda r: (sr[r], si[r]), store_out, temps)
        else:
            def store_spec(r, x):
                or_ref[0, i, g, r] = x[0]
                oi_ref[0, i, g, r] = x[1]

            _run_fft(fwd, -1, load, store_spec, temps)

    _interleaved_loop(ncol, one, scratch)


def _twiddle_table(n1, n2, scale=1.0):
    n = n1 * n2
    k1 = jnp.arange(n1).reshape(n1 // SUBLANES, 1, SUBLANES)
    m = (k1 * jnp.arange(n2).reshape(1, n2, 1)) % n
    ang = m.astype(F32) * (-2.0 * math.pi / n)
    shape = (n1 // SUBLANES, n2, SUBLANES, LANES)
    return (jnp.broadcast_to((scale * jnp.cos(ang))[..., None], shape),
            jnp.broadcast_to((scale * jnp.sin(ang))[..., None], shape))


def _fft_mid(zr, zi, tw, filt=None):
    p, na, nl, n2 = zr.shape[:4]
    conv = filt is not None
    bl = min(FFT_MID_COLS if conv else 2 * FFT_MID_COLS, nl)
    zspec = pl.BlockSpec((1, 1, bl, n2, SUBLANES, LANES), lambda i, g, pp: (pp, i, g, 0, 0, 0))
    twspec = pl.BlockSpec((1, n2, SUBLANES, LANES), lambda i, g, pp: (i, 0, 0, 0))
    in_specs = [zspec, zspec]
    args = [zr, zi]
    if conv:
        fidx = filt[2]
        kspec = pl.BlockSpec((1, 1, bl, n2, SUBLANES, LANES), lambda i, g, pp: (fidx, i, g, 0, 0, 0))
        in_specs += [kspec, kspec]
        args += [filt[0], filt[1]]
    in_specs += [twspec, twspec]
    args += [tw[0], tw[1]]
    if conv:
        out_specs = pl.BlockSpec((1, 1, bl, n2, 2 * SUBLANES, LANES), lambda i, g, pp: (pp, i, g, 0, 0, 0))
        out_shape = jax.ShapeDtypeStruct((p, na, nl, n2, 2 * SUBLANES, LANES), BF16)
    else:
        out_specs = [zspec, zspec]
        out_shape = [jax.ShapeDtypeStruct(zr.shape, F32)] * 2
    return pl.pallas_call(
        functools.partial(_fft_mid_kernel, n2=n2, ncol=bl, nl=bl, conv=conv),
        grid=(na, nl // bl, p),
        in_specs=in_specs,
        out_specs=out_specs,
        out_shape=out_shape,
        scratch_shapes=_fft_scratch(n2),
        compiler_params=_cparams(("arbitrary", "arbitrary", "arbitrary")),
        name="fft_mid_conv" if conv else "fft_mid_spec",
    )(*args)


def _fft_ainv_kernel(z_ref, u_ref, x_ref, d_ref, o_ref, *scratch, n1, nj):
    plan = _fft_plan(n1, _radices(n1), +1)
    d = d_ref[...]

    def one(j, scr):
        temps = [(scr[0], scr[1]), (scr[2], scr[3])]
        lr, li = scr[4], scr[5]
        rows = pl.ds(pl.multiple_of(j * SUBLANES, SUBLANES), SUBLANES)
        for a in range(n1 // SUBLANES):
            sl = slice(a * SUBLANES, (a + 1) * SUBLANES)
            z = z_ref[0, a, 0, rows].astype(F32)
            lr[sl] = jnp.swapaxes(z[:, :SUBLANES], 0, 1)
            li[sl] = jnp.swapaxes(z[:, SUBLANES:], 0, 1)

        def store(i, y):
            if i < n1 // 2:
                o_ref[0, 0, i, rows, :] = x_ref[0, 0, i, rows, :] * (y[0] + u_ref[0, 0, i, rows, :] * d)
                o_ref[1, 0, i, rows, :] = x_ref[1, 0, i, rows, :] * (y[1] + u_ref[1, 0, i, rows, :] * d)

        _run_fft(plan, +1, lambda i: (lr[i], li[i]), store, temps)

    _interleaved_loop(nj, one, scratch)


def _fft_ainv(z, u, u_blk0, x, x_blk0, d, d_row):
    p, na, nl, n2 = z.shape[:4]
    n1 = na * SUBLANES
    bn2 = min(max(FFT_A_ROWS // n1, SUBLANES), n2)
    zspec = pl.BlockSpec((1, na, 1, bn2, 2 * SUBLANES, LANES), lambda pp, lb, j: (pp, 0, lb, j, 0, 0))

    def sspec(blk0):
        return pl.BlockSpec((2, 1, n1 // 2, bn2, LANES), lambda pp, lb, j: (0, pp, 0, j, blk0 + lb))

    return pl.pallas_call(
        functools.partial(_fft_ainv_kernel, n1=n1, nj=bn2 // SUBLANES),
        grid=(p, nl, n2 // bn2),
        in_specs=[zspec, sspec(u_blk0), sspec(x_blk0),
                  pl.BlockSpec((1, LANES), lambda pp, lb, j: (0, lb))],
        out_specs=sspec(0),
        out_shape=jax.ShapeDtypeStruct((2, p, n1 // 2, n2, D_INNER), F32),
        scratch_shapes=_fft_scratch(n1),
        compiler_params=_cparams(("arbitrary", "arbitrary", "arbitrary")),
        name="fft_a_inv",
    )(z, u, x, d[d_row].reshape(1, D_INNER))


def _hyena_layer(x, norm_w, w_in, conv_w, conv_b, fw1, fb1, fw2, fb2, fw3, fb3, ffreq, fwout,
                 bias_d, w_out, final_norm=None):
    b, l, _ = x.shape
    e = D_INNER
    n = 2 * l
    n2 = FFT_N2
    n1 = n // n2
    p = b // 2
    nl = N_LANE_BLOCKS
    x2d = x.reshape(b * l, D_MODEL)
    proj = _norm_matmul_conv(x2d, norm_w, w_in, conv_w, conv_b, l, act=False, dual=False)
    pv = proj.reshape(2, p, n1 // 2, n2, 4 * e)

    tw = _twiddle_table(n1, n2)
    kt = _hyena_two_sided_filters(l, fw1, fb1, fw2, fb2, fw3, fb3, ffreq, fwout)
    kr, ki = _fft_a(kt.reshape(2, n1, n2, e), n1, n2, packed=False)
    kfr, kfi = _fft_mid(kr, ki, _twiddle_table(n1, n2, scale=0.5 / n))

    zr, zi = _fft_a(pv, n1, n2, packed=True)
    zc = _fft_mid(zr, zi, tw, filt=(kfr, kfi, 0))
    z1 = _fft_ainv(zc, pv, 0, pv, nl, bias_d, 0)
    zr, zi = _fft_a(z1, n1, n2, packed=True)
    zc = _fft_mid(zr, zi, tw, filt=(kfr, kfi, 1))
    z2 = _fft_ainv(zc, z1, 0, pv, 2 * nl, bias_d, 1)
    out = _out_matmul([z2.reshape(b * l, e)], proj, 3, w_out, x2d, final_norm)
    return out.reshape(b, l, D_MODEL)


def _swa_kernel(q_ref, kp_ref, kc_ref, kn_ref, vp_ref, vc_ref, vn_ref,
                cq_ref, sq_ref, ckp_ref, skp_ref, ckn_ref, skn_ref, sink_ref, o_ref, *, nb):
    n = pl.program_id(1)
    lane = lax.broadcasted_iota(jnp.int32, (BLOCK, HEAD_DIM), 1)
    half = ROT_DIM // 2

    def rope(x, c, s):
        sw = jnp.where(lane < half, pltpu.roll(x, HEAD_DIM - half, 1), pltpu.roll(x, half, 1))
        return x * c + sw * s

    cq, sq = cq_ref[...], sq_ref[...]
    cks = (ckp_ref[...], cq, ckn_ref[...])
    sks = (skp_ref[...], sq, skn_ref[...])
    qi = lax.broadcasted_iota(jnp.int32, (BLOCK, 3 * BLOCK), 0)
    kj = lax.broadcasted_iota(jnp.int32, (BLOCK, 3 * BLOCK), 1) - BLOCK
    mask = jnp.abs(qi - kj) <= WINDOW
    mask = mask & ((kj >= 0) | (n > 0)) & ((kj < BLOCK) | (n < nb - 1))
    mask4 = jnp.concatenate([mask] * GROUP, axis=0)
    scale = HEAD_DIM ** -0.5
    for h in range(N_KV_HEADS):
        sl = slice(h * HEAD_DIM, (h + 1) * HEAD_DIM)
        kparts = [rope(r[0, :, sl], c, s) for r, c, s in zip((kp_ref, kc_ref, kn_ref), cks, sks)]
        kh = jnp.concatenate(kparts, axis=0).astype(BF16)
        vh = jnp.concatenate([vp_ref[0, :, sl], vc_ref[0, :, sl], vn_ref[0, :, sl]],
                             axis=0).astype(BF16)
        qs = []
        for g in range(GROUP):
            hq = h * GROUP + g
            qs.append(rope(q_ref[0, :, hq * HEAD_DIM:(hq + 1) * HEAD_DIM], cq, sq))
        qh = (jnp.concatenate(qs, axis=0) * scale).astype(BF16)
        s = lax.dot_general(qh, kh, (((1,), (1,)), ((), ())), preferred_element_type=F32)
        s = jnp.where(mask4, s, -jnp.inf)
        sink = jnp.concatenate(
            [jnp.full((BLOCK, 1), 1.0, F32) * sink_ref[h * GROUP + g] for g in range(GROUP)], axis=0)
        m = jnp.maximum(jnp.max(s, axis=-1, keepdims=True), sink)
        pr = jnp.exp(s - m)
        den = jnp.sum(pr, axis=-1, keepdims=True) + jnp.exp(sink - m)
        o = jnp.dot(pr.astype(BF16), vh, preferred_element_type=F32) * (1.0 / den)
        for g in range(GROUP):
            hq = h * GROUP + g
            o_ref[0, :, hq * HEAD_DIM:(hq + 1) * HEAD_DIM] = o[g * BLOCK:(g + 1) * BLOCK]


def _rope_tables(l):
    half = ROT_DIM // 2
    inv = ROPE_THETA ** (-jnp.arange(half, dtype=F32) / half)
    ang = jnp.arange(l, dtype=F32)[:, None] * inv[None, :]
    cos, sin = jnp.cos(ang), jnp.sin(ang)
    ones = jnp.ones((l, HEAD_DIM - ROT_DIM), F32)
    return (jnp.concatenate([cos, cos, ones], axis=-1),
            jnp.concatenate([-sin, sin, 0.0 * ones], axis=-1))


def _swa_layer(x, norm_w, w_in, sink, w_out, final_norm=None):
    b, l, _ = x.shape
    nb = l // BLOCK
    x2d = x.reshape(b * l, D_MODEL)
    w_perm = jnp.concatenate([w_in[:, :Q_DIM], w_in[:, Q_DIM + 2 * KV_DIM:],
                              w_in[:, Q_DIM:Q_DIM + 2 * KV_DIM]], axis=1)
    proj = _norm_matmul(x2d, norm_w, w_perm)
    p3 = proj.reshape(b, l, 2 * Q_DIM + 2 * KV_DIM)
    ct, st = _rope_tables(l)
    kblk = 2 * Q_DIM // KV_DIM
    prv = lambda bb, i: jnp.maximum(i - 1, 0)
    nxt = lambda bb, i: jnp.minimum(i + 1, nb - 1)
    kv = lambda col, f: pl.BlockSpec((1, BLOCK, KV_DIM), lambda bb, i: (bb, f(bb, i), col))
    cur = lambda bb, i: i
    tab = lambda f: pl.BlockSpec((BLOCK, HEAD_DIM), lambda bb, i: (f(bb, i), 0))
    o = pl.pallas_call(
        functools.partial(_swa_kernel, nb=nb),
        grid=(b, nb),
        in_specs=[
            pl.BlockSpec((1, BLOCK, Q_DIM), lambda bb, i: (bb, i, 0)),
            kv(kblk, prv), kv(kblk, cur), kv(kblk, nxt),
            kv(kblk + 1, prv), kv(kblk + 1, cur), kv(kblk + 1, nxt),
            tab(cur), tab(cur), tab(prv), tab(prv), tab(nxt), tab(nxt),
            pl.BlockSpec(memory_space=pltpu.SMEM),
        ],
        out_specs=pl.BlockSpec((1, BLOCK, Q_DIM), lambda bb, i: (bb, i, 0)),
        out_shape=jax.ShapeDtypeStruct((b, l, Q_DIM), F32),
        compiler_params=_cparams(("arbitrary", "arbitrary")),
        name="swa_attention",
    )(p3, p3, p3, p3, p3, p3, p3, ct, st, ct, st, ct, st, sink)
    out = _out_matmul([o.reshape(b * l, D_INNER)], proj, 1, w_out, x2d, final_norm)
    return out.reshape(b, l, D_MODEL)


def _ml_qkv_kernel(xc_ref, xi_ref, wq_ref, wk_ref, wv_ref, wg_ref, bg_ref, q_ref, k_ref, v_ref,
                   g_ref, gt_ref):
    nt = D_INNER // BD_TILE
    ng = 4 * ML_HEADS
    qs, ks, vs = [], [], []
    for j in range(nt):
        sl = slice(j * BD_TILE, (j + 1) * BD_TILE)
        xc = xc_ref[:, sl]
        qs.append(_dot3(xc, wq_ref[0, j], wq_ref[1, j]))
        ks.append(_dot3(xc, wk_ref[0, j], wk_ref[1, j]))
        vs.append(_dot3(xi_ref[:, sl], wv_ref[0, j], wv_ref[1, j]))
    qkv = jnp.concatenate(qs + ks + vs, axis=-1)
    xh = qkv.astype(BF16)
    xl = (qkv - xh.astype(F32)).astype(BF16)
    both = jnp.dot(xh, wg_ref[...], preferred_element_type=F32)
    low = jnp.dot(xl, wg_ref[:, :ng], preferred_element_type=F32)
    gates = both[:, :ng] + (both[:, ng:] + low) + bg_ref[...]
    g_ref[...] = gates
    gt_ref[...] = gates.T
    e = D_INNER
    q_ref[...] = qkv[:, :e].astype(BF16)
    k_ref[...] = (qkv[:, e:2 * e] * (ML_HEAD_DIM ** -0.5)).astype(BF16)
    v_ref[...] = qkv[:, 2 * e:].astype(BF16)


BD_TILE = 256


def _blockdiag_tiles(w):
    per = BD_TILE // ML_QKV_BLOCK
    wt = w.reshape(D_INNER // BD_TILE, per, ML_QKV_BLOCK, ML_QKV_BLOCK)
    eye = jnp.eye(per, dtype=w.dtype)
    return jnp.einsum("tnio,nm->tnimo", wt, eye).reshape(D_INNER // BD_TILE, BD_TILE, BD_TILE)


def _ml_qkv(xc2d, proj, wq, wk, wv, w_gate, b_gate, tm=256):
    t = xc2d.shape[0]
    tm = min(tm, t)
    e = D_INNER
    ng = 4 * ML_HEADS
    nt = e // BD_TILE
    wspec = pl.BlockSpec((2, nt, BD_TILE, BD_TILE), lambda i: (0, 0, 0, 0))
    act = jax.ShapeDtypeStruct((t, e), BF16)
    tiles = lambda w: _split_hi_lo(_blockdiag_tiles(w))
    wg = _split_hi_lo(w_gate)
    wg = jnp.concatenate([wg[0], wg[1]], axis=1)
    return pl.pallas_call(
        _ml_qkv_kernel,
        grid=(t // tm,),
        in_specs=[
            pl.BlockSpec((tm, e), lambda i: (i, 0)),
            pl.BlockSpec((tm, e), lambda i: (i, 0)),
            wspec, wspec, wspec,
            pl.BlockSpec((3 * e, 2 * ng), lambda i: (0, 0)),
            pl.BlockSpec((1, ng), lambda i: (0, 0)),
        ],
        out_specs=[pl.BlockSpec((tm, e), lambda i: (i, 0))] * 3
        + [pl.BlockSpec((tm, ng), lambda i: (i, 0)), pl.BlockSpec((ng, tm), lambda i: (0, i))],
        out_shape=[act, act, act, jax.ShapeDtypeStruct((t, ng), F32),
                   jax.ShapeDtypeStruct((ng, t), F32)],
        compiler_params=_cparams(("arbitrary",)),
        name="mlstm_qkv_gates",
    )(xc2d, proj, tiles(wq), tiles(wk), tiles(wv), wg, b_gate.reshape(1, ng))


def _log_sigmoid(x):
    return jnp.minimum(x, 0.0) - jnp.log(1.0 + jnp.exp(-jnp.abs(x)))


def _ml_chunk(d, hd, q_ref, k_ref, v_ref, g, gt, o_ref, c_ref, n_ref, m_ref):
    lc = ML_CHUNK
    ng = 4 * ML_HEADS
    dh = ML_HEAD_DIM
    hs = slice(hd * dh, (hd + 1) * dh)
    col_i = 2 * ML_HEADS * d + hd
    col_f = col_i + ML_HEADS
    lane = lax.broadcasted_iota(jnp.int32, (lc, ng), 1)
    i_col = jnp.sum(jnp.where(lane == col_i, g, 0.0), axis=-1, keepdims=True)
    f_col = _log_sigmoid(jnp.sum(jnp.where(lane == col_f, g, 0.0), axis=-1, keepdims=True))
    sub = lax.broadcasted_iota(jnp.int32, (ng, lc), 0)
    i_row = jnp.sum(jnp.where(sub == col_i, gt, 0.0), axis=0, keepdims=True)
    f_row = _log_sigmoid(jnp.sum(jnp.where(sub == col_f, gt, 0.0), axis=0, keepdims=True))

    t_idx = lax.broadcasted_iota(jnp.int32, (lc, lc), 0)
    s_idx = lax.broadcasted_iota(jnp.int32, (lc, lc), 1)
    if d == 0:
        causal = s_idx <= t_idx
        causal_t = t_idx <= s_idx
    else:
        causal = s_idx >= t_idx
        causal_t = t_idx >= s_idx
    b_col = jnp.sum(jnp.where(causal, f_row, 0.0), axis=-1, keepdims=True)
    b_row = jnp.sum(jnp.where(causal_t, f_col, 0.0), axis=0, keepdims=True)
    m_prev = m_ref[d, hd]
    dmat = jnp.where(causal, b_col - b_row + i_row, -jnp.inf)
    inter = b_col + m_prev
    m_out = jnp.maximum(inter, jnp.max(dmat, axis=-1, keepdims=True))
    w_intra = jnp.exp(dmat - m_out)
    w_state = jnp.exp(inter - m_out)

    q = q_ref[:, hs]
    k = k_ref[:, hs]
    v = v_ref[:, hs]
    cst = c_ref[d, hd]
    s = lax.dot_general(q, k, (((1,), (1,)), ((), ())), preferred_element_type=F32) * w_intra
    num = jnp.dot(s.astype(BF16), v, preferred_element_type=F32)
    num = num + w_state * jnp.dot(q, cst.astype(BF16), preferred_element_type=F32)
    qn = jnp.sum(q.astype(F32) * n_ref[d, hd], axis=-1, keepdims=True)
    den = jnp.sum(s, axis=-1, keepdims=True) + w_state * qn
    o_ref[:, hs] = num / jnp.maximum(jnp.abs(den), jnp.exp(-m_out))

    last = lc - 1 if d == 0 else 0
    bl = jnp.sum(f_row, axis=-1, keepdims=True)
    m_new = m_out[last:last + 1]
    wk = jnp.exp(bl - b_col + i_col - m_new)
    decay = jnp.exp(bl + m_prev - m_new)
    vw = (v.astype(F32) * wk).astype(BF16)
    c_ref[d, hd] = decay * cst + lax.dot_general(
        k, vw, (((0,), (0,)), ((), ())), preferred_element_type=F32)
    n_ref[d, hd] = decay * n_ref[d, hd] + jnp.sum(k.astype(F32) * wk, axis=0, keepdims=True)
    m_ref[d, hd] = m_new


def _ml_scan_kernel(qf_ref, kf_ref, vf_ref, gf_ref, gtf_ref, qb_ref, kb_ref, vb_ref, gb_ref, gtb_ref,
                    of_ref, ob_ref, c_ref, n_ref, m_ref):
    @pl.when(pl.program_id(1) == 0)
    def _():
        c_ref[...] = jnp.zeros_like(c_ref)
        n_ref[...] = jnp.zeros_like(n_ref)
        m_ref[...] = jnp.zeros_like(m_ref)

    dirs = ((qf_ref, kf_ref, vf_ref, gf_ref, gtf_ref, of_ref),
            (qb_ref, kb_ref, vb_ref, gb_ref, gtb_ref, ob_ref))
    for d, (q_ref, k_ref, v_ref, g_ref, gt_ref, o_ref) in enumerate(dirs):
        g = g_ref[...]
        gt = gt_ref[...]
        for hd in range(ML_HEADS):
            _ml_chunk(d, hd, q_ref, k_ref, v_ref, g, gt, o_ref, c_ref, n_ref, m_ref)


def _ml_scan(q, k, v, gates, gates_t, b, l):
    t = b * l
    nc = l // ML_CHUNK
    dh = ML_HEAD_DIM
    e = D_INNER
    ng = 4 * ML_HEADS
    fw = lambda bb, c: bb * nc + c
    bw = lambda bb, c: bb * nc + nc - 1 - c

    def specs(ch):
        qspec = pl.BlockSpec((ML_CHUNK, e), lambda bb, c: (ch(bb, c), 0))
        return [qspec, qspec, qspec,
                pl.BlockSpec((ML_CHUNK, ng), lambda bb, c: (ch(bb, c), 0)),
                pl.BlockSpec((ng, ML_CHUNK), lambda bb, c: (0, ch(bb, c)))]

    out = jax.ShapeDtypeStruct((t, e), F32)
    return pl.pallas_call(
        _ml_scan_kernel,
        grid=(b, nc),
        in_specs=specs(fw) + specs(bw),
        out_specs=[pl.BlockSpec((ML_CHUNK, e), lambda bb, c: (fw(bb, c), 0)),
                   pl.BlockSpec((ML_CHUNK, e), lambda bb, c: (bw(bb, c), 0))],
        out_shape=[out, out],
        scratch_shapes=[pltpu.VMEM((2, ML_HEADS, dh, dh), F32), pltpu.VMEM((2, ML_HEADS, 1, dh), F32),
                        pltpu.VMEM((2, ML_HEADS, 1, 1), F32)],
        compiler_params=_cparams(("arbitrary", "arbitrary")),
        name="mlstm_scan",
    )(q, k, v, gates, gates_t, q, k, v, gates, gates_t)


def _mlstm_layer(x, norm_w, w_in, conv_w, conv_b, wq, wk, wv, w_gate, b_gate, head_norm, skip,
                 w_out, final_norm=None):
    b, l, _ = x.shape
    x2d = x.reshape(b * l, D_MODEL)
    proj, xc = _norm_matmul_conv(x2d, norm_w, w_in, conv_w, conv_b, l, act=True, dual=True, tm=1024)
    q, k, v, gates, gates_t = _ml_qkv(xc, proj, wq, wk, wv, w_gate, b_gate)
    h_fwd, h_bwd = _ml_scan(q, k, v, gates, gates_t, b, l)
    out = _out_matmul([h_fwd, h_bwd, xc], proj, 1, w_out, x2d, final_norm,
                      extra=(head_norm, skip))
    return out.reshape(b, l, D_MODEL)


def _trunk(x, layers, final_norm):
    kinds = (_hyena_layer, _swa_layer, _mlstm_layer, _hyena_layer)
    for i, (fn, params) in enumerate(zip(kinds, layers)):
        x = fn(x, *params, final_norm=final_norm if i == len(kinds) - 1 else None)
    return x


def kernel(x_prompt, x_sample, l0_norm, l0_hy_w_in, l0_hy_conv_w, l0_hy_conv_b, l0_hy_filt_w1, l0_hy_filt_b1, l0_hy_filt_w2, l0_hy_filt_b2, l0_hy_filt_w3, l0_hy_filt_b3, l0_hy_filt_freq, l0_hy_filt_wout, l0_hy_bias_d, l0_hy_w_out, l1_norm, l1_swa_w_in, l1_swa_sink, l1_swa_w_out, l2_norm, l2_ml_w_in, l2_ml_conv_w, l2_ml_conv_b, l2_ml_wq, l2_ml_wk, l2_ml_wv, l2_ml_w_gate, l2_ml_b_gate, l2_ml_head_norm, l2_ml_skip, l2_ml_w_out, l3_norm, l3_hy_w_in, l3_hy_conv_w, l3_hy_conv_b, l3_hy_filt_w1, l3_hy_filt_b1, l3_hy_filt_w2, l3_hy_filt_b2, l3_hy_filt_w3, l3_hy_filt_b3, l3_hy_filt_freq, l3_hy_filt_wout, l3_hy_bias_d, l3_hy_w_out, final_norm):
    layers = [
        (l0_norm, l0_hy_w_in, l0_hy_conv_w, l0_hy_conv_b, l0_hy_filt_w1, l0_hy_filt_b1, l0_hy_filt_w2,
         l0_hy_filt_b2, l0_hy_filt_w3, l0_hy_filt_b3, l0_hy_filt_freq, l0_hy_filt_wout, l0_hy_bias_d,
         l0_hy_w_out),
        (l1_norm, l1_swa_w_in, l1_swa_sink, l1_swa_w_out),
        (l2_norm, l2_ml_w_in, l2_ml_conv_w, l2_ml_conv_b, l2_ml_wq, l2_ml_wk, l2_ml_wv, l2_ml_w_gate,
         l2_ml_b_gate, l2_ml_head_norm, l2_ml_skip, l2_ml_w_out),
        (l3_norm, l3_hy_w_in, l3_hy_conv_w, l3_hy_conv_b, l3_hy_filt_w1, l3_hy_filt_b1, l3_hy_filt_w2,
         l3_hy_filt_b2, l3_hy_filt_w3, l3_hy_filt_b3, l3_hy_filt_freq, l3_hy_filt_wout, l3_hy_bias_d,
         l3_hy_w_out),
    ]
    return (_trunk(x_prompt, layers, final_norm), _trunk(x_sample, layers, final_norm))
```

```python
import cmath
import functools
import math

import jax
import jax.numpy as jnp
from jax import lax
from jax.experimental import pallas as pl
from jax.experimental.pallas import tpu as pltpu

F32 = jnp.float32
BF16 = jnp.bfloat16

D_MODEL = 1024
D_INNER = 2048
RMS_EPS = 1e-6
LN_EPS = 1e-6

LANES = 128
SUBLANES = 8
N_LANE_BLOCKS = D_INNER // LANES
VMEM_LIMIT = 56 * 1024 * 1024

HY_EMB = 33
HY_FILT = 64
HY_MIN_DECAY = math.log(1e-2) / 1.5
HY_MAX_DECAY = math.log(1e-2) / 0.3
FFT_N2 = 128
FFT_N2_RADICES = (8, 4, 4)
FFT_A_ROWS = 8192
FFT_MID_COLS = 4

HEAD_DIM = 128
N_Q_HEADS = D_INNER // HEAD_DIM
N_KV_HEADS = 4
GROUP = N_Q_HEADS // N_KV_HEADS
WINDOW = 128
BLOCK = 128
ROT_DIM = HEAD_DIM // 4
ROPE_THETA = 500000.0
Q_DIM = N_Q_HEADS * HEAD_DIM
KV_DIM = N_KV_HEADS * HEAD_DIM

ML_HEADS = 4
ML_HEAD_DIM = D_INNER // ML_HEADS
ML_QKV_BLOCK = 4
ML_CHUNK = 128


def _cparams(sem):
    return pltpu.CompilerParams(dimension_semantics=sem, vmem_limit_bytes=VMEM_LIMIT)


def _silu(x):
    return x * (1.0 / (1.0 + jnp.exp(-x)))


def _split_hi_lo(w):
    hi = w.astype(BF16)
    lo = (w - hi.astype(F32)).astype(BF16)
    return jnp.stack([hi, lo])


def _dot3(x, w_hi, w_lo):
    xh = x.astype(BF16)
    xl = (x - xh.astype(F32)).astype(BF16)
    d = lambda a, b: jnp.dot(a, b, preferred_element_type=F32)
    return d(xh, w_hi) + (d(xl, w_hi) + d(xh, w_lo))


def _add(a, b):
    if a is None:
        return b
    if b is None:
        return a
    return a + b


def _sub(a, b):
    if b is None:
        return a
    if a is None:
        return -b
    return a - b


def _scale(a, c):
    if a is None or c == 0.0:
        return None
    if c == 1.0:
        return a
    if c == -1.0:
        return -a
    return a * c


def _cadd(x, y):
    return (_add(x[0], y[0]), _add(x[1], y[1]))


def _csub(x, y):
    return (_sub(x[0], y[0]), _sub(x[1], y[1]))


def _snap(v):
    for t in (0.0, 1.0, -1.0):
        if abs(v - t) < 1e-14:
            return t
    return v


def _cmul_const(x, c):
    cr, ci = _snap(c.real), _snap(c.imag)
    if cr != 0.0 and abs(abs(cr) - abs(ci)) < 1e-14:
        sr, si = math.copysign(1.0, cr), math.copysign(1.0, ci)
        re = _sub(_scale(x[0], sr), _scale(x[1], si))
        im = _add(_scale(x[0], si), _scale(x[1], sr))
        return (_scale(re, abs(cr)), _scale(im, abs(cr)))
    re = _sub(_scale(x[0], cr), _scale(x[1], ci))
    im = _add(_scale(x[0], ci), _scale(x[1], cr))
    return (re, im)


def _cmul(x, y):
    xr, xi = x
    yr, yi = y
    if xi is None:
        return (xr * yr, xr * yi)
    return (xr * yr - xi * yi, xr * yi + xi * yr)


def _dense(x):
    re, im = x
    if re is None:
        re = jnp.zeros_like(im)
    if im is None:
        im = jnp.zeros_like(re)
    return (re, im)


def _small_dft(xs, sign):
    r = len(xs)
    if r == 1:
        return list(xs)
    ev = _small_dft(xs[0::2], sign)
    od = _small_dft(xs[1::2], sign)
    out = [None] * r
    for k in range(r // 2):
        t = _cmul_const(od[k], cmath.exp(sign * 2j * math.pi * k / r))
        out[k] = _cadd(ev[k], t)
        out[k + r // 2] = _csub(ev[k], t)
    return out


def _fft_plan(n, radices, sign):
    assert math.prod(radices) == n
    passes = []
    s, ncur = 1, n
    for r in radices:
        m = ncur // r
        bfs = []
        for p in range(m):
            tws = [cmath.exp(sign * 2j * math.pi * p * k / ncur) for k in range(r)]
            for q in range(s):
                ins = [q + s * (p + m * j) for j in range(r)]
                outs = [q + s * (r * p + k) for k in range(r)]
                bfs.append((ins, outs, tws))
        passes.append(bfs)
        ncur, s = m, s * r
    return passes


def _run_fft(plan, sign, load, store, temps):
    npass = len(plan)
    for pi, bfs in enumerate(plan):
        if pi == 0:
            ld = load
        else:
            tr, ti = temps[(pi - 1) % 2]
            ld = lambda i, tr=tr, ti=ti: (tr[i], ti[i])
        if pi == npass - 1:
            st = store
        else:
            tr2, ti2 = temps[pi % 2]

            def st(i, x, tr2=tr2, ti2=ti2):
                tr2[i] = x[0]
                ti2[i] = x[1]
        for ins, outs, tws in bfs:
            ys = _small_dft([ld(i) for i in ins], sign)
            for o, y, tw in zip(outs, ys, tws):
                st(o, _dense(_cmul_const(y, tw)))


def _radices(n):
    out = []
    while n > 1:
        r = 8 if n % 8 == 0 else (4 if n % 4 == 0 else 2)
        out.append(r)
        n //= r
    return tuple(out)


def _norm_matmul_kernel(x_ref, nw_ref, w_ref, o_ref, xn_ref):
    @pl.when(pl.program_id(1) == 0)
    def _():
        x = x_ref[...]
        inv = lax.rsqrt(jnp.mean(x * x, axis=-1, keepdims=True) + RMS_EPS)
        xn_ref[...] = ((x * inv) * nw_ref[...]).astype(BF16)

    o_ref[...] = jnp.dot(xn_ref[...], w_ref[...], preferred_element_type=F32)


def _norm_matmul(x2d, norm_w, w, tm=2048, tn=1024):
    t, d = x2d.shape
    n = w.shape[1]
    tm = min(tm, t)
    tn = min(tn, n)
    return pl.pallas_call(
        _norm_matmul_kernel,
        grid=(t // tm, n // tn),
        in_specs=[
            pl.BlockSpec((tm, d), lambda i, j: (i, 0)),
            pl.BlockSpec((1, d), lambda i, j: (0, 0)),
            pl.BlockSpec((d, tn), lambda i, j: (0, j)),
        ],
        out_specs=pl.BlockSpec((tm, tn), lambda i, j: (i, j)),
        out_shape=jax.ShapeDtypeStruct((t, n), F32),
        scratch_shapes=[pltpu.VMEM((tm, d), BF16)],
        compiler_params=_cparams(("arbitrary", "arbitrary")),
        name="norm_matmul",
    )(x2d, norm_w.reshape(1, d), w.astype(BF16))


HALO = 16


def _norm_matmul_conv_kernel(x_ref, xp_ref, xn_ref, nw_ref, w_ref, cw_ref, cb_ref, *rest,
                             tm, nconv, seq_tiles, act, dual):
    if dual:
        o_ref, oc_ref, xs_ref = rest
    else:
        o_ref, xs_ref = rest
    i = pl.program_id(0)
    j = pl.program_id(1)

    @pl.when(j == 0)
    def _():
        def nrm(x):
            inv = lax.rsqrt(jnp.mean(x * x, axis=-1, keepdims=True) + RMS_EPS)
            return (x * inv) * nw_ref[...]

        first = (i % seq_tiles) == 0
        last = (i % seq_tiles) == seq_tiles - 1
        xs_ref[0:HALO] = jnp.where(first, 0.0, nrm(xp_ref[...])).astype(BF16)
        xs_ref[HALO:HALO + tm] = nrm(x_ref[...]).astype(BF16)
        xs_ref[HALO + tm:] = jnp.where(last, 0.0, nrm(xn_ref[...])).astype(BF16)

    @pl.when(j < nconv)
    def _():
        rows = tm + 2 * HALO
        tn = w_ref.shape[1]
        nsplit = 4 if tn % (8 * LANES) == 0 else 1
        for c in range(nsplit):
            cs = slice(c * tn // nsplit, (c + 1) * tn // nsplit)
            acc = jnp.dot(xs_ref[...], w_ref[:, cs], preferred_element_type=F32)
            y = (acc * cw_ref[1:2, cs] + pltpu.roll(acc, 1, 0) * cw_ref[0:1, cs]
                 + pltpu.roll(acc, rows - 1, 0) * cw_ref[2:3, cs] + cb_ref[:, cs])
            y = y[HALO:HALO + tm]
            if act:
                y = _silu(y)
            if dual:
                oc_ref[:, cs] = y
                o_ref[:, cs] = acc[HALO:HALO + tm]
            else:
                o_ref[:, cs] = y

    @pl.when(j >= nconv)
    def _():
        o_ref[...] = jnp.dot(xs_ref[HALO:HALO + tm], w_ref[...], preferred_element_type=F32)


def _norm_matmul_conv(x2d, norm_w, w, conv_w, conv_b, seq_len, act, dual, tm=2048, tn=1024):
    t, d = x2d.shape
    n = w.shape[1]
    nc = conv_w.shape[1]
    tm = min(tm, seq_len)
    tn = min(tn, nc)
    nconv = nc // tn
    hb = tm // HALO
    nhb = t // HALO
    cj = lambda j: jnp.minimum(j, nconv - 1)
    in_specs = [
        pl.BlockSpec((tm, d), lambda i, j: (i, 0)),
        pl.BlockSpec((HALO, d), lambda i, j: (jnp.maximum(i * hb - 1, 0), 0)),
        pl.BlockSpec((HALO, d), lambda i, j: (jnp.minimum((i + 1) * hb, nhb - 1), 0)),
        pl.BlockSpec((1, d), lambda i, j: (0, 0)),
        pl.BlockSpec((d, tn), lambda i, j: (0, j)),
        pl.BlockSpec((3, tn), lambda i, j: (0, cj(j))),
        pl.BlockSpec((1, tn), lambda i, j: (0, cj(j))),
    ]
    out_specs = pl.BlockSpec((tm, tn), lambda i, j: (i, j))
    out_shape = jax.ShapeDtypeStruct((t, n), F32)
    if dual:
        out_specs = [out_specs, pl.BlockSpec((tm, tn), lambda i, j: (i, cj(j)))]
        out_shape = [out_shape, jax.ShapeDtypeStruct((t, nc), F32)]
    return pl.pallas_call(
        functools.partial(_norm_matmul_conv_kernel, tm=tm, nconv=nconv, seq_tiles=seq_len // tm,
                          act=act, dual=dual),
        grid=(t // tm, n // tn),
        in_specs=in_specs,
        out_specs=out_specs,
        out_shape=out_shape,
        scratch_shapes=[pltpu.VMEM((tm + 2 * HALO, d), BF16)],
        compiler_params=_cparams(("arbitrary", "arbitrary")),
        name="norm_matmul_conv",
    )(x2d, x2d, x2d, norm_w.reshape(1, d), w.astype(BF16), conv_w, conv_b.reshape(1, nc))


def _out_kernel(*refs, mode, final):
    if mode == "mlstm":
        hf_ref, hb_ref, xc_ref, g_ref, hn_ref, sk_ref, w_ref, r_ref = refs[:8]
        rest = refs[8:]
        h = hf_ref[...] + hb_ref[...]
        parts = []
        for hd in range(ML_HEADS):
            seg = h[:, hd * ML_HEAD_DIM:(hd + 1) * ML_HEAD_DIM]
            mu = jnp.mean(seg, axis=-1, keepdims=True)
            cen = seg - mu
            var = jnp.mean(cen * cen, axis=-1, keepdims=True)
            parts.append(cen * lax.rsqrt(var + LN_EPS))
        a = jnp.concatenate(parts, axis=-1) * hn_ref[...] + sk_ref[...] * xc_ref[...]
    else:
        a_ref, g_ref, w_ref, r_ref = refs[:4]
        rest = refs[4:]
        a = a_ref[...]
    if final:
        fn_ref, o_ref = rest
    else:
        (o_ref,) = rest
    y = jnp.dot((a * _silu(g_ref[...])).astype(BF16), w_ref[...], preferred_element_type=F32)
    x = r_ref[...] + y
    if final:
        inv = lax.rsqrt(jnp.mean(x * x, axis=-1, keepdims=True) + RMS_EPS)
        x = (x * inv) * fn_ref[...]
    o_ref[...] = x


def _out_matmul(acts, gate_arr, gate_blk, w, resid, final_norm=None, extra=(), tm=512):
    t = resid.shape[0]
    tm = min(tm, t)
    e = D_INNER
    mode = "mlstm" if len(acts) == 3 else "plain"
    in_specs, args = [], []
    for a in acts:
        if isinstance(a, tuple):
            arr, idx = a
            in_specs.append(pl.BlockSpec((None, tm, e), lambda i, idx=idx: (idx, i, 0)))
            args.append(arr)
        else:
            in_specs.append(pl.BlockSpec((tm, e), lambda i: (i, 0)))
            args.append(a)
    in_specs.append(pl.BlockSpec((tm, e), lambda i: (i, gate_blk)))
    args.append(gate_arr)
    for v in extra:
        in_specs.append(pl.BlockSpec((1, e), lambda i: (0, 0)))
        args.append(v.reshape(1, e))
    in_specs.append(pl.BlockSpec((e, D_MODEL), lambda i: (0, 0)))
    args.append(w.astype(BF16))
    in_specs.append(pl.BlockSpec((tm, D_MODEL), lambda i: (i, 0)))
    args.append(resid)
    if final_norm is not None:
        in_specs.append(pl.BlockSpec((1, D_MODEL), lambda i: (0, 0)))
        args.append(final_norm.reshape(1, D_MODEL))
    return pl.pallas_call(
        functools.partial(_out_kernel, mode=mode, final=final_norm is not None),
        grid=(t // tm,),
        in_specs=in_specs,
        out_specs=pl.BlockSpec((tm, D_MODEL), lambda i: (i, 0)),
        out_shape=jax.ShapeDtypeStruct((t, D_MODEL), F32),
        compiler_params=_cparams(("arbitrary",)),
        name="out_matmul_" + mode,
    )(*args)


def _filter_kernel(feat_ref, t_ref, w1_ref, b1_ref, w2_ref, b2_ref, w3_ref, b3_ref, fr_ref,
                   wo_ref, dl_ref, o_ref):
    fr = fr_ref[...]
    h = jnp.sin(fr * (_dot3(feat_ref[...], w1_ref[0], w1_ref[1]) + b1_ref[...]))
    h = jnp.sin(fr * (_dot3(h, w2_ref[0], w2_ref[1]) + b2_ref[...]))
    h = jnp.sin(fr * (_dot3(h, w3_ref[0], w3_ref[1]) + b3_ref[...]))
    tcol = t_ref[...]
    decay = jnp.exp(-tcol[:, 0:1] * dl_ref[...]) * tcol[:, 1:2]
    for f in range(2):
        k = _dot3(h, wo_ref[0, f], wo_ref[1, f])
        o_ref[f] = k * decay


def _hyena_two_sided_filters(l, fw1, fb1, fw2, fb2, fw3, fb3, ffreq, fwout, tr=1024):
    n = 2 * l
    e = D_INNER
    tr = min(tr, l)
    idx = jnp.arange(n)
    src = jnp.where(idx < l, idx, n - idx)
    valid = (idx != l).astype(F32)
    src = jnp.where(idx == l, 0, src)
    t_all = jnp.linspace(0.0, 1.0, l, dtype=F32)
    bands = (HY_EMB - 1) // 2
    t_src = t_all[src]
    ang = 2.0 * math.pi * src.astype(F32)[:, None] / l
    fb = jnp.linspace(1e-4, bands - 1, bands, dtype=F32)[None, :]
    feat = jnp.concatenate([t_src[:, None], jnp.cos(fb * ang), -jnp.sin(fb * ang)], axis=-1)
    feat = jnp.pad(feat, ((0, 0), (0, LANES - HY_EMB)))
    tcol = jnp.stack([t_src, valid], axis=-1)
    w1p = jnp.pad(fw1, ((0, LANES - HY_EMB), (0, 0)))
    deltas = jnp.abs(jnp.linspace(HY_MIN_DECAY, HY_MAX_DECAY, e, dtype=F32)).reshape(1, e)
    wo = _split_hi_lo(fwout.reshape(HY_FILT, 4, e).transpose(1, 0, 2))
    nhalf = l // tr
    row = lambda v: v.reshape(1, HY_FILT)
    full = lambda shp: pl.BlockSpec(shp, lambda i: tuple(0 for _ in shp))
    return pl.pallas_call(
        _filter_kernel,
        grid=(n // tr,),
        in_specs=[
            pl.BlockSpec((tr, LANES), lambda i: (i, 0)),
            pl.BlockSpec((tr, 2), lambda i: (i, 0)),
            full((2, LANES, HY_FILT)), full((1, HY_FILT)),
            full((2, HY_FILT, HY_FILT)), full((1, HY_FILT)),
            full((2, HY_FILT, HY_FILT)), full((1, HY_FILT)),
            full((1, HY_FILT)),
            pl.BlockSpec((2, 2, HY_FILT, e), lambda i: (0, i // nhalf, 0, 0)),
            full((1, e)),
        ],
        out_specs=pl.BlockSpec((2, tr, e), lambda i: (0, i, 0)),
        out_shape=jax.ShapeDtypeStruct((2, n, e), F32),
        compiler_params=_cparams(("arbitrary",)),
        name="hyena_filter",
    )(feat, tcol, _split_hi_lo(w1p), row(fb1), _split_hi_lo(fw2), row(fb2), _split_hi_lo(fw3),
      row(fb3), row(ffreq), wo, deltas)


FFT_SCRATCH = 6
FFT_INTERLEAVE = 2


def _interleaved_loop(n, one, scratch):
    u = len(scratch) // FFT_SCRATCH
    while n % u:
        u //= 2

    def body(c, carry):
        for k in range(u):
            one(c * u + k, scratch[k * FFT_SCRATCH:(k + 1) * FFT_SCRATCH])
        return carry

    lax.fori_loop(0, n // u, body, 0)


def _fft_scratch(n, interleave=FFT_INTERLEAVE):
    return [pltpu.VMEM((n, SUBLANES, LANES), F32) for _ in range(FFT_SCRATCH * interleave)]


def _fft_a_kernel(x_ref, zr_ref, zi_ref, *scratch, n1, nj, packed):
    plan = _fft_plan(n1, _radices(n1), -1)

    def one(j, scr):
        temps = [(scr[0], scr[1]), (scr[2], scr[3])]
        fr, fi = scr[4], scr[5]
        rows = pl.ds(pl.multiple_of(j * SUBLANES, SUBLANES), SUBLANES)

        def load(i):
            if packed:
                if i >= n1 // 2:
                    return (None, None)
                return (x_ref[0, 0, i, rows, :], x_ref[1, 0, i, rows, :])
            return (x_ref[0, i, rows, :], x_ref[1, i, rows, :])

        def store(i, x):
            fr[i] = x[0]
            fi[i] = x[1]

        _run_fft(plan, -1, load, store, temps)
        if packed:
            planes = ((0, fr, fi),)
        else:
            ar, ai, br, bi = scr[0], scr[1], scr[2], scr[3]
            for k in range(n1):
                m = (n1 - k) % n1
                ar[k] = fr[k] + fr[m]
                ai[k] = fi[k] - fi[m]
                br[k] = fi[k] + fi[m]
                bi[k] = fr[m] - fr[k]
            planes = ((0, ar, ai), (1, br, bi))
        for f, pr, pi in planes:
            for a in range(n1 // SUBLANES):
                sl = slice(a * SUBLANES, (a + 1) * SUBLANES)
                zr_ref[f, a, 0, rows] = jnp.swapaxes(pr[sl], 0, 1)
                zi_ref[f, a, 0, rows] = jnp.swapaxes(pi[sl], 0, 1)

    _interleaved_loop(nj, one, scratch)


def _fft_a(x, n1, n2, packed, lane_blk0=0):
    nl = N_LANE_BLOCKS
    if packed:
        p = x.shape[1]
        bn2 = min(max(FFT_A_ROWS // n1, SUBLANES), n2)
        in_spec = pl.BlockSpec((2, 1, n1 // 2, bn2, LANES),
                               lambda pp, lb, j: (0, pp, 0, j, lane_blk0 + lb))
        nf = 1
    else:
        p = 1
        bn2 = min(max(FFT_A_ROWS // (2 * n1), SUBLANES), n2)
        in_spec = pl.BlockSpec((2, n1, bn2, LANES), lambda pp, lb, j: (0, 0, j, lb))
        nf = 2
    out_spec = pl.BlockSpec((nf, n1 // SUBLANES, 1, bn2, SUBLANES, LANES),
                            lambda pp, lb, j: (pp, 0, lb, j, 0, 0))
    shp = jax.ShapeDtypeStruct((p * nf, n1 // SUBLANES, nl, n2, SUBLANES, LANES), F32)
    return pl.pallas_call(
        functools.partial(_fft_a_kernel, n1=n1, nj=bn2 // SUBLANES, packed=packed),
        grid=(p, nl, n2 // bn2),
        in_specs=[in_spec],
        out_specs=[out_spec, out_spec],
        out_shape=[shp, shp],
        scratch_shapes=_fft_scratch(n1),
        compiler_params=_cparams(("arbitrary", "arbitrary", "arbitrary")),
        name="fft_a_packed" if packed else "fft_a_pair",
    )(x)


def _fft_mid_kernel(*refs, n2, ncol, nl, conv):
    if conv:
        zr_ref, zi_ref, kr_ref, ki_ref, twr_ref, twi_ref, or_ref = refs[:7]
        scratch = refs[7:]
    else:
        zr_ref, zi_ref, twr_ref, twi_ref, or_ref, oi_ref = refs[:6]
        scratch = refs[6:]
    fwd = _fft_plan(n2, FFT_N2_RADICES, -1)
    inv = _fft_plan(n2, FFT_N2_RADICES, +1)

    def one(c, scr):
        temps = [(scr[0], scr[1]), (scr[2], scr[3])]
        sr, si = scr[4], scr[5]
        i = c // nl
        g = c % nl

        def tw(r):
            return (twr_ref[i, r], twi_ref[i, r])

        def load(r):
            return _cmul((zr_ref[0, i, g, r], zi_ref[0, i, g, r]), tw(r))

        if conv:
            def store_spec(r, x):
                y = _cmul(x, (kr_ref[0, i, g, r], ki_ref[0, i, g, r]))
                sr[r] = y[0]
                si[r] = y[1]

            _run_fft(fwd, -1, load, store_spec, temps)

            def store_out(r, x):
                tr, ti = tw(r)
                yr = x[0] * tr + x[1] * ti
                yi = x[1] * tr - x[0] * ti
                or_ref[0, i, g, r] = jnp.concatenate([yr, yi], axis=0).astype(BF16)

            _run_fft(inv, +1, lambda r: (sr[r], si[r]), store_out, temps)
        else:
            def store_spec(r, x):
                or_ref[0, i, g, r] = x[0]
                oi_ref[0, i, g, r] = x[1]

            _run_fft(fwd, -1, load, store_spec, temps)

    _interleaved_loop(ncol, one, scratch)


def _twiddle_table(n1, n2, scale=1.0):
    n = n1 * n2
    k1 = jnp.arange(n1).reshape(n1 // SUBLANES, 1, SUBLANES)
    m = (k1 * jnp.arange(n2).reshape(1, n2, 1)) % n
    ang = m.astype(F32) * (-2.0 * math.pi / n)
    shape = (n1 // SUBLANES, n2, SUBLANES, LANES)
    return (jnp.broadcast_to((scale * jnp.cos(ang))[..., None], shape),
            jnp.broadcast_to((scale * jnp.sin(ang))[..., None], shape))


def _fft_mid(zr, zi, tw, filt=None):
    p, na, nl, n2 = zr.shape[:4]
    conv = filt is not None
    bl = min(FFT_MID_COLS if conv else 2 * FFT_MID_COLS, nl)
    zspec = pl.BlockSpec((1, 1, bl, n2, SUBLANES, LANES), lambda i, g, pp: (pp, i, g, 0, 0, 0))
    twspec = pl.BlockSpec((1, n2, SUBLANES, LANES), lambda i, g, pp: (i, 0, 0, 0))
    in_specs = [zspec, zspec]
    args = [zr, zi]
    if conv:
        fidx = filt[2]
        kspec = pl.BlockSpec((1, 1, bl, n2, SUBLANES, LANES), lambda i, g, pp: (fidx, i, g, 0, 0, 0))
        in_specs += [kspec, kspec]
        args += [filt[0], filt[1]]
    in_specs += [twspec, twspec]
    args += [tw[0], tw[1]]
    if conv:
        out_specs = pl.BlockSpec((1, 1, bl, n2, 2 * SUBLANES, LANES), lambda i, g, pp: (pp, i, g, 0, 0, 0))
        out_shape = jax.ShapeDtypeStruct((p, na, nl, n2, 2 * SUBLANES, LANES), BF16)
    else:
        out_specs = [zspec, zspec]
        out_shape = [jax.ShapeDtypeStruct(zr.shape, F32)] * 2
    return pl.pallas_call(
        functools.partial(_fft_mid_kernel, n2=n2, ncol=bl, nl=bl, conv=conv),
        grid=(na, nl // bl, p),
        in_specs=in_specs,
        out_specs=out_specs,
        out_shape=out_shape,
        scratch_shapes=_fft_scratch(n2, FFT_MID_COLS),
        compiler_params=_cparams(("arbitrary", "arbitrary", "arbitrary")),
        name="fft_mid_conv" if conv else "fft_mid_spec",
    )(*args)


def _fft_ainv_kernel(z_ref, u_ref, x_ref, d_ref, o_ref, *scratch, n1, nj):
    plan = _fft_plan(n1, _radices(n1), +1)
    d = d_ref[...]

    def one(j, scr):
        temps = [(scr[0], scr[1]), (scr[2], scr[3])]
        lr, li = scr[4], scr[5]
        rows = pl.ds(pl.multiple_of(j * SUBLANES, SUBLANES), SUBLANES)
        for a in range(n1 // SUBLANES):
            sl = slice(a * SUBLANES, (a + 1) * SUBLANES)
            z = z_ref[0, a, 0, rows].astype(F32)
            lr[sl] = jnp.swapaxes(z[:, :SUBLANES], 0, 1)
            li[sl] = jnp.swapaxes(z[:, SUBLANES:], 0, 1)

        def store(i, y):
            if i < n1 // 2:
                o_ref[0, 0, i, rows, :] = x_ref[0, 0, i, rows, :] * (y[0] + u_ref[0, 0, i, rows, :] * d)
                o_ref[1, 0, i, rows, :] = x_ref[1, 0, i, rows, :] * (y[1] + u_ref[1, 0, i, rows, :] * d)

        _run_fft(plan, +1, lambda i: (lr[i], li[i]), store, temps)

    _interleaved_loop(nj, one, scratch)


def _fft_ainv(z, u, u_blk0, x, x_blk0, d, d_row):
    p, na, nl, n2 = z.shape[:4]
    n1 = na * SUBLANES
    bn2 = min(max(FFT_A_ROWS // n1, SUBLANES), n2)
    zspec = pl.BlockSpec((1, na, 1, bn2, 2 * SUBLANES, LANES), lambda pp, lb, j: (pp, 0, lb, j, 0, 0))

    def sspec(blk0):
        return pl.BlockSpec((2, 1, n1 // 2, bn2, LANES), lambda pp, lb, j: (0, pp, 0, j, blk0 + lb))

    return pl.pallas_call(
        functools.partial(_fft_ainv_kernel, n1=n1, nj=bn2 // SUBLANES),
        grid=(p, nl, n2 // bn2),
        in_specs=[zspec, sspec(u_blk0), sspec(x_blk0),
                  pl.BlockSpec((1, LANES), lambda pp, lb, j: (0, lb))],
        out_specs=sspec(0),
        out_shape=jax.ShapeDtypeStruct((2, p, n1 // 2, n2, D_INNER), F32),
        scratch_shapes=_fft_scratch(n1),
        compiler_params=_cparams(("arbitrary", "arbitrary", "arbitrary")),
        name="fft_a_inv",
    )(z, u, x, d[d_row].reshape(1, D_INNER))


def _hyena_layer(x, norm_w, w_in, conv_w, conv_b, fw1, fb1, fw2, fb2, fw3, fb3, ffreq, fwout,
                 bias_d, w_out, final_norm=None):
    b, l, _ = x.shape
    e = D_INNER
    n = 2 * l
    n2 = FFT_N2
    n1 = n // n2
    p = b // 2
    nl = N_LANE_BLOCKS
    x2d = x.reshape(b * l, D_MODEL)
    proj = _norm_matmul_conv(x2d, norm_w, w_in, conv_w, conv_b, l, act=False, dual=False)
    pv = proj.reshape(2, p, n1 // 2, n2, 4 * e)

    tw = _twiddle_table(n1, n2)
    kt = _hyena_two_sided_filters(l, fw1, fb1, fw2, fb2, fw3, fb3, ffreq, fwout)
    kr, ki = _fft_a(kt.reshape(2, n1, n2, e), n1, n2, packed=False)
    kfr, kfi = _fft_mid(kr, ki, _twiddle_table(n1, n2, scale=0.5 / n))

    zr, zi = _fft_a(pv, n1, n2, packed=True)
    zc = _fft_mid(zr, zi, tw, filt=(kfr, kfi, 0))
    z1 = _fft_ainv(zc, pv, 0, pv, nl, bias_d, 0)
    zr, zi = _fft_a(z1, n1, n2, packed=True)
    zc = _fft_mid(zr, zi, tw, filt=(kfr, kfi, 1))
    z2 = _fft_ainv(zc, z1, 0, pv, 2 * nl, bias_d, 1)
    out = _out_matmul([z2.reshape(b * l, e)], proj, 3, w_out, x2d, final_norm)
    return out.reshape(b, l, D_MODEL)


SWA_QBLOCKS = 2


def _swa_kernel(q_ref, k0_ref, k1_ref, k2_ref, k3_ref, v0_ref, v1_ref, v2_ref, v3_ref,
                cq_ref, sq_ref, c0_ref, s0_ref, c3_ref, s3_ref, sink_ref, o_ref, *, nb):
    n = pl.program_id(1)
    lane = lax.broadcasted_iota(jnp.int32, (BLOCK, HEAD_DIM), 1)
    half = ROT_DIM // 2

    def rope(x, c, s):
        sw = jnp.where(lane < half, pltpu.roll(x, HEAD_DIM - half, 1), pltpu.roll(x, half, 1))
        return x * c + sw * s

    cq, sq = cq_ref[...], sq_ref[...]
    kcos = (c0_ref[...], cq[:BLOCK], cq[BLOCK:], c3_ref[...])
    ksin = (s0_ref[...], sq[:BLOCK], sq[BLOCK:], s3_ref[...])
    k_refs = (k0_ref, k1_ref, k2_ref, k3_ref)
    v_refs = (v0_ref, v1_ref, v2_ref, v3_ref)
    qi = lax.broadcasted_iota(jnp.int32, (BLOCK, 3 * BLOCK), 0)
    kj = lax.broadcasted_iota(jnp.int32, (BLOCK, 3 * BLOCK), 1) - BLOCK
    band = jnp.abs(qi - kj) <= WINDOW
    masks = (band & ((kj >= 0) | (n > 0)), band & ((kj < BLOCK) | (n < nb // SWA_QBLOCKS - 1)))
    masks = [jnp.concatenate([mk] * GROUP, axis=0) for mk in masks]
    scale = HEAD_DIM ** -0.5
    for h in range(N_KV_HEADS):
        sl = slice(h * HEAD_DIM, (h + 1) * HEAD_DIM)
        kr = [rope(r[0, :, sl], c, s).astype(BF16) for r, c, s in zip(k_refs, kcos, ksin)]
        vr = [r[0, :, sl].astype(BF16) for r in v_refs]
        for u in range(SWA_QBLOCKS):
            kh = jnp.concatenate(kr[u:u + 3], axis=0)
            vh = jnp.concatenate(vr[u:u + 3], axis=0)
            rows = slice(u * BLOCK, (u + 1) * BLOCK)
            qs = []
            for g in range(GROUP):
                hq = h * GROUP + g
                qs.append(rope(q_ref[0, rows, hq * HEAD_DIM:(hq + 1) * HEAD_DIM], cq[rows], sq[rows]))
            qh = (jnp.concatenate(qs, axis=0) * scale).astype(BF16)
            s = lax.dot_general(qh, kh, (((1,), (1,)), ((), ())), preferred_element_type=F32)
            s = jnp.where(masks[u], s, -jnp.inf)
            sink = jnp.concatenate(
                [jnp.full((BLOCK, 1), 1.0, F32) * sink_ref[h * GROUP + g] for g in range(GROUP)], axis=0)
            m = jnp.maximum(jnp.max(s, axis=-1, keepdims=True), sink)
            pr = jnp.exp(s - m)
            den = jnp.sum(pr, axis=-1, keepdims=True) + jnp.exp(sink - m)
            o = jnp.dot(pr.astype(BF16), vh, preferred_element_type=F32) * (1.0 / den)
            for g in range(GROUP):
                hq = h * GROUP + g
                o_ref[0, rows, hq * HEAD_DIM:(hq + 1) * HEAD_DIM] = o[g * BLOCK:(g + 1) * BLOCK]


def _rope_tables(l):
    half = ROT_DIM // 2
    inv = ROPE_THETA ** (-jnp.arange(half, dtype=F32) / half)
    ang = jnp.arange(l, dtype=F32)[:, None] * inv[None, :]
    cos, sin = jnp.cos(ang), jnp.sin(ang)
    ones = jnp.ones((l, HEAD_DIM - ROT_DIM), F32)
    return (jnp.concatenate([cos, cos, ones], axis=-1),
            jnp.concatenate([-sin, sin, 0.0 * ones], axis=-1))


def _swa_layer(x, norm_w, w_in, sink, w_out, final_norm=None):
    b, l, _ = x.shape
    nb = l // BLOCK
    x2d = x.reshape(b * l, D_MODEL)
    w_perm = jnp.concatenate([w_in[:, :Q_DIM], w_in[:, Q_DIM + 2 * KV_DIM:],
                              w_in[:, Q_DIM:Q_DIM + 2 * KV_DIM]], axis=1)
    proj = _norm_matmul(x2d, norm_w, w_perm)
    p3 = proj.reshape(b, l, 2 * Q_DIM + 2 * KV_DIM)
    ct, st = _rope_tables(l)
    kblk = 2 * Q_DIM // KV_DIM
    nq = SWA_QBLOCKS
    blk = [lambda bb, i, o=o: jnp.clip(nq * i + o, 0, nb - 1) for o in range(-1, nq + 1)]
    kv = lambda col, f: pl.BlockSpec((1, BLOCK, KV_DIM), lambda bb, i: (bb, f(bb, i), col))
    tab = lambda f: pl.BlockSpec((BLOCK, HEAD_DIM), lambda bb, i: (f(bb, i), 0))
    qtab = pl.BlockSpec((nq * BLOCK, HEAD_DIM), lambda bb, i: (i, 0))
    o = pl.pallas_call(
        functools.partial(_swa_kernel, nb=nb),
        grid=(b, nb // nq),
        in_specs=[pl.BlockSpec((1, nq * BLOCK, Q_DIM), lambda bb, i: (bb, i, 0))]
        + [kv(kblk, f) for f in blk] + [kv(kblk + 1, f) for f in blk]
        + [qtab, qtab, tab(blk[0]), tab(blk[0]), tab(blk[-1]), tab(blk[-1]),
           pl.BlockSpec(memory_space=pltpu.SMEM)],
        out_specs=pl.BlockSpec((1, nq * BLOCK, Q_DIM), lambda bb, i: (bb, i, 0)),
        out_shape=jax.ShapeDtypeStruct((b, l, Q_DIM), F32),
        compiler_params=_cparams(("arbitrary", "arbitrary")),
        name="swa_attention",
    )(*([p3] * 9), ct, st, ct, st, ct, st, sink)
    out = _out_matmul([o.reshape(b * l, D_INNER)], proj, 1, w_out, x2d, final_norm)
    return out.reshape(b, l, D_MODEL)


def _ml_qkv_kernel(xc_ref, xi_ref, wq_ref, wk_ref, wv_ref, wg_ref, bg_ref, q_ref, k_ref, v_ref,
                   g_ref, gt_ref):
    nt = D_INNER // BD_TILE
    ng = 4 * ML_HEADS
    qs, ks, vs = [], [], []
    for j in range(nt):
        sl = slice(j * BD_TILE, (j + 1) * BD_TILE)
        xc = xc_ref[:, sl]
        qs.append(_dot3(xc, wq_ref[0, j], wq_ref[1, j]))
        ks.append(_dot3(xc, wk_ref[0, j], wk_ref[1, j]))
        vs.append(_dot3(xi_ref[:, sl], wv_ref[0, j], wv_ref[1, j]))
    qkv = jnp.concatenate(qs + ks + vs, axis=-1)
    xh = qkv.astype(BF16)
    xl = (qkv - xh.astype(F32)).astype(BF16)
    both = jnp.dot(xh, wg_ref[...], preferred_element_type=F32)
    low = jnp.dot(xl, wg_ref[:, :ng], preferred_element_type=F32)
    gates = both[:, :ng] + (both[:, ng:] + low) + bg_ref[...]
    g_ref[...] = gates
    gt_ref[...] = gates.T
    e = D_INNER
    q_ref[...] = qkv[:, :e].astype(BF16)
    k_ref[...] = (qkv[:, e:2 * e] * (ML_HEAD_DIM ** -0.5)).astype(BF16)
    v_ref[...] = qkv[:, 2 * e:].astype(BF16)


BD_TILE = 256


def _blockdiag_tiles(w):
    per = BD_TILE // ML_QKV_BLOCK
    wt = w.reshape(D_INNER // BD_TILE, per, ML_QKV_BLOCK, ML_QKV_BLOCK)
    eye = jnp.eye(per, dtype=w.dtype)
    return jnp.einsum("tnio,nm->tnimo", wt, eye).reshape(D_INNER // BD_TILE, BD_TILE, BD_TILE)


def _ml_qkv(xc2d, proj, wq, wk, wv, w_gate, b_gate, tm=256):
    t = xc2d.shape[0]
    tm = min(tm, t)
    e = D_INNER
    ng = 4 * ML_HEADS
    nt = e // BD_TILE
    wspec = pl.BlockSpec((2, nt, BD_TILE, BD_TILE), lambda i: (0, 0, 0, 0))
    act = jax.ShapeDtypeStruct((t, e), BF16)
    tiles = lambda w: _split_hi_lo(_blockdiag_tiles(w))
    wg = _split_hi_lo(w_gate)
    wg = jnp.concatenate([wg[0], wg[1]], axis=1)
    return pl.pallas_call(
        _ml_qkv_kernel,
        grid=(t // tm,),
        in_specs=[
            pl.BlockSpec((tm, e), lambda i: (i, 0)),
            pl.BlockSpec((tm, e), lambda i: (i, 0)),
            wspec, wspec, wspec,
            pl.BlockSpec((3 * e, 2 * ng), lambda i: (0, 0)),
            pl.BlockSpec((1, ng), lambda i: (0, 0)),
        ],
        out_specs=[pl.BlockSpec((tm, e), lambda i: (i, 0))] * 3
        + [pl.BlockSpec((tm, ng), lambda i: (i, 0)), pl.BlockSpec((ng, tm), lambda i: (0, i))],
        out_shape=[act, act, act, jax.ShapeDtypeStruct((t, ng), F32),
                   jax.ShapeDtypeStruct((ng, t), F32)],
        compiler_params=_cparams(("arbitrary",)),
        name="mlstm_qkv_gates",
    )(xc2d, proj, tiles(wq), tiles(wk), tiles(wv), wg, b_gate.reshape(1, ng))


def _log_sigmoid(x):
    return jnp.minimum(x, 0.0) - jnp.log(1.0 + jnp.exp(-jnp.abs(x)))


def _ml_chunk(d, hd, q_ref, k_ref, v_ref, g, gt, o_ref, c_ref, n_ref, m_ref):
    lc = ML_CHUNK
    ng = 4 * ML_HEADS
    dh = ML_HEAD_DIM
    hs = slice(hd * dh, (hd + 1) * dh)
    col_i = 2 * ML_HEADS * d + hd
    col_f = col_i + ML_HEADS
    lane = lax.broadcasted_iota(jnp.int32, (lc, ng), 1)
    i_col = jnp.sum(jnp.where(lane == col_i, g, 0.0), axis=-1, keepdims=True)
    f_col = _log_sigmoid(jnp.sum(jnp.where(lane == col_f, g, 0.0), axis=-1, keepdims=True))
    sub = lax.broadcasted_iota(jnp.int32, (ng, lc), 0)
    i_row = jnp.sum(jnp.where(sub == col_i, gt, 0.0), axis=0, keepdims=True)
    f_row = _log_sigmoid(jnp.sum(jnp.where(sub == col_f, gt, 0.0), axis=0, keepdims=True))

    t_idx = lax.broadcasted_iota(jnp.int32, (lc, lc), 0)
    s_idx = lax.broadcasted_iota(jnp.int32, (lc, lc), 1)
    if d == 0:
        causal = s_idx <= t_idx
        causal_t = t_idx <= s_idx
    else:
        causal = s_idx >= t_idx
        causal_t = t_idx >= s_idx
    b_col = jnp.sum(jnp.where(causal, f_row, 0.0), axis=-1, keepdims=True)
    b_row = jnp.sum(jnp.where(causal_t, f_col, 0.0), axis=0, keepdims=True)
    m_prev = m_ref[d, hd]
    dmat = jnp.where(causal, b_col - b_row + i_row, -jnp.inf)
    inter = b_col + m_prev
    m_out = jnp.maximum(inter, jnp.max(dmat, axis=-1, keepdims=True))
    w_intra = jnp.exp(dmat - m_out)
    w_state = jnp.exp(inter - m_out)

    q = q_ref[:, hs]
    k = k_ref[:, hs]
    v = v_ref[:, hs]
    cst = c_ref[d, hd]
    s = lax.dot_general(q, k, (((1,), (1,)), ((), ())), preferred_element_type=F32) * w_intra
    num = jnp.dot(s.astype(BF16), v, preferred_element_type=F32)
    num = num + w_state * jnp.dot(q, cst.astype(BF16), preferred_element_type=F32)
    qn = jnp.sum(q.astype(F32) * n_ref[d, hd], axis=-1, keepdims=True)
    den = jnp.sum(s, axis=-1, keepdims=True) + w_state * qn
    o_ref[:, hs] = num / jnp.maximum(jnp.abs(den), jnp.exp(-m_out))

    last = lc - 1 if d == 0 else 0
    bl = jnp.sum(f_row, axis=-1, keepdims=True)
    m_new = m_out[last:last + 1]
    wk = jnp.exp(bl - b_col + i_col - m_new)
    decay = jnp.exp(bl + m_prev - m_new)
    vw = (v.astype(F32) * wk).astype(BF16)
    c_ref[d, hd] = decay * cst + lax.dot_general(
        k, vw, (((0,), (0,)), ((), ())), preferred_element_type=F32)
    n_ref[d, hd] = decay * n_ref[d, hd] + jnp.sum(k.astype(F32) * wk, axis=0, keepdims=True)
    m_ref[d, hd] = m_new


def _ml_scan_kernel(qf_ref, kf_ref, vf_ref, gf_ref, gtf_ref, qb_ref, kb_ref, vb_ref, gb_ref, gtb_ref,
                    of_ref, ob_ref, c_ref, n_ref, m_ref):
    @pl.when(pl.program_id(1) == 0)
    def _():
        c_ref[...] = jnp.zeros_like(c_ref)
        n_ref[...] = jnp.zeros_like(n_ref)
        m_ref[...] = jnp.zeros_like(m_ref)

    dirs = ((qf_ref, kf_ref, vf_ref, gf_ref, gtf_ref, of_ref),
            (qb_ref, kb_ref, vb_ref, gb_ref, gtb_ref, ob_ref))
    for d, (q_ref, k_ref, v_ref, g_ref, gt_ref, o_ref) in enumerate(dirs):
        g = g_ref[...]
        gt = gt_ref[...]
        for hd in range(ML_HEADS):
            _ml_chunk(d, hd, q_ref, k_ref, v_ref, g, gt, o_ref, c_ref, n_ref, m_ref)


def _ml_scan(q, k, v, gates, gates_t, b, l):
    t = b * l
    nc = l // ML_CHUNK
    dh = ML_HEAD_DIM
    e = D_INNER
    ng = 4 * ML_HEADS
    fw = lambda bb, c: bb * nc + c
    bw = lambda bb, c: bb * nc + nc - 1 - c

    def specs(ch):
        qspec = pl.BlockSpec((ML_CHUNK, e), lambda bb, c: (ch(bb, c), 0))
        return [qspec, qspec, qspec,
                pl.BlockSpec((ML_CHUNK, ng), lambda bb, c: (ch(bb, c), 0)),
                pl.BlockSpec((ng, ML_CHUNK), lambda bb, c: (0, ch(bb, c)))]

    out = jax.ShapeDtypeStruct((t, e), F32)
    return pl.pallas_call(
        _ml_scan_kernel,
        grid=(b, nc),
        in_specs=specs(fw) + specs(bw),
        out_specs=[pl.BlockSpec((ML_CHUNK, e), lambda bb, c: (fw(bb, c), 0)),
                   pl.BlockSpec((ML_CHUNK, e), lambda bb, c: (bw(bb, c), 0))],
        out_shape=[out, out],
        scratch_shapes=[pltpu.VMEM((2, ML_HEADS, dh, dh), F32), pltpu.VMEM((2, ML_HEADS, 1, dh), F32),
                        pltpu.VMEM((2, ML_HEADS, 1, 1), F32)],
        compiler_params=_cparams(("arbitrary", "arbitrary")),
        name="mlstm_scan",
    )(q, k, v, gates, gates_t, q, k, v, gates, gates_t)


def _mlstm_layer(x, norm_w, w_in, conv_w, conv_b, wq, wk, wv, w_gate, b_gate, head_norm, skip,
                 w_out, final_norm=None):
    b, l, _ = x.shape
    x2d = x.reshape(b * l, D_MODEL)
    proj, xc = _norm_matmul_conv(x2d, norm_w, w_in, conv_w, conv_b, l, act=True, dual=True, tm=1024)
    q, k, v, gates, gates_t = _ml_qkv(xc, proj, wq, wk, wv, w_gate, b_gate)
    h_fwd, h_bwd = _ml_scan(q, k, v, gates, gates_t, b, l)
    out = _out_matmul([h_fwd, h_bwd, xc], proj, 1, w_out, x2d, final_norm,
                      extra=(head_norm, skip))
    return out.reshape(b, l, D_MODEL)


def _trunk(x, layers, final_norm):
    kinds = (_hyena_layer, _swa_layer, _mlstm_layer, _hyena_layer)
    for i, (fn, params) in enumerate(zip(kinds, layers)):
        x = fn(x, *params, final_norm=final_norm if i == len(kinds) - 1 else None)
    return x


def kernel(x_prompt, x_sample, l0_norm, l0_hy_w_in, l0_hy_conv_w, l0_hy_conv_b, l0_hy_filt_w1, l0_hy_filt_b1, l0_hy_filt_w2, l0_hy_filt_b2, l0_hy_filt_w3, l0_hy_filt_b3, l0_hy_filt_freq, l0_hy_filt_wout, l0_hy_bias_d, l0_hy_w_out, l1_norm, l1_swa_w_in, l1_swa_sink, l1_swa_w_out, l2_norm, l2_ml_w_in, l2_ml_conv_w, l2_ml_conv_b, l2_ml_wq, l2_ml_wk, l2_ml_wv, l2_ml_w_gate, l2_ml_b_gate, l2_ml_head_norm, l2_ml_skip, l2_ml_w_out, l3_norm, l3_hy_w_in, l3_hy_conv_w, l3_hy_conv_b, l3_hy_filt_w1, l3_hy_filt_b1, l3_hy_filt_w2, l3_hy_filt_b2, l3_hy_filt_w3, l3_hy_filt_b3, l3_hy_filt_freq, l3_hy_filt_wout, l3_hy_bias_d, l3_hy_w_out, final_norm):
    layers = [
        (l0_norm, l0_hy_w_in, l0_hy_conv_w, l0_hy_conv_b, l0_hy_filt_w1, l0_hy_filt_b1, l0_hy_filt_w2,
         l0_hy_filt_b2, l0_hy_filt_w3, l0_hy_filt_b3, l0_hy_filt_freq, l0_hy_filt_wout, l0_hy_bias_d,
         l0_hy_w_out),
        (l1_norm, l1_swa_w_in, l1_swa_sink, l1_swa_w_out),
        (l2_norm, l2_ml_w_in, l2_ml_conv_w, l2_ml_conv_b, l2_ml_wq, l2_ml_wk, l2_ml_wv, l2_ml_w_gate,
         l2_ml_b_gate, l2_ml_head_norm, l2_ml_skip, l2_ml_w_out),
        (l3_norm, l3_hy_w_in, l3_hy_conv_w, l3_hy_conv_b, l3_hy_filt_w1, l3_hy_filt_b1, l3_hy_filt_w2,
         l3_hy_filt_b2, l3_hy_filt_w3, l3_hy_filt_b3, l3_hy_filt_freq, l3_hy_filt_wout, l3_hy_bias_d,
         l3_hy_w_out),
    ]
    return (_trunk(x_prompt, layers, final_norm), _trunk(x_sample, layers, final_norm))
```

```python
import cmath
import functools
import math

import jax
import jax.numpy as jnp
from jax import lax
from jax.experimental import pallas as pl
from jax.experimental.pallas import tpu as pltpu

F32 = jnp.float32
BF16 = jnp.bfloat16

D_MODEL = 1024
D_INNER = 2048
RMS_EPS = 1e-6
LN_EPS = 1e-6

LANES = 128
SUBLANES = 8
N_LANE_BLOCKS = D_INNER // LANES
VMEM_LIMIT = 56 * 1024 * 1024

HY_EMB = 33
HY_FILT = 64
HY_MIN_DECAY = math.log(1e-2) / 1.5
HY_MAX_DECAY = math.log(1e-2) / 0.3
FFT_N2 = 128
FFT_N2_RADICES = (8, 4, 4)
FFT_A_ROWS = 8192
FFT_MID_COLS = 4

HEAD_DIM = 128
N_Q_HEADS = D_INNER // HEAD_DIM
N_KV_HEADS = 4
GROUP = N_Q_HEADS // N_KV_HEADS
WINDOW = 128
BLOCK = 128
ROT_DIM = HEAD_DIM // 4
ROPE_THETA = 500000.0
Q_DIM = N_Q_HEADS * HEAD_DIM
KV_DIM = N_KV_HEADS * HEAD_DIM

ML_HEADS = 4
ML_HEAD_DIM = D_INNER // ML_HEADS
ML_QKV_BLOCK = 4
ML_CHUNK = 128


def _cparams(sem):
    return pltpu.CompilerParams(dimension_semantics=sem, vmem_limit_bytes=VMEM_LIMIT)


def _silu(x):
    return x * (1.0 / (1.0 + jnp.exp(-x)))


def _split_hi_lo(w):
    hi = w.astype(BF16)
    lo = (w - hi.astype(F32)).astype(BF16)
    return jnp.stack([hi, lo])


def _dot3(x, w_hi, w_lo):
    xh = x.astype(BF16)
    xl = (x - xh.astype(F32)).astype(BF16)
    d = lambda a, b: jnp.dot(a, b, preferred_element_type=F32)
    return d(xh, w_hi) + (d(xl, w_hi) + d(xh, w_lo))


def _add(a, b):
    if a is None:
        return b
    if b is None:
        return a
    return a + b


def _sub(a, b):
    if b is None:
        return a
    if a is None:
        return -b
    return a - b


def _scale(a, c):
    if a is None or c == 0.0:
        return None
    if c == 1.0:
        return a
    if c == -1.0:
        return -a
    return a * c


def _cadd(x, y):
    return (_add(x[0], y[0]), _add(x[1], y[1]))


def _csub(x, y):
    return (_sub(x[0], y[0]), _sub(x[1], y[1]))


def _snap(v):
    for t in (0.0, 1.0, -1.0):
        if abs(v - t) < 1e-14:
            return t
    return v


def _cmul_const(x, c):
    cr, ci = _snap(c.real), _snap(c.imag)
    if cr != 0.0 and abs(abs(cr) - abs(ci)) < 1e-14:
        sr, si = math.copysign(1.0, cr), math.copysign(1.0, ci)
        re = _sub(_scale(x[0], sr), _scale(x[1], si))
        im = _add(_scale(x[0], si), _scale(x[1], sr))
        return (_scale(re, abs(cr)), _scale(im, abs(cr)))
    re = _sub(_scale(x[0], cr), _scale(x[1], ci))
    im = _add(_scale(x[0], ci), _scale(x[1], cr))
    return (re, im)


def _cmul(x, y):
    xr, xi = x
    yr, yi = y
    if xi is None:
        return (xr * yr, xr * yi)
    return (xr * yr - xi * yi, xr * yi + xi * yr)


def _dense(x):
    re, im = x
    if re is None:
        re = jnp.zeros_like(im)
    if im is None:
        im = jnp.zeros_like(re)
    return (re, im)


def _small_dft(xs, sign):
    r = len(xs)
    if r == 1:
        return list(xs)
    ev = _small_dft(xs[0::2], sign)
    od = _small_dft(xs[1::2], sign)
    out = [None] * r
    for k in range(r // 2):
        t = _cmul_const(od[k], cmath.exp(sign * 2j * math.pi * k / r))
        out[k] = _cadd(ev[k], t)
        out[k + r // 2] = _csub(ev[k], t)
    return out


def _fft_plan(n, radices, sign):
    assert math.prod(radices) == n
    passes = []
    s, ncur = 1, n
    for r in radices:
        m = ncur // r
        bfs = []
        for p in range(m):
            tws = [cmath.exp(sign * 2j * math.pi * p * k / ncur) for k in range(r)]
            for q in range(s):
                ins = [q + s * (p + m * j) for j in range(r)]
                outs = [q + s * (r * p + k) for k in range(r)]
                bfs.append((ins, outs, tws))
        passes.append(bfs)
        ncur, s = m, s * r
    return passes


def _run_fft(plan, sign, load, store, temps):
    npass = len(plan)
    for pi, bfs in enumerate(plan):
        if pi == 0:
            ld = load
        else:
            tr, ti = temps[(pi - 1) % 2]
            ld = lambda i, tr=tr, ti=ti: (tr[i], ti[i])
        if pi == npass - 1:
            st = store
        else:
            tr2, ti2 = temps[pi % 2]

            def st(i, x, tr2=tr2, ti2=ti2):
                tr2[i] = x[0]
                ti2[i] = x[1]
        for ins, outs, tws in bfs:
            ys = _small_dft([ld(i) for i in ins], sign)
            for o, y, tw in zip(outs, ys, tws):
                st(o, _dense(_cmul_const(y, tw)))


def _radices(n):
    out = []
    while n > 1:
        r = 8 if n % 8 == 0 else (4 if n % 4 == 0 else 2)
        out.append(r)
        n //= r
    return tuple(out)


def _norm_matmul_kernel(x_ref, nw_ref, w_ref, o_ref, xn_ref):
    @pl.when(pl.program_id(1) == 0)
    def _():
        x = x_ref[...]
        inv = lax.rsqrt(jnp.mean(x * x, axis=-1, keepdims=True) + RMS_EPS)
        xn_ref[...] = ((x * inv) * nw_ref[...]).astype(BF16)

    o_ref[...] = jnp.dot(xn_ref[...], w_ref[...], preferred_element_type=F32)


def _norm_matmul(x2d, norm_w, w, tm=2048, tn=1024):
    t, d = x2d.shape
    n = w.shape[1]
    tm = min(tm, t)
    tn = min(tn, n)
    return pl.pallas_call(
        _norm_matmul_kernel,
        grid=(t // tm, n // tn),
        in_specs=[
            pl.BlockSpec((tm, d), lambda i, j: (i, 0)),
            pl.BlockSpec((1, d), lambda i, j: (0, 0)),
            pl.BlockSpec((d, tn), lambda i, j: (0, j)),
        ],
        out_specs=pl.BlockSpec((tm, tn), lambda i, j: (i, j)),
        out_shape=jax.ShapeDtypeStruct((t, n), F32),
        scratch_shapes=[pltpu.VMEM((tm, d), BF16)],
        compiler_params=_cparams(("arbitrary", "arbitrary")),
        name="norm_matmul",
    )(x2d, norm_w.reshape(1, d), w.astype(BF16))


HALO = 16


def _norm_matmul_conv_kernel(x_ref, xp_ref, xn_ref, nw_ref, w_ref, cw_ref, cb_ref, *rest,
                             tm, nconv, seq_tiles, act, dual):
    if dual:
        o_ref, oc_ref, xs_ref = rest
    else:
        o_ref, xs_ref = rest
    i = pl.program_id(0)
    j = pl.program_id(1)

    @pl.when(j == 0)
    def _():
        def nrm(x):
            inv = lax.rsqrt(jnp.mean(x * x, axis=-1, keepdims=True) + RMS_EPS)
            return (x * inv) * nw_ref[...]

        first = (i % seq_tiles) == 0
        last = (i % seq_tiles) == seq_tiles - 1
        xs_ref[0:HALO] = jnp.where(first, 0.0, nrm(xp_ref[...])).astype(BF16)
        xs_ref[HALO:HALO + tm] = nrm(x_ref[...]).astype(BF16)
        xs_ref[HALO + tm:] = jnp.where(last, 0.0, nrm(xn_ref[...])).astype(BF16)

    @pl.when(j < nconv)
    def _():
        rows = tm + 2 * HALO
        tn = w_ref.shape[1]
        nsplit = 4 if tn % (8 * LANES) == 0 else 1
        for c in range(nsplit):
            cs = slice(c * tn // nsplit, (c + 1) * tn // nsplit)
            acc = jnp.dot(xs_ref[...], w_ref[:, cs], preferred_element_type=F32)
            y = (acc * cw_ref[1:2, cs] + pltpu.roll(acc, 1, 0) * cw_ref[0:1, cs]
                 + pltpu.roll(acc, rows - 1, 0) * cw_ref[2:3, cs] + cb_ref[:, cs])
            y = y[HALO:HALO + tm]
            if act:
                y = _silu(y)
            if dual:
                oc_ref[:, cs] = y
                o_ref[:, cs] = acc[HALO:HALO + tm]
            else:
                o_ref[:, cs] = y

    @pl.when(j >= nconv)
    def _():
        o_ref[...] = jnp.dot(xs_ref[HALO:HALO + tm], w_ref[...], preferred_element_type=F32)


def _norm_matmul_conv(x2d, norm_w, w, conv_w, conv_b, seq_len, act, dual, tm=2048, tn=1024):
    t, d = x2d.shape
    n = w.shape[1]
    nc = conv_w.shape[1]
    tm = min(tm, seq_len)
    tn = min(tn, nc)
    nconv = nc // tn
    hb = tm // HALO
    nhb = t // HALO
    cj = lambda j: jnp.minimum(j, nconv - 1)
    in_specs = [
        pl.BlockSpec((tm, d), lambda i, j: (i, 0)),
        pl.BlockSpec((HALO, d), lambda i, j: (jnp.maximum(i * hb - 1, 0), 0)),
        pl.BlockSpec((HALO, d), lambda i, j: (jnp.minimum((i + 1) * hb, nhb - 1), 0)),
        pl.BlockSpec((1, d), lambda i, j: (0, 0)),
        pl.BlockSpec((d, tn), lambda i, j: (0, j)),
        pl.BlockSpec((3, tn), lambda i, j: (0, cj(j))),
        pl.BlockSpec((1, tn), lambda i, j: (0, cj(j))),
    ]
    out_specs = pl.BlockSpec((tm, tn), lambda i, j: (i, j))
    out_shape = jax.ShapeDtypeStruct((t, n), F32)
    if dual:
        out_specs = [out_specs, pl.BlockSpec((tm, tn), lambda i, j: (i, cj(j)))]
        out_shape = [out_shape, jax.ShapeDtypeStruct((t, nc), F32)]
    return pl.pallas_call(
        functools.partial(_norm_matmul_conv_kernel, tm=tm, nconv=nconv, seq_tiles=seq_len // tm,
                          act=act, dual=dual),
        grid=(t // tm, n // tn),
        in_specs=in_specs,
        out_specs=out_specs,
        out_shape=out_shape,
        scratch_shapes=[pltpu.VMEM((tm + 2 * HALO, d), BF16)],
        compiler_params=_cparams(("arbitrary", "arbitrary")),
        name="norm_matmul_conv",
    )(x2d, x2d, x2d, norm_w.reshape(1, d), w.astype(BF16), conv_w, conv_b.reshape(1, nc))


def _out_kernel(*refs, mode, final):
    if mode == "mlstm":
        hf_ref, hb_ref, xc_ref, g_ref, hn_ref, sk_ref, w_ref, r_ref = refs[:8]
        rest = refs[8:]
        h = hf_ref[...] + hb_ref[...]
        parts = []
        for hd in range(ML_HEADS):
            seg = h[:, hd * ML_HEAD_DIM:(hd + 1) * ML_HEAD_DIM]
            mu = jnp.mean(seg, axis=-1, keepdims=True)
            cen = seg - mu
            var = jnp.mean(cen * cen, axis=-1, keepdims=True)
            parts.append(cen * lax.rsqrt(var + LN_EPS))
        a = jnp.concatenate(parts, axis=-1) * hn_ref[...] + sk_ref[...] * xc_ref[...]
    else:
        a_ref, g_ref, w_ref, r_ref = refs[:4]
        rest = refs[4:]
        a = a_ref[...]
    if final:
        fn_ref, o_ref = rest
    else:
        (o_ref,) = rest
    y = jnp.dot((a * _silu(g_ref[...])).astype(BF16), w_ref[...], preferred_element_type=F32)
    x = r_ref[...] + y
    if final:
        inv = lax.rsqrt(jnp.mean(x * x, axis=-1, keepdims=True) + RMS_EPS)
        x = (x * inv) * fn_ref[...]
    o_ref[...] = x


def _out_matmul(acts, gate_arr, gate_blk, w, resid, final_norm=None, extra=(), tm=512):
    t = resid.shape[0]
    tm = min(tm, t)
    e = D_INNER
    mode = "mlstm" if len(acts) == 3 else "plain"
    in_specs, args = [], []
    for a in acts:
        if isinstance(a, tuple):
            arr, idx = a
            in_specs.append(pl.BlockSpec((None, tm, e), lambda i, idx=idx: (idx, i, 0)))
            args.append(arr)
        else:
            in_specs.append(pl.BlockSpec((tm, e), lambda i: (i, 0)))
            args.append(a)
    in_specs.append(pl.BlockSpec((tm, e), lambda i: (i, gate_blk)))
    args.append(gate_arr)
    for v in extra:
        in_specs.append(pl.BlockSpec((1, e), lambda i: (0, 0)))
        args.append(v.reshape(1, e))
    in_specs.append(pl.BlockSpec((e, D_MODEL), lambda i: (0, 0)))
    args.append(w.astype(BF16))
    in_specs.append(pl.BlockSpec((tm, D_MODEL), lambda i: (i, 0)))
    args.append(resid)
    if final_norm is not None:
        in_specs.append(pl.BlockSpec((1, D_MODEL), lambda i: (0, 0)))
        args.append(final_norm.reshape(1, D_MODEL))
    return pl.pallas_call(
        functools.partial(_out_kernel, mode=mode, final=final_norm is not None),
        grid=(t // tm,),
        in_specs=in_specs,
        out_specs=pl.BlockSpec((tm, D_MODEL), lambda i: (i, 0)),
        out_shape=jax.ShapeDtypeStruct((t, D_MODEL), F32),
        compiler_params=_cparams(("arbitrary",)),
        name="out_matmul_" + mode,
    )(*args)


def _filter_kernel(feat_ref, t_ref, w1_ref, b1_ref, w2_ref, b2_ref, w3_ref, b3_ref, fr_ref,
                   wo_ref, dl_ref, o_ref):
    fr = fr_ref[...]
    h = jnp.sin(fr * (_dot3(feat_ref[...], w1_ref[0], w1_ref[1]) + b1_ref[...]))
    h = jnp.sin(fr * (_dot3(h, w2_ref[0], w2_ref[1]) + b2_ref[...]))
    h = jnp.sin(fr * (_dot3(h, w3_ref[0], w3_ref[1]) + b3_ref[...]))
    tcol = t_ref[...]
    decay = jnp.exp(-tcol[:, 0:1] * dl_ref[...]) * tcol[:, 1:2]
    for f in range(2):
        k = _dot3(h, wo_ref[0, f], wo_ref[1, f])
        o_ref[f] = k * decay


def _hyena_two_sided_filters(l, fw1, fb1, fw2, fb2, fw3, fb3, ffreq, fwout, tr=1024):
    n = 2 * l
    e = D_INNER
    tr = min(tr, l)
    idx = jnp.arange(n)
    src = jnp.where(idx < l, idx, n - idx)
    valid = (idx != l).astype(F32)
    src = jnp.where(idx == l, 0, src)
    t_all = jnp.linspace(0.0, 1.0, l, dtype=F32)
    bands = (HY_EMB - 1) // 2
    t_src = t_all[src]
    ang = 2.0 * math.pi * src.astype(F32)[:, None] / l
    fb = jnp.linspace(1e-4, bands - 1, bands, dtype=F32)[None, :]
    feat = jnp.concatenate([t_src[:, None], jnp.cos(fb * ang), -jnp.sin(fb * ang)], axis=-1)
    feat = jnp.pad(feat, ((0, 0), (0, LANES - HY_EMB)))
    tcol = jnp.stack([t_src, valid], axis=-1)
    w1p = jnp.pad(fw1, ((0, LANES - HY_EMB), (0, 0)))
    deltas = jnp.abs(jnp.linspace(HY_MIN_DECAY, HY_MAX_DECAY, e, dtype=F32)).reshape(1, e)
    wo = _split_hi_lo(fwout.reshape(HY_FILT, 4, e).transpose(1, 0, 2))
    nhalf = l // tr
    row = lambda v: v.reshape(1, HY_FILT)
    full = lambda shp: pl.BlockSpec(shp, lambda i: tuple(0 for _ in shp))
    return pl.pallas_call(
        _filter_kernel,
        grid=(n // tr,),
        in_specs=[
            pl.BlockSpec((tr, LANES), lambda i: (i, 0)),
            pl.BlockSpec((tr, 2), lambda i: (i, 0)),
            full((2, LANES, HY_FILT)), full((1, HY_FILT)),
            full((2, HY_FILT, HY_FILT)), full((1, HY_FILT)),
            full((2, HY_FILT, HY_FILT)), full((1, HY_FILT)),
            full((1, HY_FILT)),
            pl.BlockSpec((2, 2, HY_FILT, e), lambda i: (0, i // nhalf, 0, 0)),
            full((1, e)),
        ],
        out_specs=pl.BlockSpec((2, tr, e), lambda i: (0, i, 0)),
        out_shape=jax.ShapeDtypeStruct((2, n, e), F32),
        compiler_params=_cparams(("arbitrary",)),
        name="hyena_filter",
    )(feat, tcol, _split_hi_lo(w1p), row(fb1), _split_hi_lo(fw2), row(fb2), _split_hi_lo(fw3),
      row(fb3), row(ffreq), wo, deltas)


FFT_SCRATCH = 6
FFT_INTERLEAVE = 2


def _interleaved_loop(n, one, scratch):
    u = len(scratch) // FFT_SCRATCH
    while n % u:
        u //= 2

    def body(c, carry):
        for k in range(u):
            one(c * u + k, scratch[k * FFT_SCRATCH:(k + 1) * FFT_SCRATCH])
        return carry

    lax.fori_loop(0, n // u, body, 0)


def _fft_scratch(n, interleave=FFT_INTERLEAVE):
    return [pltpu.VMEM((n, SUBLANES, LANES), F32) for _ in range(FFT_SCRATCH * interleave)]


def _fft_a_kernel(x_ref, zr_ref, zi_ref, *scratch, n1, nj, packed):
    plan = _fft_plan(n1, _radices(n1), -1)

    def one(j, scr):
        temps = [(scr[0], scr[1]), (scr[2], scr[3])]
        fr, fi = scr[4], scr[5]
        rows = pl.ds(pl.multiple_of(j * SUBLANES, SUBLANES), SUBLANES)

        def load(i):
            if packed:
                if i >= n1 // 2:
                    return (None, None)
                return (x_ref[0, 0, i, rows, :], x_ref[1, 0, i, rows, :])
            return (x_ref[0, i, rows, :], x_ref[1, i, rows, :])

        def store(i, x):
            fr[i] = x[0]
            fi[i] = x[1]

        _run_fft(plan, -1, load, store, temps)
        if packed:
            planes = ((0, fr, fi),)
        else:
            ar, ai, br, bi = scr[0], scr[1], scr[2], scr[3]
            for k in range(n1):
                m = (n1 - k) % n1
                ar[k] = fr[k] + fr[m]
                ai[k] = fi[k] - fi[m]
                br[k] = fi[k] + fi[m]
                bi[k] = fr[m] - fr[k]
            planes = ((0, ar, ai), (1, br, bi))
        for f, pr, pi in planes:
            for a in range(n1 // SUBLANES):
                sl = slice(a * SUBLANES, (a + 1) * SUBLANES)
                zr_ref[f, a, 0, rows] = jnp.swapaxes(pr[sl], 0, 1)
                zi_ref[f, a, 0, rows] = jnp.swapaxes(pi[sl], 0, 1)

    _interleaved_loop(nj, one, scratch)


def _fft_a(x, n1, n2, packed, lane_blk0=0):
    nl = N_LANE_BLOCKS
    if packed:
        p = x.shape[1]
        bn2 = min(max(FFT_A_ROWS // n1, SUBLANES), n2)
        in_spec = pl.BlockSpec((2, 1, n1 // 2, bn2, LANES),
                               lambda pp, lb, j: (0, pp, 0, j, lane_blk0 + lb))
        nf = 1
    else:
        p = 1
        bn2 = min(max(FFT_A_ROWS // (2 * n1), SUBLANES), n2)
        in_spec = pl.BlockSpec((2, n1, bn2, LANES), lambda pp, lb, j: (0, 0, j, lb))
        nf = 2
    out_spec = pl.BlockSpec((nf, n1 // SUBLANES, 1, bn2, SUBLANES, LANES),
                            lambda pp, lb, j: (pp, 0, lb, j, 0, 0))
    shp = jax.ShapeDtypeStruct((p * nf, n1 // SUBLANES, nl, n2, SUBLANES, LANES), F32)
    return pl.pallas_call(
        functools.partial(_fft_a_kernel, n1=n1, nj=bn2 // SUBLANES, packed=packed),
        grid=(p, nl, n2 // bn2),
        in_specs=[in_spec],
        out_specs=[out_spec, out_spec],
        out_shape=[shp, shp],
        scratch_shapes=_fft_scratch(n1),
        compiler_params=_cparams(("arbitrary", "arbitrary", "arbitrary")),
        name="fft_a_packed" if packed else "fft_a_pair",
    )(x)


def _fft_mid_kernel(*refs, n2, ncol, nl, conv):
    if conv:
        zr_ref, zi_ref, kr_ref, ki_ref, twr_ref, twi_ref, or_ref = refs[:7]
        scratch = refs[7:]
    else:
        zr_ref, zi_ref, twr_ref, twi_ref, or_ref, oi_ref = refs[:6]
        scratch = refs[6:]
    fwd = _fft_plan(n2, FFT_N2_RADICES, -1)
    inv = _fft_plan(n2, FFT_N2_RADICES, +1)

    def one(c, scr):
        temps = [(scr[0], scr[1]), (scr[2], scr[3])]
        sr, si = scr[4], scr[5]
        i = c // nl
        g = c % nl

        def tw(r):
            return (twr_ref[i, r], twi_ref[i, r])

        def load(r):
            return _cmul((zr_ref[0, i, g, r], zi_ref[0, i, g, r]), tw(r))

        if conv:
            def store_spec(r, x):
                y = _cmul(x, (kr_ref[0, i, g, r], ki_ref[0, i, g, r]))
                sr[r] = y[0]
                si[r] = y[1]

            _run_fft(fwd, -1, load, store_spec, temps)

            def store_out(r, x):
                tr, ti = tw(r)
                yr = x[0] * tr + x[1] * ti
                yi = x[1] * tr - x[0] * ti
                or_ref[0, i, g, r] = jnp.concatenate([yr, yi], axis=0).astype(BF16)

            _run_fft(inv, +1, lambda r: (sr[r], si[r]), store_out, temps)
        else:
            def store_spec(r, x):
                or_ref[0, i, g, r] = x[0]
                oi_ref[0, i, g, r] = x[1]

            _run_fft(fwd, -1, load, store_spec, temps)

    _interleaved_loop(ncol, one, scratch)


def _twiddle_table(n1, n2, scale=1.0):
    n = n1 * n2
    k1 = jnp.arange(n1).reshape(n1 // SUBLANES, 1, SUBLANES)
    m = (k1 * jnp.arange(n2).reshape(1, n2, 1)) % n
    ang = m.astype(F32) * (-2.0 * math.pi / n)
    shape = (n1 // SUBLANES, n2, SUBLANES, LANES)
    return (jnp.broadcast_to((scale * jnp.cos(ang))[..., None], shape),
            jnp.broadcast_to((scale * jnp.sin(ang))[..., None], shape))


def _fft_mid(zr, zi, tw, filt=None):
    p, na, nl, n2 = zr.shape[:4]
    conv = filt is not None
    bl = min(FFT_MID_COLS if conv else 2 * FFT_MID_COLS, nl)
    zspec = pl.BlockSpec((1, 1, bl, n2, SUBLANES, LANES), lambda i, g, pp: (pp, i, g, 0, 0, 0))
    twspec = pl.BlockSpec((1, n2, SUBLANES, LANES), lambda i, g, pp: (i, 0, 0, 0))
    in_specs = [zspec, zspec]
    args = [zr, zi]
    if conv:
        fidx = filt[2]
        kspec = pl.BlockSpec((1, 1, bl, n2, SUBLANES, LANES), lambda i, g, pp: (fidx, i, g, 0, 0, 0))
        in_specs += [kspec, kspec]
        args += [filt[0], filt[1]]
    in_specs += [twspec, twspec]
    args += [tw[0], tw[1]]
    if conv:
        out_specs = pl.BlockSpec((1, 1, bl, n2, 2 * SUBLANES, LANES), lambda i, g, pp: (pp, i, g, 0, 0, 0))
        out_shape = jax.ShapeDtypeStruct((p, na, nl, n2, 2 * SUBLANES, LANES), BF16)
    else:
        out_specs = [zspec, zspec]
        out_shape = [jax.ShapeDtypeStruct(zr.shape, F32)] * 2
    return pl.pallas_call(
        functools.partial(_fft_mid_kernel, n2=n2, ncol=bl, nl=bl, conv=conv),
        grid=(na, nl // bl, p),
        in_specs=in_specs,
        out_specs=out_specs,
        out_shape=out_shape,
        scratch_shapes=_fft_scratch(n2, FFT_MID_COLS),
        compiler_params=_cparams(("arbitrary", "arbitrary", "arbitrary")),
        name="fft_mid_conv" if conv else "fft_mid_spec",
    )(*args)


def _fft_ainv_kernel(z_ref, u_ref, x_ref, d_ref, o_ref, *scratch, n1, nj):
    plan = _fft_plan(n1, _radices(n1), +1)
    d = d_ref[...]

    def one(j, scr):
        temps = [(scr[0], scr[1]), (scr[2], scr[3])]
        lr, li = scr[4], scr[5]
        rows = pl.ds(pl.multiple_of(j * SUBLANES, SUBLANES), SUBLANES)
        for a in range(n1 // SUBLANES):
            sl = slice(a * SUBLANES, (a + 1) * SUBLANES)
            z = z_ref[0, a, 0, rows].astype(F32)
            lr[sl] = jnp.swapaxes(z[:, :SUBLANES], 0, 1)
            li[sl] = jnp.swapaxes(z[:, SUBLANES:], 0, 1)

        def store(i, y):
            if i < n1 // 2:
                o_ref[0, 0, i, rows, :] = x_ref[0, 0, i, rows, :] * (y[0] + u_ref[0, 0, i, rows, :] * d)
                o_ref[1, 0, i, rows, :] = x_ref[1, 0, i, rows, :] * (y[1] + u_ref[1, 0, i, rows, :] * d)

        _run_fft(plan, +1, lambda i: (lr[i], li[i]), store, temps)

    _interleaved_loop(nj, one, scratch)


def _fft_ainv(z, u, u_blk0, x, x_blk0, d, d_row):
    p, na, nl, n2 = z.shape[:4]
    n1 = na * SUBLANES
    bn2 = min(max(FFT_A_ROWS // n1, SUBLANES), n2)
    zspec = pl.BlockSpec((1, na, 1, bn2, 2 * SUBLANES, LANES), lambda pp, lb, j: (pp, 0, lb, j, 0, 0))

    def sspec(blk0):
        return pl.BlockSpec((2, 1, n1 // 2, bn2, LANES), lambda pp, lb, j: (0, pp, 0, j, blk0 + lb))

    return pl.pallas_call(
        functools.partial(_fft_ainv_kernel, n1=n1, nj=bn2 // SUBLANES),
        grid=(p, nl, n2 // bn2),
        in_specs=[zspec, sspec(u_blk0), sspec(x_blk0),
                  pl.BlockSpec((1, LANES), lambda pp, lb, j: (0, lb))],
        out_specs=sspec(0),
        out_shape=jax.ShapeDtypeStruct((2, p, n1 // 2, n2, D_INNER), F32),
        scratch_shapes=_fft_scratch(n1),
        compiler_params=_cparams(("arbitrary", "arbitrary", "arbitrary")),
        name="fft_a_inv",
    )(z, u, x, d[d_row].reshape(1, D_INNER))


def _hyena_layer(x, norm_w, w_in, conv_w, conv_b, fw1, fb1, fw2, fb2, fw3, fb3, ffreq, fwout,
                 bias_d, w_out, final_norm=None):
    b, l, _ = x.shape
    e = D_INNER
    n = 2 * l
    n2 = FFT_N2
    n1 = n // n2
    p = b // 2
    nl = N_LANE_BLOCKS
    x2d = x.reshape(b * l, D_MODEL)
    proj = _norm_matmul_conv(x2d, norm_w, w_in, conv_w, conv_b, l, act=False, dual=False)
    pv = proj.reshape(2, p, n1 // 2, n2, 4 * e)

    tw = _twiddle_table(n1, n2)
    kt = _hyena_two_sided_filters(l, fw1, fb1, fw2, fb2, fw3, fb3, ffreq, fwout)
    kr, ki = _fft_a(kt.reshape(2, n1, n2, e), n1, n2, packed=False)
    kfr, kfi = _fft_mid(kr, ki, _twiddle_table(n1, n2, scale=0.5 / n))

    zr, zi = _fft_a(pv, n1, n2, packed=True)
    zc = _fft_mid(zr, zi, tw, filt=(kfr, kfi, 0))
    z1 = _fft_ainv(zc, pv, 0, pv, nl, bias_d, 0)
    zr, zi = _fft_a(z1, n1, n2, packed=True)
    zc = _fft_mid(zr, zi, tw, filt=(kfr, kfi, 1))
    z2 = _fft_ainv(zc, z1, 0, pv, 2 * nl, bias_d, 1)
    out = _out_matmul([z2.reshape(b * l, e)], proj, 3, w_out, x2d, final_norm)
    return out.reshape(b, l, D_MODEL)


SWA_QBLOCKS = 4


def _swa_kernel(*refs, nb):
    nq = SWA_QBLOCKS
    q_ref = refs[0]
    k_refs = refs[1:nq + 3]
    v_refs = refs[nq + 3:2 * nq + 5]
    cq_ref, sq_ref, c0_ref, s0_ref, c3_ref, s3_ref, sink_ref, o_ref = refs[2 * nq + 5:]
    n = pl.program_id(1)
    lane = lax.broadcasted_iota(jnp.int32, (BLOCK, HEAD_DIM), 1)
    half = ROT_DIM // 2

    def rope(x, c, s):
        sw = jnp.where(lane < half, pltpu.roll(x, HEAD_DIM - half, 1), pltpu.roll(x, half, 1))
        return x * c + sw * s

    cq, sq = cq_ref[...], sq_ref[...]
    blk = lambda t, u: t[u * BLOCK:(u + 1) * BLOCK]
    kcos = [c0_ref[...]] + [blk(cq, u) for u in range(nq)] + [c3_ref[...]]
    ksin = [s0_ref[...]] + [blk(sq, u) for u in range(nq)] + [s3_ref[...]]
    qi = lax.broadcasted_iota(jnp.int32, (BLOCK, 3 * BLOCK), 0)
    kj = lax.broadcasted_iota(jnp.int32, (BLOCK, 3 * BLOCK), 1) - BLOCK
    band = jnp.abs(qi - kj) <= WINDOW
    masks = [band] * nq
    masks[0] = masks[0] & ((kj >= 0) | (n > 0))
    masks[-1] = masks[-1] & ((kj < BLOCK) | (n < nb // nq - 1))
    masks = [jnp.concatenate([mk] * GROUP, axis=0) for mk in masks]
    scale = HEAD_DIM ** -0.5
    for h in range(N_KV_HEADS):
        sl = slice(h * HEAD_DIM, (h + 1) * HEAD_DIM)
        kr = [rope(r[0, :, sl], c, s).astype(BF16) for r, c, s in zip(k_refs, kcos, ksin)]
        vr = [r[0, :, sl].astype(BF16) for r in v_refs]
        for u in range(nq):
            kh = jnp.concatenate(kr[u:u + 3], axis=0)
            vh = jnp.concatenate(vr[u:u + 3], axis=0)
            rows = slice(u * BLOCK, (u + 1) * BLOCK)
            qs = []
            for g in range(GROUP):
                hq = h * GROUP + g
                qs.append(rope(q_ref[0, rows, hq * HEAD_DIM:(hq + 1) * HEAD_DIM], cq[rows], sq[rows]))
            qh = (jnp.concatenate(qs, axis=0) * scale).astype(BF16)
            s = lax.dot_general(qh, kh, (((1,), (1,)), ((), ())), preferred_element_type=F32)
            s = jnp.where(masks[u], s, -jnp.inf)
            sink = jnp.concatenate(
                [jnp.full((BLOCK, 1), 1.0, F32) * sink_ref[h * GROUP + g] for g in range(GROUP)], axis=0)
            m = jnp.maximum(jnp.max(s, axis=-1, keepdims=True), sink)
            pr = jnp.exp(s - m)
            den = jnp.sum(pr, axis=-1, keepdims=True) + jnp.exp(sink - m)
            o = jnp.dot(pr.astype(BF16), vh, preferred_element_type=F32) * (1.0 / den)
            for g in range(GROUP):
                hq = h * GROUP + g
                o_ref[0, rows, hq * HEAD_DIM:(hq + 1) * HEAD_DIM] = o[g * BLOCK:(g + 1) * BLOCK]


def _rope_tables(l):
    half = ROT_DIM // 2
    inv = ROPE_THETA ** (-jnp.arange(half, dtype=F32) / half)
    ang = jnp.arange(l, dtype=F32)[:, None] * inv[None, :]
    cos, sin = jnp.cos(ang), jnp.sin(ang)
    ones = jnp.ones((l, HEAD_DIM - ROT_DIM), F32)
    return (jnp.concatenate([cos, cos, ones], axis=-1),
            jnp.concatenate([-sin, sin, 0.0 * ones], axis=-1))


def _swa_layer(x, norm_w, w_in, sink, w_out, final_norm=None):
    b, l, _ = x.shape
    nb = l // BLOCK
    x2d = x.reshape(b * l, D_MODEL)
    w_perm = jnp.concatenate([w_in[:, :Q_DIM], w_in[:, Q_DIM + 2 * KV_DIM:],
                              w_in[:, Q_DIM:Q_DIM + 2 * KV_DIM]], axis=1)
    proj = _norm_matmul(x2d, norm_w, w_perm)
    p3 = proj.reshape(b, l, 2 * Q_DIM + 2 * KV_DIM)
    ct, st = _rope_tables(l)
    kblk = 2 * Q_DIM // KV_DIM
    nq = SWA_QBLOCKS
    blk = [lambda bb, i, o=o: jnp.clip(nq * i + o, 0, nb - 1) for o in range(-1, nq + 1)]
    kv = lambda col, f: pl.BlockSpec((1, BLOCK, KV_DIM), lambda bb, i: (bb, f(bb, i), col))
    tab = lambda f: pl.BlockSpec((BLOCK, HEAD_DIM), lambda bb, i: (f(bb, i), 0))
    qtab = pl.BlockSpec((nq * BLOCK, HEAD_DIM), lambda bb, i: (i, 0))
    o = pl.pallas_call(
        functools.partial(_swa_kernel, nb=nb),
        grid=(b, nb // nq),
        in_specs=[pl.BlockSpec((1, nq * BLOCK, Q_DIM), lambda bb, i: (bb, i, 0))]
        + [kv(kblk, f) for f in blk] + [kv(kblk + 1, f) for f in blk]
        + [qtab, qtab, tab(blk[0]), tab(blk[0]), tab(blk[-1]), tab(blk[-1]),
           pl.BlockSpec(memory_space=pltpu.SMEM)],
        out_specs=pl.BlockSpec((1, nq * BLOCK, Q_DIM), lambda bb, i: (bb, i, 0)),
        out_shape=jax.ShapeDtypeStruct((b, l, Q_DIM), F32),
        compiler_params=_cparams(("arbitrary", "arbitrary")),
        name="swa_attention",
    )(*([p3] * (2 * nq + 5)), ct, st, ct, st, ct, st, sink)
    out = _out_matmul([o.reshape(b * l, D_INNER)], proj, 1, w_out, x2d, final_norm)
    return out.reshape(b, l, D_MODEL)


def _ml_qkv_kernel(xc_ref, xi_ref, wq_ref, wk_ref, wv_ref, wg_ref, bg_ref, q_ref, k_ref, v_ref,
                   g_ref, gt_ref):
    nt = D_INNER // BD_TILE
    ng = 4 * ML_HEADS
    qs, ks, vs = [], [], []
    for j in range(nt):
        sl = slice(j * BD_TILE, (j + 1) * BD_TILE)
        xc = xc_ref[:, sl]
        qs.append(_dot3(xc, wq_ref[0, j], wq_ref[1, j]))
        ks.append(_dot3(xc, wk_ref[0, j], wk_ref[1, j]))
        vs.append(_dot3(xi_ref[:, sl], wv_ref[0, j], wv_ref[1, j]))
    qkv = jnp.concatenate(qs + ks + vs, axis=-1)
    xh = qkv.astype(BF16)
    xl = (qkv - xh.astype(F32)).astype(BF16)
    both = jnp.dot(xh, wg_ref[...], preferred_element_type=F32)
    low = jnp.dot(xl, wg_ref[:, :ng], preferred_element_type=F32)
    gates = both[:, :ng] + (both[:, ng:] + low) + bg_ref[...]
    g_ref[...] = gates
    gt_ref[...] = gates.T
    e = D_INNER
    q_ref[...] = qkv[:, :e].astype(BF16)
    k_ref[...] = (qkv[:, e:2 * e] * (ML_HEAD_DIM ** -0.5)).astype(BF16)
    v_ref[...] = qkv[:, 2 * e:].astype(BF16)


BD_TILE = 256


def _blockdiag_tiles(w):
    per = BD_TILE // ML_QKV_BLOCK
    wt = w.reshape(D_INNER // BD_TILE, per, ML_QKV_BLOCK, ML_QKV_BLOCK)
    eye = jnp.eye(per, dtype=w.dtype)
    return jnp.einsum("tnio,nm->tnimo", wt, eye).reshape(D_INNER // BD_TILE, BD_TILE, BD_TILE)


def _ml_qkv(xc2d, proj, wq, wk, wv, w_gate, b_gate, tm=256):
    t = xc2d.shape[0]
    tm = min(tm, t)
    e = D_INNER
    ng = 4 * ML_HEADS
    nt = e // BD_TILE
    wspec = pl.BlockSpec((2, nt, BD_TILE, BD_TILE), lambda i: (0, 0, 0, 0))
    act = jax.ShapeDtypeStruct((t, e), BF16)
    tiles = lambda w: _split_hi_lo(_blockdiag_tiles(w))
    wg = _split_hi_lo(w_gate)
    wg = jnp.concatenate([wg[0], wg[1]], axis=1)
    return pl.pallas_call(
        _ml_qkv_kernel,
        grid=(t // tm,),
        in_specs=[
            pl.BlockSpec((tm, e), lambda i: (i, 0)),
            pl.BlockSpec((tm, e), lambda i: (i, 0)),
            wspec, wspec, wspec,
            pl.BlockSpec((3 * e, 2 * ng), lambda i: (0, 0)),
            pl.BlockSpec((1, ng), lambda i: (0, 0)),
        ],
        out_specs=[pl.BlockSpec((tm, e), lambda i: (i, 0))] * 3
        + [pl.BlockSpec((tm, ng), lambda i: (i, 0)), pl.BlockSpec((ng, tm), lambda i: (0, i))],
        out_shape=[act, act, act, jax.ShapeDtypeStruct((t, ng), F32),
                   jax.ShapeDtypeStruct((ng, t), F32)],
        compiler_params=_cparams(("arbitrary",)),
        name="mlstm_qkv_gates",
    )(xc2d, proj, tiles(wq), tiles(wk), tiles(wv), wg, b_gate.reshape(1, ng))


def _log_sigmoid(x):
    return jnp.minimum(x, 0.0) - jnp.log(1.0 + jnp.exp(-jnp.abs(x)))


def _ml_chunk(d, hd, q_ref, k_ref, v_ref, g, gt, o_ref, c_ref, n_ref, m_ref):
    lc = ML_CHUNK
    ng = 4 * ML_HEADS
    dh = ML_HEAD_DIM
    hs = slice(hd * dh, (hd + 1) * dh)
    col_i = 2 * ML_HEADS * d + hd
    col_f = col_i + ML_HEADS
    lane = lax.broadcasted_iota(jnp.int32, (lc, ng), 1)
    i_col = jnp.sum(jnp.where(lane == col_i, g, 0.0), axis=-1, keepdims=True)
    f_col = _log_sigmoid(jnp.sum(jnp.where(lane == col_f, g, 0.0), axis=-1, keepdims=True))
    sub = lax.broadcasted_iota(jnp.int32, (ng, lc), 0)
    i_row = jnp.sum(jnp.where(sub == col_i, gt, 0.0), axis=0, keepdims=True)
    f_row = _log_sigmoid(jnp.sum(jnp.where(sub == col_f, gt, 0.0), axis=0, keepdims=True))

    t_idx = lax.broadcasted_iota(jnp.int32, (lc, lc), 0)
    s_idx = lax.broadcasted_iota(jnp.int32, (lc, lc), 1)
    if d == 0:
        causal = s_idx <= t_idx
        causal_t = t_idx <= s_idx
    else:
        causal = s_idx >= t_idx
        causal_t = t_idx >= s_idx
    b_col = jnp.sum(jnp.where(causal, f_row, 0.0), axis=-1, keepdims=True)
    b_row = jnp.sum(jnp.where(causal_t, f_col, 0.0), axis=0, keepdims=True)
    m_prev = m_ref[d, hd]
    dmat = jnp.where(causal, b_col - b_row + i_row, -jnp.inf)
    inter = b_col + m_prev
    m_out = jnp.maximum(inter, jnp.max(dmat, axis=-1, keepdims=True))
    w_intra = jnp.exp(dmat - m_out)
    w_state = jnp.exp(inter - m_out)

    q = q_ref[:, hs]
    k = k_ref[:, hs]
    v = v_ref[:, hs]
    cst = c_ref[d, hd]
    s = lax.dot_general(q, k, (((1,), (1,)), ((), ())), preferred_element_type=F32) * w_intra
    num = jnp.dot(s.astype(BF16), v, preferred_element_type=F32)
    num = num + w_state * jnp.dot(q, cst.astype(BF16), preferred_element_type=F32)
    qn = jnp.sum(q.astype(F32) * n_ref[d, hd], axis=-1, keepdims=True)
    den = jnp.sum(s, axis=-1, keepdims=True) + w_state * qn
    o_ref[:, hs] = num / jnp.maximum(jnp.abs(den), jnp.exp(-m_out))

    last = lc - 1 if d == 0 else 0
    bl = jnp.sum(f_row, axis=-1, keepdims=True)
    m_new = m_out[last:last + 1]
    wk = jnp.exp(bl - b_col + i_col - m_new)
    decay = jnp.exp(bl + m_prev - m_new)
    vw = (v.astype(F32) * wk).astype(BF16)
    c_ref[d, hd] = decay * cst + lax.dot_general(
        k, vw, (((0,), (0,)), ((), ())), preferred_element_type=F32)
    n_ref[d, hd] = decay * n_ref[d, hd] + jnp.sum(k.astype(F32) * wk, axis=0, keepdims=True)
    m_ref[d, hd] = m_new


def _ml_scan_kernel(qf_ref, kf_ref, vf_ref, gf_ref, gtf_ref, qb_ref, kb_ref, vb_ref, gb_ref, gtb_ref,
                    of_ref, ob_ref, c_ref, n_ref, m_ref):
    @pl.when(pl.program_id(1) == 0)
    def _():
        c_ref[...] = jnp.zeros_like(c_ref)
        n_ref[...] = jnp.zeros_like(n_ref)
        m_ref[...] = jnp.zeros_like(m_ref)

    dirs = ((qf_ref, kf_ref, vf_ref, gf_ref, gtf_ref, of_ref),
            (qb_ref, kb_ref, vb_ref, gb_ref, gtb_ref, ob_ref))
    for d, (q_ref, k_ref, v_ref, g_ref, gt_ref, o_ref) in enumerate(dirs):
        g = g_ref[...]
        gt = gt_ref[...]
        for hd in range(ML_HEADS):
            _ml_chunk(d, hd, q_ref, k_ref, v_ref, g, gt, o_ref, c_ref, n_ref, m_ref)


def _ml_scan(q, k, v, gates, gates_t, b, l):
    t = b * l
    nc = l // ML_CHUNK
    dh = ML_HEAD_DIM
    e = D_INNER
    ng = 4 * ML_HEADS
    fw = lambda bb, c: bb * nc + c
    bw = lambda bb, c: bb * nc + nc - 1 - c

    def specs(ch):
        qspec = pl.BlockSpec((ML_CHUNK, e), lambda bb, c: (ch(bb, c), 0))
        return [qspec, qspec, qspec,
                pl.BlockSpec((ML_CHUNK, ng), lambda bb, c: (ch(bb, c), 0)),
                pl.BlockSpec((ng, ML_CHUNK), lambda bb, c: (0, ch(bb, c)))]

    out = jax.ShapeDtypeStruct((t, e), F32)
    return pl.pallas_call(
        _ml_scan_kernel,
        grid=(b, nc),
        in_specs=specs(fw) + specs(bw),
        out_specs=[pl.BlockSpec((ML_CHUNK, e), lambda bb, c: (fw(bb, c), 0)),
                   pl.BlockSpec((ML_CHUNK, e), lambda bb, c: (bw(bb, c), 0))],
        out_shape=[out, out],
        scratch_shapes=[pltpu.VMEM((2, ML_HEADS, dh, dh), F32), pltpu.VMEM((2, ML_HEADS, 1, dh), F32),
                        pltpu.VMEM((2, ML_HEADS, 1, 1), F32)],
        compiler_params=_cparams(("arbitrary", "arbitrary")),
        name="mlstm_scan",
    )(q, k, v, gates, gates_t, q, k, v, gates, gates_t)


def _mlstm_layer(x, norm_w, w_in, conv_w, conv_b, wq, wk, wv, w_gate, b_gate, head_norm, skip,
                 w_out, final_norm=None):
    b, l, _ = x.shape
    x2d = x.reshape(b * l, D_MODEL)
    proj, xc = _norm_matmul_conv(x2d, norm_w, w_in, conv_w, conv_b, l, act=True, dual=True, tm=1024)
    q, k, v, gates, gates_t = _ml_qkv(xc, proj, wq, wk, wv, w_gate, b_gate)
    h_fwd, h_bwd = _ml_scan(q, k, v, gates, gates_t, b, l)
    out = _out_matmul([h_fwd, h_bwd, xc], proj, 1, w_out, x2d, final_norm,
                      extra=(head_norm, skip))
    return out.reshape(b, l, D_MODEL)


def _trunk(x, layers, final_norm):
    kinds = (_hyena_layer, _swa_layer, _mlstm_layer, _hyena_layer)
    for i, (fn, params) in enumerate(zip(kinds, layers)):
        x = fn(x, *params, final_norm=final_norm if i == len(kinds) - 1 else None)
    return x


def kernel(x_prompt, x_sample, l0_norm, l0_hy_w_in, l0_hy_conv_w, l0_hy_conv_b, l0_hy_filt_w1, l0_hy_filt_b1, l0_hy_filt_w2, l0_hy_filt_b2, l0_hy_filt_w3, l0_hy_filt_b3, l0_hy_filt_freq, l0_hy_filt_wout, l0_hy_bias_d, l0_hy_w_out, l1_norm, l1_swa_w_in, l1_swa_sink, l1_swa_w_out, l2_norm, l2_ml_w_in, l2_ml_conv_w, l2_ml_conv_b, l2_ml_wq, l2_ml_wk, l2_ml_wv, l2_ml_w_gate, l2_ml_b_gate, l2_ml_head_norm, l2_ml_skip, l2_ml_w_out, l3_norm, l3_hy_w_in, l3_hy_conv_w, l3_hy_conv_b, l3_hy_filt_w1, l3_hy_filt_b1, l3_hy_filt_w2, l3_hy_filt_b2, l3_hy_filt_w3, l3_hy_filt_b3, l3_hy_filt_freq, l3_hy_filt_wout, l3_hy_bias_d, l3_hy_w_out, final_norm):
    layers = [
        (l0_norm, l0_hy_w_in, l0_hy_conv_w, l0_hy_conv_b, l0_hy_filt_w1, l0_hy_filt_b1, l0_hy_filt_w2,
         l0_hy_filt_b2, l0_hy_filt_w3, l0_hy_filt_b3, l0_hy_filt_freq, l0_hy_filt_wout, l0_hy_bias_d,
         l0_hy_w_out),
        (l1_norm, l1_swa_w_in, l1_swa_sink, l1_swa_w_out),
        (l2_norm, l2_ml_w_in, l2_ml_conv_w, l2_ml_conv_b, l2_ml_wq, l2_ml_wk, l2_ml_wv, l2_ml_w_gate,
         l2_ml_b_gate, l2_ml_head_norm, l2_ml_skip, l2_ml_w_out),
        (l3_norm, l3_hy_w_in, l3_hy_conv_w, l3_hy_conv_b, l3_hy_filt_w1, l3_hy_filt_b1, l3_hy_filt_w2,
         l3_hy_filt_b2, l3_hy_filt_w3, l3_hy_filt_b3, l3_hy_filt_freq, l3_hy_filt_wout, l3_hy_bias_d,
         l3_hy_w_out),
    ]
    return (_trunk(x_prompt, layers, final_norm), _trunk(x_sample, layers, final_norm))
```

```python
import cmath
import functools
import math

import jax
import jax.numpy as jnp
from jax import lax
from jax.experimental import pallas as pl
from jax.experimental.pallas import tpu as pltpu

F32 = jnp.float32
BF16 = jnp.bfloat16

D_MODEL = 1024
D_INNER = 2048
RMS_EPS = 1e-6
LN_EPS = 1e-6

LANES = 128
SUBLANES = 8
N_LANE_BLOCKS = D_INNER // LANES
VMEM_LIMIT = 56 * 1024 * 1024

HY_EMB = 33
HY_FILT = 64
HY_MIN_DECAY = math.log(1e-2) / 1.5
HY_MAX_DECAY = math.log(1e-2) / 0.3
FFT_N2 = 128
FFT_N2_RADICES = (8, 4, 4)
FFT_A_ROWS = 8192
FFT_MID_COLS = 4

HEAD_DIM = 128
N_Q_HEADS = D_INNER // HEAD_DIM
N_KV_HEADS = 4
GROUP = N_Q_HEADS // N_KV_HEADS
WINDOW = 128
BLOCK = 128
ROT_DIM = HEAD_DIM // 4
ROPE_THETA = 500000.0
Q_DIM = N_Q_HEADS * HEAD_DIM
KV_DIM = N_KV_HEADS * HEAD_DIM

ML_HEADS = 4
ML_HEAD_DIM = D_INNER // ML_HEADS
ML_QKV_BLOCK = 4
ML_CHUNK = 128


def _cparams(sem):
    return pltpu.CompilerParams(dimension_semantics=sem, vmem_limit_bytes=VMEM_LIMIT)


def _silu(x):
    return x * (1.0 / (1.0 + jnp.exp(-x)))


def _split_hi_lo(w):
    hi = w.astype(BF16)
    lo = (w - hi.astype(F32)).astype(BF16)
    return jnp.stack([hi, lo])


def _dot3(x, w_hi, w_lo):
    xh = x.astype(BF16)
    xl = (x - xh.astype(F32)).astype(BF16)
    d = lambda a, b: jnp.dot(a, b, preferred_element_type=F32)
    return d(xh, w_hi) + (d(xl, w_hi) + d(xh, w_lo))


def _add(a, b):
    if a is None:
        return b
    if b is None:
        return a
    return a + b


def _sub(a, b):
    if b is None:
        return a
    if a is None:
        return -b
    return a - b


def _scale(a, c):
    if a is None or c == 0.0:
        return None
    if c == 1.0:
        return a
    if c == -1.0:
        return -a
    return a * c


def _cadd(x, y):
    return (_add(x[0], y[0]), _add(x[1], y[1]))


def _csub(x, y):
    return (_sub(x[0], y[0]), _sub(x[1], y[1]))


def _snap(v):
    for t in (0.0, 1.0, -1.0):
        if abs(v - t) < 1e-14:
            return t
    return v


def _cmul_const(x, c):
    cr, ci = _snap(c.real), _snap(c.imag)
    if cr != 0.0 and abs(abs(cr) - abs(ci)) < 1e-14:
        sr, si = math.copysign(1.0, cr), math.copysign(1.0, ci)
        re = _sub(_scale(x[0], sr), _scale(x[1], si))
        im = _add(_scale(x[0], si), _scale(x[1], sr))
        return (_scale(re, abs(cr)), _scale(im, abs(cr)))
    re = _sub(_scale(x[0], cr), _scale(x[1], ci))
    im = _add(_scale(x[0], ci), _scale(x[1], cr))
    return (re, im)


def _cmul(x, y):
    xr, xi = x
    yr, yi = y
    if xi is None:
        return (xr * yr, xr * yi)
    return (xr * yr - xi * yi, xr * yi + xi * yr)


def _dense(x):
    re, im = x
    if re is None:
        re = jnp.zeros_like(im)
    if im is None:
        im = jnp.zeros_like(re)
    return (re, im)


def _small_dft(xs, sign):
    r = len(xs)
    if r == 1:
        return list(xs)
    ev = _small_dft(xs[0::2], sign)
    od = _small_dft(xs[1::2], sign)
    out = [None] * r
    for k in range(r // 2):
        t = _cmul_const(od[k], cmath.exp(sign * 2j * math.pi * k / r))
        out[k] = _cadd(ev[k], t)
        out[k + r // 2] = _csub(ev[k], t)
    return out


def _fft_plan(n, radices, sign):
    assert math.prod(radices) == n
    passes = []
    s, ncur = 1, n
    for r in radices:
        m = ncur // r
        bfs = []
        for p in range(m):
            tws = [cmath.exp(sign * 2j * math.pi * p * k / ncur) for k in range(r)]
            for q in range(s):
                ins = [q + s * (p + m * j) for j in range(r)]
                outs = [q + s * (r * p + k) for k in range(r)]
                bfs.append((ins, outs, tws))
        passes.append(bfs)
        ncur, s = m, s * r
    return passes


def _run_fft(plan, sign, load, store, temps):
    npass = len(plan)
    for pi, bfs in enumerate(plan):
        if pi == 0:
            ld = load
        else:
            tr, ti = temps[(pi - 1) % 2]
            ld = lambda i, tr=tr, ti=ti: (tr[i], ti[i])
        if pi == npass - 1:
            st = store
        else:
            tr2, ti2 = temps[pi % 2]

            def st(i, x, tr2=tr2, ti2=ti2):
                tr2[i] = x[0]
                ti2[i] = x[1]
        for ins, outs, tws in bfs:
            ys = _small_dft([ld(i) for i in ins], sign)
            for o, y, tw in zip(outs, ys, tws):
                st(o, _dense(_cmul_const(y, tw)))


def _radices(n):
    out = []
    while n > 1:
        r = 8 if n % 8 == 0 else (4 if n % 4 == 0 else 2)
        out.append(r)
        n //= r
    return tuple(out)


def _norm_matmul_kernel(x_ref, nw_ref, w_ref, o_ref, xn_ref):
    @pl.when(pl.program_id(1) == 0)
    def _():
        x = x_ref[...]
        inv = lax.rsqrt(jnp.mean(x * x, axis=-1, keepdims=True) + RMS_EPS)
        xn_ref[...] = ((x * inv) * nw_ref[...]).astype(BF16)

    o_ref[...] = jnp.dot(xn_ref[...], w_ref[...], preferred_element_type=F32)


def _norm_matmul(x2d, norm_w, w, tm=2048, tn=1024):
    t, d = x2d.shape
    n = w.shape[1]
    tm = min(tm, t)
    tn = min(tn, n)
    return pl.pallas_call(
        _norm_matmul_kernel,
        grid=(t // tm, n // tn),
        in_specs=[
            pl.BlockSpec((tm, d), lambda i, j: (i, 0)),
            pl.BlockSpec((1, d), lambda i, j: (0, 0)),
            pl.BlockSpec((d, tn), lambda i, j: (0, j)),
        ],
        out_specs=pl.BlockSpec((tm, tn), lambda i, j: (i, j)),
        out_shape=jax.ShapeDtypeStruct((t, n), F32),
        scratch_shapes=[pltpu.VMEM((tm, d), BF16)],
        compiler_params=_cparams(("arbitrary", "arbitrary")),
        name="norm_matmul",
    )(x2d, norm_w.reshape(1, d), w.astype(BF16))


HALO = 16


def _norm_matmul_conv_kernel(x_ref, xp_ref, xn_ref, nw_ref, w_ref, cw_ref, cb_ref, *rest,
                             tm, nconv, seq_tiles, act, dual):
    if dual:
        o_ref, oc_ref, xs_ref = rest
    else:
        o_ref, xs_ref = rest
    i = pl.program_id(0)
    j = pl.program_id(1)

    @pl.when(j == 0)
    def _():
        def nrm(x):
            inv = lax.rsqrt(jnp.mean(x * x, axis=-1, keepdims=True) + RMS_EPS)
            return (x * inv) * nw_ref[...]

        first = (i % seq_tiles) == 0
        last = (i % seq_tiles) == seq_tiles - 1
        xs_ref[0:HALO] = jnp.where(first, 0.0, nrm(xp_ref[...])).astype(BF16)
        xs_ref[HALO:HALO + tm] = nrm(x_ref[...]).astype(BF16)
        xs_ref[HALO + tm:] = jnp.where(last, 0.0, nrm(xn_ref[...])).astype(BF16)

    @pl.when(j < nconv)
    def _():
        rows = tm + 2 * HALO
        tn = w_ref.shape[1]
        nsplit = 4 if tn % (8 * LANES) == 0 else 1
        for c in range(nsplit):
            cs = slice(c * tn // nsplit, (c + 1) * tn // nsplit)
            acc = jnp.dot(xs_ref[...], w_ref[:, cs], preferred_element_type=F32)
            y = (acc * cw_ref[1:2, cs] + pltpu.roll(acc, 1, 0) * cw_ref[0:1, cs]
                 + pltpu.roll(acc, rows - 1, 0) * cw_ref[2:3, cs] + cb_ref[:, cs])
            y = y[HALO:HALO + tm]
            if act:
                y = _silu(y)
            if dual:
                oc_ref[:, cs] = y
                o_ref[:, cs] = acc[HALO:HALO + tm]
            else:
                o_ref[:, cs] = y

    @pl.when(j >= nconv)
    def _():
        o_ref[...] = jnp.dot(xs_ref[HALO:HALO + tm], w_ref[...], preferred_element_type=F32)


def _norm_matmul_conv(x2d, norm_w, w, conv_w, conv_b, seq_len, act, dual, tm=2048, tn=1024):
    t, d = x2d.shape
    n = w.shape[1]
    nc = conv_w.shape[1]
    tm = min(tm, seq_len)
    tn = min(tn, nc)
    nconv = nc // tn
    hb = tm // HALO
    nhb = t // HALO
    cj = lambda j: jnp.minimum(j, nconv - 1)
    in_specs = [
        pl.BlockSpec((tm, d), lambda i, j: (i, 0)),
        pl.BlockSpec((HALO, d), lambda i, j: (jnp.maximum(i * hb - 1, 0), 0)),
        pl.BlockSpec((HALO, d), lambda i, j: (jnp.minimum((i + 1) * hb, nhb - 1), 0)),
        pl.BlockSpec((1, d), lambda i, j: (0, 0)),
        pl.BlockSpec((d, tn), lambda i, j: (0, j)),
        pl.BlockSpec((3, tn), lambda i, j: (0, cj(j))),
        pl.BlockSpec((1, tn), lambda i, j: (0, cj(j))),
    ]
    out_specs = pl.BlockSpec((tm, tn), lambda i, j: (i, j))
    out_shape = jax.ShapeDtypeStruct((t, n), F32)
    if dual:
        out_specs = [out_specs, pl.BlockSpec((tm, tn), lambda i, j: (i, cj(j)))]
        out_shape = [out_shape, jax.ShapeDtypeStruct((t, nc), F32)]
    return pl.pallas_call(
        functools.partial(_norm_matmul_conv_kernel, tm=tm, nconv=nconv, seq_tiles=seq_len // tm,
                          act=act, dual=dual),
        grid=(t // tm, n // tn),
        in_specs=in_specs,
        out_specs=out_specs,
        out_shape=out_shape,
        scratch_shapes=[pltpu.VMEM((tm + 2 * HALO, d), BF16)],
        compiler_params=_cparams(("arbitrary", "arbitrary")),
        name="norm_matmul_conv",
    )(x2d, x2d, x2d, norm_w.reshape(1, d), w.astype(BF16), conv_w, conv_b.reshape(1, nc))


def _out_kernel(*refs, mode, final):
    if mode == "mlstm":
        hf_ref, hb_ref, xc_ref, g_ref, hn_ref, sk_ref, w_ref, r_ref = refs[:8]
        rest = refs[8:]
        h = hf_ref[...] + hb_ref[...]
        parts = []
        for hd in range(ML_HEADS):
            seg = h[:, hd * ML_HEAD_DIM:(hd + 1) * ML_HEAD_DIM]
            mu = jnp.mean(seg, axis=-1, keepdims=True)
            cen = seg - mu
            var = jnp.mean(cen * cen, axis=-1, keepdims=True)
            parts.append(cen * lax.rsqrt(var + LN_EPS))
        a = jnp.concatenate(parts, axis=-1) * hn_ref[...] + sk_ref[...] * xc_ref[...]
    else:
        a_ref, g_ref, w_ref, r_ref = refs[:4]
        rest = refs[4:]
        a = a_ref[...]
    if final:
        fn_ref, o_ref = rest
    else:
        (o_ref,) = rest
    y = jnp.dot((a * _silu(g_ref[...])).astype(BF16), w_ref[...], preferred_element_type=F32)
    x = r_ref[...] + y
    if final:
        inv = lax.rsqrt(jnp.mean(x * x, axis=-1, keepdims=True) + RMS_EPS)
        x = (x * inv) * fn_ref[...]
    o_ref[...] = x


def _out_matmul(acts, gate_arr, gate_blk, w, resid, final_norm=None, extra=(), tm=512):
    t = resid.shape[0]
    tm = min(tm, t)
    e = D_INNER
    mode = "mlstm" if len(acts) == 3 else "plain"
    in_specs, args = [], []
    for a in acts:
        if isinstance(a, tuple):
            arr, idx = a
            in_specs.append(pl.BlockSpec((None, tm, e), lambda i, idx=idx: (idx, i, 0)))
            args.append(arr)
        else:
            in_specs.append(pl.BlockSpec((tm, e), lambda i: (i, 0)))
            args.append(a)
    in_specs.append(pl.BlockSpec((tm, e), lambda i: (i, gate_blk)))
    args.append(gate_arr)
    for v in extra:
        in_specs.append(pl.BlockSpec((1, e), lambda i: (0, 0)))
        args.append(v.reshape(1, e))
    in_specs.append(pl.BlockSpec((e, D_MODEL), lambda i: (0, 0)))
    args.append(w.astype(BF16))
    in_specs.append(pl.BlockSpec((tm, D_MODEL), lambda i: (i, 0)))
    args.append(resid)
    if final_norm is not None:
        in_specs.append(pl.BlockSpec((1, D_MODEL), lambda i: (0, 0)))
        args.append(final_norm.reshape(1, D_MODEL))
    return pl.pallas_call(
        functools.partial(_out_kernel, mode=mode, final=final_norm is not None),
        grid=(t // tm,),
        in_specs=in_specs,
        out_specs=pl.BlockSpec((tm, D_MODEL), lambda i: (i, 0)),
        out_shape=jax.ShapeDtypeStruct((t, D_MODEL), F32),
        compiler_params=_cparams(("arbitrary",)),
        name="out_matmul_" + mode,
    )(*args)


def _filter_kernel(feat_ref, t_ref, w1_ref, b1_ref, w2_ref, b2_ref, w3_ref, b3_ref, fr_ref,
                   wo_ref, dl_ref, o_ref):
    fr = fr_ref[...]
    h = jnp.sin(fr * (_dot3(feat_ref[...], w1_ref[0], w1_ref[1]) + b1_ref[...]))
    h = jnp.sin(fr * (_dot3(h, w2_ref[0], w2_ref[1]) + b2_ref[...]))
    h = jnp.sin(fr * (_dot3(h, w3_ref[0], w3_ref[1]) + b3_ref[...]))
    tcol = t_ref[...]
    decay = jnp.exp(-tcol[:, 0:1] * dl_ref[...]) * tcol[:, 1:2]
    for f in range(2):
        k = _dot3(h, wo_ref[0, f], wo_ref[1, f])
        o_ref[f] = k * decay


def _hyena_two_sided_filters(l, fw1, fb1, fw2, fb2, fw3, fb3, ffreq, fwout, tr=1024):
    n = 2 * l
    e = D_INNER
    tr = min(tr, l)
    idx = jnp.arange(n)
    src = jnp.where(idx < l, idx, n - idx)
    valid = (idx != l).astype(F32)
    src = jnp.where(idx == l, 0, src)
    bands = (HY_EMB - 1) // 2
    t_src = src.astype(F32) / (l - 1)
    ang = 2.0 * math.pi * src.astype(F32)[:, None] / l
    fb = jnp.linspace(1e-4, bands - 1, bands, dtype=F32)[None, :]
    feat = jnp.concatenate([t_src[:, None], jnp.cos(fb * ang), -jnp.sin(fb * ang)], axis=-1)
    feat = jnp.pad(feat, ((0, 0), (0, LANES - HY_EMB)))
    tcol = jnp.stack([t_src, valid], axis=-1)
    w1p = jnp.pad(fw1, ((0, LANES - HY_EMB), (0, 0)))
    deltas = jnp.abs(jnp.linspace(HY_MIN_DECAY, HY_MAX_DECAY, e, dtype=F32)).reshape(1, e)
    wo = _split_hi_lo(fwout.reshape(HY_FILT, 4, e).transpose(1, 0, 2))
    nhalf = l // tr
    row = lambda v: v.reshape(1, HY_FILT)
    full = lambda shp: pl.BlockSpec(shp, lambda i: tuple(0 for _ in shp))
    return pl.pallas_call(
        _filter_kernel,
        grid=(n // tr,),
        in_specs=[
            pl.BlockSpec((tr, LANES), lambda i: (i, 0)),
            pl.BlockSpec((tr, 2), lambda i: (i, 0)),
            full((2, LANES, HY_FILT)), full((1, HY_FILT)),
            full((2, HY_FILT, HY_FILT)), full((1, HY_FILT)),
            full((2, HY_FILT, HY_FILT)), full((1, HY_FILT)),
            full((1, HY_FILT)),
            pl.BlockSpec((2, 2, HY_FILT, e), lambda i: (0, i // nhalf, 0, 0)),
            full((1, e)),
        ],
        out_specs=pl.BlockSpec((2, tr, e), lambda i: (0, i, 0)),
        out_shape=jax.ShapeDtypeStruct((2, n, e), F32),
        compiler_params=_cparams(("arbitrary",)),
        name="hyena_filter",
    )(feat, tcol, _split_hi_lo(w1p), row(fb1), _split_hi_lo(fw2), row(fb2), _split_hi_lo(fw3),
      row(fb3), row(ffreq), wo, deltas)


FFT_SCRATCH = 6
FFT_INTERLEAVE = 2


def _interleaved_loop(n, one, scratch):
    u = len(scratch) // FFT_SCRATCH
    while n % u:
        u //= 2

    def body(c, carry):
        for k in range(u):
            one(c * u + k, scratch[k * FFT_SCRATCH:(k + 1) * FFT_SCRATCH])
        return carry

    lax.fori_loop(0, n // u, body, 0)


def _fft_scratch(n, interleave=FFT_INTERLEAVE):
    return [pltpu.VMEM((n, SUBLANES, LANES), F32) for _ in range(FFT_SCRATCH * interleave)]


def _fft_a_kernel(x_ref, zr_ref, zi_ref, *scratch, n1, nj, packed):
    plan = _fft_plan(n1, _radices(n1), -1)

    def one(j, scr):
        temps = [(scr[0], scr[1]), (scr[2], scr[3])]
        fr, fi = scr[4], scr[5]
        rows = pl.ds(pl.multiple_of(j * SUBLANES, SUBLANES), SUBLANES)

        def load(i):
            if packed:
                if i >= n1 // 2:
                    return (None, None)
                return (x_ref[0, 0, i, rows, :], x_ref[1, 0, i, rows, :])
            return (x_ref[0, i, rows, :], x_ref[1, i, rows, :])

        def store(i, x):
            fr[i] = x[0]
            fi[i] = x[1]

        _run_fft(plan, -1, load, store, temps)
        if packed:
            planes = ((0, fr, fi),)
        else:
            ar, ai, br, bi = scr[0], scr[1], scr[2], scr[3]
            for k in range(n1):
                m = (n1 - k) % n1
                ar[k] = fr[k] + fr[m]
                ai[k] = fi[k] - fi[m]
                br[k] = fi[k] + fi[m]
                bi[k] = fr[m] - fr[k]
            planes = ((0, ar, ai), (1, br, bi))
        for f, pr, pi in planes:
            for a in range(n1 // SUBLANES):
                sl = slice(a * SUBLANES, (a + 1) * SUBLANES)
                zr_ref[f, a, 0, rows] = jnp.swapaxes(pr[sl], 0, 1)
                zi_ref[f, a, 0, rows] = jnp.swapaxes(pi[sl], 0, 1)

    _interleaved_loop(nj, one, scratch)


def _fft_a(x, n1, n2, packed, lane_blk0=0):
    nl = N_LANE_BLOCKS
    if packed:
        p = x.shape[1]
        bn2 = min(max(FFT_A_ROWS // n1, SUBLANES), n2)
        in_spec = pl.BlockSpec((2, 1, n1 // 2, bn2, LANES),
                               lambda pp, lb, j: (0, pp, 0, j, lane_blk0 + lb))
        nf = 1
    else:
        p = 1
        bn2 = min(max(FFT_A_ROWS // (2 * n1), SUBLANES), n2)
        in_spec = pl.BlockSpec((2, n1, bn2, LANES), lambda pp, lb, j: (0, 0, j, lb))
        nf = 2
    out_spec = pl.BlockSpec((nf, n1 // SUBLANES, 1, bn2, SUBLANES, LANES),
                            lambda pp, lb, j: (pp, 0, lb, j, 0, 0))
    shp = jax.ShapeDtypeStruct((p * nf, n1 // SUBLANES, nl, n2, SUBLANES, LANES), F32)
    return pl.pallas_call(
        functools.partial(_fft_a_kernel, n1=n1, nj=bn2 // SUBLANES, packed=packed),
        grid=(p, nl, n2 // bn2),
        in_specs=[in_spec],
        out_specs=[out_spec, out_spec],
        out_shape=[shp, shp],
        scratch_shapes=_fft_scratch(n1),
        compiler_params=_cparams(("arbitrary", "arbitrary", "arbitrary")),
        name="fft_a_packed" if packed else "fft_a_pair",
    )(x)


def _fft_mid_kernel(*refs, n2, ncol, nl, conv):
    if conv:
        zr_ref, zi_ref, kr_ref, ki_ref, twr_ref, twi_ref, or_ref = refs[:7]
        scratch = refs[7:]
    else:
        zr_ref, zi_ref, twr_ref, twi_ref, or_ref, oi_ref = refs[:6]
        scratch = refs[6:]
    fwd = _fft_plan(n2, FFT_N2_RADICES, -1)
    inv = _fft_plan(n2, FFT_N2_RADICES, +1)

    def one(c, scr):
        temps = [(scr[0], scr[1]), (scr[2], scr[3])]
        sr, si = scr[4], scr[5]
        i = c // nl
        g = c % nl

        def tw(r):
            return (twr_ref[i, r], twi_ref[i, r])

        def load(r):
            return _cmul((zr_ref[0, i, g, r], zi_ref[0, i, g, r]), tw(r))

        if conv:
            def store_spec(r, x):
                y = _cmul(x, (kr_ref[0, i, g, r], ki_ref[0, i, g, r]))
                sr[r] = y[0]
                si[r] = y[1]

            _run_fft(fwd, -1, load, store_spec, temps)

            def store_out(r, x):
                tr, ti = tw(r)
                yr = x[0] * tr + x[1] * ti
                yi = x[1] * tr - x[0] * ti
                or_ref[0, i, g, r] = jnp.concatenate([yr, yi], axis=0).astype(BF16)

            _run_fft(inv, +1, lambda r: (sr[r], si[r]), store_out, temps)
        else:
            def store_spec(r, x):
                or_ref[0, i, g, r] = x[0]
                oi_ref[0, i, g, r] = x[1]

            _run_fft(fwd, -1, load, store_spec, temps)

    _interleaved_loop(ncol, one, scratch)


def _twiddle_table(n1, n2, scale=1.0):
    n = n1 * n2
    k1 = jnp.arange(n1).reshape(n1 // SUBLANES, 1, SUBLANES)
    m = (k1 * jnp.arange(n2).reshape(1, n2, 1)) % n
    ang = m.astype(F32) * (-2.0 * math.pi / n)
    shape = (n1 // SUBLANES, n2, SUBLANES, LANES)
    return (jnp.broadcast_to((scale * jnp.cos(ang))[..., None], shape),
            jnp.broadcast_to((scale * jnp.sin(ang))[..., None], shape))


def _fft_mid(zr, zi, tw, filt=None):
    p, na, nl, n2 = zr.shape[:4]
    conv = filt is not None
    bl = min(FFT_MID_COLS if conv else 2 * FFT_MID_COLS, nl)
    zspec = pl.BlockSpec((1, 1, bl, n2, SUBLANES, LANES), lambda i, g, pp: (pp, i, g, 0, 0, 0))
    twspec = pl.BlockSpec((1, n2, SUBLANES, LANES), lambda i, g, pp: (i, 0, 0, 0))
    in_specs = [zspec, zspec]
    args = [zr, zi]
    if conv:
        fidx = filt[2]
        kspec = pl.BlockSpec((1, 1, bl, n2, SUBLANES, LANES), lambda i, g, pp: (fidx, i, g, 0, 0, 0))
        in_specs += [kspec, kspec]
        args += [filt[0], filt[1]]
    in_specs += [twspec, twspec]
    args += [tw[0], tw[1]]
    if conv:
        out_specs = pl.BlockSpec((1, 1, bl, n2, 2 * SUBLANES, LANES), lambda i, g, pp: (pp, i, g, 0, 0, 0))
        out_shape = jax.ShapeDtypeStruct((p, na, nl, n2, 2 * SUBLANES, LANES), BF16)
    else:
        out_specs = [zspec, zspec]
        out_shape = [jax.ShapeDtypeStruct(zr.shape, F32)] * 2
    return pl.pallas_call(
        functools.partial(_fft_mid_kernel, n2=n2, ncol=bl, nl=bl, conv=conv),
        grid=(na, nl // bl, p),
        in_specs=in_specs,
        out_specs=out_specs,
        out_shape=out_shape,
        scratch_shapes=_fft_scratch(n2, FFT_MID_COLS),
        compiler_params=_cparams(("arbitrary", "arbitrary", "arbitrary")),
        name="fft_mid_conv" if conv else "fft_mid_spec",
    )(*args)


def _fft_ainv_kernel(z_ref, u_ref, x_ref, d_ref, o_ref, *scratch, n1, nj):
    plan = _fft_plan(n1, _radices(n1), +1)
    d = d_ref[...]

    def one(j, scr):
        temps = [(scr[0], scr[1]), (scr[2], scr[3])]
        lr, li = scr[4], scr[5]
        rows = pl.ds(pl.multiple_of(j * SUBLANES, SUBLANES), SUBLANES)
        for a in range(n1 // SUBLANES):
            sl = slice(a * SUBLANES, (a + 1) * SUBLANES)
            z = z_ref[0, a, 0, rows].astype(F32)
            lr[sl] = jnp.swapaxes(z[:, :SUBLANES], 0, 1)
            li[sl] = jnp.swapaxes(z[:, SUBLANES:], 0, 1)

        def store(i, y):
            if i < n1 // 2:
                o_ref[0, 0, i, rows, :] = x_ref[0, 0, i, rows, :] * (y[0] + u_ref[0, 0, i, rows, :] * d)
                o_ref[1, 0, i, rows, :] = x_ref[1, 0, i, rows, :] * (y[1] + u_ref[1, 0, i, rows, :] * d)

        _run_fft(plan, +1, lambda i: (lr[i], li[i]), store, temps)

    _interleaved_loop(nj, one, scratch)


def _fft_ainv(z, u, u_blk0, x, x_blk0, d, d_row):
    p, na, nl, n2 = z.shape[:4]
    n1 = na * SUBLANES
    bn2 = min(max(FFT_A_ROWS // n1, SUBLANES), n2)
    zspec = pl.BlockSpec((1, na, 1, bn2, 2 * SUBLANES, LANES), lambda pp, lb, j: (pp, 0, lb, j, 0, 0))

    def sspec(blk0):
        return pl.BlockSpec((2, 1, n1 // 2, bn2, LANES), lambda pp, lb, j: (0, pp, 0, j, blk0 + lb))

    return pl.pallas_call(
        functools.partial(_fft_ainv_kernel, n1=n1, nj=bn2 // SUBLANES),
        grid=(p, nl, n2 // bn2),
        in_specs=[zspec, sspec(u_blk0), sspec(x_blk0),
                  pl.BlockSpec((1, LANES), lambda pp, lb, j: (0, lb))],
        out_specs=sspec(0),
        out_shape=jax.ShapeDtypeStruct((2, p, n1 // 2, n2, D_INNER), F32),
        scratch_shapes=_fft_scratch(n1),
        compiler_params=_cparams(("arbitrary", "arbitrary", "arbitrary")),
        name="fft_a_inv",
    )(z, u, x, d[d_row].reshape(1, D_INNER))


def _hyena_layer(x, norm_w, w_in, conv_w, conv_b, fw1, fb1, fw2, fb2, fw3, fb3, ffreq, fwout,
                 bias_d, w_out, final_norm=None):
    b, l, _ = x.shape
    e = D_INNER
    n = 2 * l
    n2 = FFT_N2
    n1 = n // n2
    p = b // 2
    nl = N_LANE_BLOCKS
    x2d = x.reshape(b * l, D_MODEL)
    proj = _norm_matmul_conv(x2d, norm_w, w_in, conv_w, conv_b, l, act=False, dual=False)
    pv = proj.reshape(2, p, n1 // 2, n2, 4 * e)

    tw = _twiddle_table(n1, n2)
    kt = _hyena_two_sided_filters(l, fw1, fb1, fw2, fb2, fw3, fb3, ffreq, fwout)
    kr, ki = _fft_a(kt.reshape(2, n1, n2, e), n1, n2, packed=False)
    kfr, kfi = _fft_mid(kr, ki, _twiddle_table(n1, n2, scale=0.5 / n))

    zr, zi = _fft_a(pv, n1, n2, packed=True)
    zc = _fft_mid(zr, zi, tw, filt=(kfr, kfi, 0))
    z1 = _fft_ainv(zc, pv, 0, pv, nl, bias_d, 0)
    zr, zi = _fft_a(z1, n1, n2, packed=True)
    zc = _fft_mid(zr, zi, tw, filt=(kfr, kfi, 1))
    z2 = _fft_ainv(zc, z1, 0, pv, 2 * nl, bias_d, 1)
    out = _out_matmul([z2.reshape(b * l, e)], proj, 3, w_out, x2d, final_norm)
    return out.reshape(b, l, D_MODEL)


SWA_QBLOCKS = 4


def _swa_kernel(*refs, nb):
    nq = SWA_QBLOCKS
    q_ref = refs[0]
    k_refs = refs[1:nq + 3]
    v_refs = refs[nq + 3:2 * nq + 5]
    cq_ref, sq_ref, c0_ref, s0_ref, c3_ref, s3_ref, sink_ref, o_ref = refs[2 * nq + 5:]
    n = pl.program_id(1)
    lane = lax.broadcasted_iota(jnp.int32, (BLOCK, HEAD_DIM), 1)
    half = ROT_DIM // 2

    def rope(x, c, s):
        sw = jnp.where(lane < half, pltpu.roll(x, HEAD_DIM - half, 1), pltpu.roll(x, half, 1))
        return x * c + sw * s

    cq, sq = cq_ref[...], sq_ref[...]
    blk = lambda t, u: t[u * BLOCK:(u + 1) * BLOCK]
    kcos = [c0_ref[...]] + [blk(cq, u) for u in range(nq)] + [c3_ref[...]]
    ksin = [s0_ref[...]] + [blk(sq, u) for u in range(nq)] + [s3_ref[...]]
    qi = lax.broadcasted_iota(jnp.int32, (BLOCK, 3 * BLOCK), 0)
    kj = lax.broadcasted_iota(jnp.int32, (BLOCK, 3 * BLOCK), 1) - BLOCK
    band = jnp.abs(qi - kj) <= WINDOW
    masks = [band] * nq
    masks[0] = masks[0] & ((kj >= 0) | (n > 0))
    masks[-1] = masks[-1] & ((kj < BLOCK) | (n < nb // nq - 1))
    masks = [jnp.concatenate([mk] * GROUP, axis=0) for mk in masks]
    scale = HEAD_DIM ** -0.5
    for h in range(N_KV_HEADS):
        sl = slice(h * HEAD_DIM, (h + 1) * HEAD_DIM)
        kr = [rope(r[0, :, sl], c, s).astype(BF16) for r, c, s in zip(k_refs, kcos, ksin)]
        vr = [r[0, :, sl].astype(BF16) for r in v_refs]
        for u in range(nq):
            kh = jnp.concatenate(kr[u:u + 3], axis=0)
            vh = jnp.concatenate(vr[u:u + 3], axis=0)
            rows = slice(u * BLOCK, (u + 1) * BLOCK)
            qs = []
            for g in range(GROUP):
                hq = h * GROUP + g
                qs.append(rope(q_ref[0, rows, hq * HEAD_DIM:(hq + 1) * HEAD_DIM], cq[rows], sq[rows]))
            qh = (jnp.concatenate(qs, axis=0) * scale).astype(BF16)
            s = lax.dot_general(qh, kh, (((1,), (1,)), ((), ())), preferred_element_type=F32)
            s = jnp.where(masks[u], s, -jnp.inf)
            sink = jnp.concatenate(
                [jnp.full((BLOCK, 1), 1.0, F32) * sink_ref[h * GROUP + g] for g in range(GROUP)], axis=0)
            m = jnp.maximum(jnp.max(s, axis=-1, keepdims=True), sink)
            pr = jnp.exp(s - m)
            den = jnp.sum(pr, axis=-1, keepdims=True) + jnp.exp(sink - m)
            o = jnp.dot(pr.astype(BF16), vh, preferred_element_type=F32) * (1.0 / den)
            for g in range(GROUP):
                hq = h * GROUP + g
                o_ref[0, rows, hq * HEAD_DIM:(hq + 1) * HEAD_DIM] = o[g * BLOCK:(g + 1) * BLOCK]


def _rope_tables(l):
    half = ROT_DIM // 2
    inv = ROPE_THETA ** (-jnp.arange(half, dtype=F32) / half)
    ang = jnp.arange(l, dtype=F32)[:, None] * inv[None, :]
    cos, sin = jnp.cos(ang), jnp.sin(ang)
    ones = jnp.ones((l, HEAD_DIM - ROT_DIM), F32)
    return (jnp.concatenate([cos, cos, ones], axis=-1),
            jnp.concatenate([-sin, sin, 0.0 * ones], axis=-1))


def _swa_layer(x, norm_w, w_in, sink, w_out, final_norm=None):
    b, l, _ = x.shape
    nb = l // BLOCK
    x2d = x.reshape(b * l, D_MODEL)
    w_perm = jnp.concatenate([w_in[:, :Q_DIM], w_in[:, Q_DIM + 2 * KV_DIM:],
                              w_in[:, Q_DIM:Q_DIM + 2 * KV_DIM]], axis=1)
    proj = _norm_matmul(x2d, norm_w, w_perm)
    p3 = proj.reshape(b, l, 2 * Q_DIM + 2 * KV_DIM)
    ct, st = _rope_tables(l)
    kblk = 2 * Q_DIM // KV_DIM
    nq = SWA_QBLOCKS
    blk = [lambda bb, i, o=o: jnp.clip(nq * i + o, 0, nb - 1) for o in range(-1, nq + 1)]
    kv = lambda col, f: pl.BlockSpec((1, BLOCK, KV_DIM), lambda bb, i: (bb, f(bb, i), col))
    tab = lambda f: pl.BlockSpec((BLOCK, HEAD_DIM), lambda bb, i: (f(bb, i), 0))
    qtab = pl.BlockSpec((nq * BLOCK, HEAD_DIM), lambda bb, i: (i, 0))
    o = pl.pallas_call(
        functools.partial(_swa_kernel, nb=nb),
        grid=(b, nb // nq),
        in_specs=[pl.BlockSpec((1, nq * BLOCK, Q_DIM), lambda bb, i: (bb, i, 0))]
        + [kv(kblk, f) for f in blk] + [kv(kblk + 1, f) for f in blk]
        + [qtab, qtab, tab(blk[0]), tab(blk[0]), tab(blk[-1]), tab(blk[-1]),
           pl.BlockSpec(memory_space=pltpu.SMEM)],
        out_specs=pl.BlockSpec((1, nq * BLOCK, Q_DIM), lambda bb, i: (bb, i, 0)),
        out_shape=jax.ShapeDtypeStruct((b, l, Q_DIM), F32),
        compiler_params=_cparams(("arbitrary", "arbitrary")),
        name="swa_attention",
    )(*([p3] * (2 * nq + 5)), ct, st, ct, st, ct, st, sink)
    out = _out_matmul([o.reshape(b * l, D_INNER)], proj, 1, w_out, x2d, final_norm)
    return out.reshape(b, l, D_MODEL)


def _ml_qkv_kernel(xc_ref, xi_ref, wq_ref, wk_ref, wv_ref, wg_ref, bg_ref, q_ref, k_ref, v_ref,
                   g_ref, gt_ref):
    nt = D_INNER // BD_TILE
    ng = 4 * ML_HEADS
    qs, ks, vs = [], [], []
    for j in range(nt):
        sl = slice(j * BD_TILE, (j + 1) * BD_TILE)
        xc = xc_ref[:, sl]
        qs.append(_dot3(xc, wq_ref[0, j], wq_ref[1, j]))
        ks.append(_dot3(xc, wk_ref[0, j], wk_ref[1, j]))
        vs.append(_dot3(xi_ref[:, sl], wv_ref[0, j], wv_ref[1, j]))
    qkv = jnp.concatenate(qs + ks + vs, axis=-1)
    xh = qkv.astype(BF16)
    xl = (qkv - xh.astype(F32)).astype(BF16)
    both = jnp.dot(xh, wg_ref[...], preferred_element_type=F32)
    low = jnp.dot(xl, wg_ref[:, :ng], preferred_element_type=F32)
    gates = both[:, :ng] + (both[:, ng:] + low) + bg_ref[...]
    g_ref[...] = gates
    gt_ref[...] = gates.T
    e = D_INNER
    q_ref[...] = qkv[:, :e].astype(BF16)
    k_ref[...] = (qkv[:, e:2 * e] * (ML_HEAD_DIM ** -0.5)).astype(BF16)
    v_ref[...] = qkv[:, 2 * e:].astype(BF16)


BD_TILE = 256


def _blockdiag_tiles(w):
    per = BD_TILE // ML_QKV_BLOCK
    wt = w.reshape(D_INNER // BD_TILE, per, ML_QKV_BLOCK, ML_QKV_BLOCK)
    eye = jnp.eye(per, dtype=w.dtype)
    return jnp.einsum("tnio,nm->tnimo", wt, eye).reshape(D_INNER // BD_TILE, BD_TILE, BD_TILE)


def _ml_qkv(xc2d, proj, wq, wk, wv, w_gate, b_gate, tm=256):
    t = xc2d.shape[0]
    tm = min(tm, t)
    e = D_INNER
    ng = 4 * ML_HEADS
    nt = e // BD_TILE
    wspec = pl.BlockSpec((2, nt, BD_TILE, BD_TILE), lambda i: (0, 0, 0, 0))
    act = jax.ShapeDtypeStruct((t, e), BF16)
    tiles = lambda w: _split_hi_lo(_blockdiag_tiles(w))
    wg = _split_hi_lo(w_gate)
    wg = jnp.concatenate([wg[0], wg[1]], axis=1)
    return pl.pallas_call(
        _ml_qkv_kernel,
        grid=(t // tm,),
        in_specs=[
            pl.BlockSpec((tm, e), lambda i: (i, 0)),
            pl.BlockSpec((tm, e), lambda i: (i, 0)),
            wspec, wspec, wspec,
            pl.BlockSpec((3 * e, 2 * ng), lambda i: (0, 0)),
            pl.BlockSpec((1, ng), lambda i: (0, 0)),
        ],
        out_specs=[pl.BlockSpec((tm, e), lambda i: (i, 0))] * 3
        + [pl.BlockSpec((tm, ng), lambda i: (i, 0)), pl.BlockSpec((ng, tm), lambda i: (0, i))],
        out_shape=[act, act, act, jax.ShapeDtypeStruct((t, ng), F32),
                   jax.ShapeDtypeStruct((ng, t), F32)],
        compiler_params=_cparams(("arbitrary",)),
        name="mlstm_qkv_gates",
    )(xc2d, proj, tiles(wq), tiles(wk), tiles(wv), wg, b_gate.reshape(1, ng))


def _log_sigmoid(x):
    return jnp.minimum(x, 0.0) - jnp.log(1.0 + jnp.exp(-jnp.abs(x)))


def _ml_chunk(d, hd, q_ref, k_ref, v_ref, g, gt, o_ref, c_ref, n_ref, m_ref):
    lc = ML_CHUNK
    ng = 4 * ML_HEADS
    dh = ML_HEAD_DIM
    hs = slice(hd * dh, (hd + 1) * dh)
    col_i = 2 * ML_HEADS * d + hd
    col_f = col_i + ML_HEADS
    lane = lax.broadcasted_iota(jnp.int32, (lc, ng), 1)
    i_col = jnp.sum(jnp.where(lane == col_i, g, 0.0), axis=-1, keepdims=True)
    f_col = _log_sigmoid(jnp.sum(jnp.where(lane == col_f, g, 0.0), axis=-1, keepdims=True))
    sub = lax.broadcasted_iota(jnp.int32, (ng, lc), 0)
    i_row = jnp.sum(jnp.where(sub == col_i, gt, 0.0), axis=0, keepdims=True)
    f_row = _log_sigmoid(jnp.sum(jnp.where(sub == col_f, gt, 0.0), axis=0, keepdims=True))

    t_idx = lax.broadcasted_iota(jnp.int32, (lc, lc), 0)
    s_idx = lax.broadcasted_iota(jnp.int32, (lc, lc), 1)
    if d == 0:
        causal = s_idx <= t_idx
        causal_t = t_idx <= s_idx
    else:
        causal = s_idx >= t_idx
        causal_t = t_idx >= s_idx
    b_col = jnp.sum(jnp.where(causal, f_row, 0.0), axis=-1, keepdims=True)
    b_row = jnp.sum(jnp.where(causal_t, f_col, 0.0), axis=0, keepdims=True)
    m_prev = m_ref[d, hd]
    dmat = jnp.where(causal, b_col - b_row + i_row, -jnp.inf)
    inter = b_col + m_prev
    m_out = jnp.maximum(inter, jnp.max(dmat, axis=-1, keepdims=True))
    w_intra = jnp.exp(dmat - m_out)
    w_state = jnp.exp(inter - m_out)

    q = q_ref[:, hs]
    k = k_ref[:, hs]
    v = v_ref[:, hs]
    cst = c_ref[d, hd]
    s = lax.dot_general(q, k, (((1,), (1,)), ((), ())), preferred_element_type=F32) * w_intra
    num = jnp.dot(s.astype(BF16), v, preferred_element_type=F32)
    num = num + w_state * jnp.dot(q, cst.astype(BF16), preferred_element_type=F32)
    qn = jnp.sum(q.astype(F32) * n_ref[d, hd], axis=-1, keepdims=True)
    den = jnp.sum(s, axis=-1, keepdims=True) + w_state * qn
    o_ref[:, hs] = num / jnp.maximum(jnp.abs(den), jnp.exp(-m_out))

    last = lc - 1 if d == 0 else 0
    bl = jnp.sum(f_row, axis=-1, keepdims=True)
    m_new = m_out[last:last + 1]
    wk = jnp.exp(bl - b_col + i_col - m_new)
    decay = jnp.exp(bl + m_prev - m_new)
    vw = (v.astype(F32) * wk).astype(BF16)
    c_ref[d, hd] = decay * cst + lax.dot_general(
        k, vw, (((0,), (0,)), ((), ())), preferred_element_type=F32)
    n_ref[d, hd] = decay * n_ref[d, hd] + jnp.sum(k.astype(F32) * wk, axis=0, keepdims=True)
    m_ref[d, hd] = m_new


def _ml_scan_kernel(qf_ref, kf_ref, vf_ref, gf_ref, gtf_ref, qb_ref, kb_ref, vb_ref, gb_ref, gtb_ref,
                    of_ref, ob_ref, c_ref, n_ref, m_ref):
    @pl.when(pl.program_id(1) == 0)
    def _():
        c_ref[...] = jnp.zeros_like(c_ref)
        n_ref[...] = jnp.zeros_like(n_ref)
        m_ref[...] = jnp.zeros_like(m_ref)

    dirs = ((qf_ref, kf_ref, vf_ref, gf_ref, gtf_ref, of_ref),
            (qb_ref, kb_ref, vb_ref, gb_ref, gtb_ref, ob_ref))
    for d, (q_ref, k_ref, v_ref, g_ref, gt_ref, o_ref) in enumerate(dirs):
        g = g_ref[...]
        gt = gt_ref[...]
        for hd in range(ML_HEADS):
            _ml_chunk(d, hd, q_ref, k_ref, v_ref, g, gt, o_ref, c_ref, n_ref, m_ref)


def _ml_scan(q, k, v, gates, gates_t, b, l):
    t = b * l
    nc = l // ML_CHUNK
    dh = ML_HEAD_DIM
    e = D_INNER
    ng = 4 * ML_HEADS
    fw = lambda bb, c: bb * nc + c
    bw = lambda bb, c: bb * nc + nc - 1 - c

    def specs(ch):
        qspec = pl.BlockSpec((ML_CHUNK, e), lambda bb, c: (ch(bb, c), 0))
        return [qspec, qspec, qspec,
                pl.BlockSpec((ML_CHUNK, ng), lambda bb, c: (ch(bb, c), 0)),
                pl.BlockSpec((ng, ML_CHUNK), lambda bb, c: (0, ch(bb, c)))]

    out = jax.ShapeDtypeStruct((t, e), F32)
    return pl.pallas_call(
        _ml_scan_kernel,
        grid=(b, nc),
        in_specs=specs(fw) + specs(bw),
        out_specs=[pl.BlockSpec((ML_CHUNK, e), lambda bb, c: (fw(bb, c), 0)),
                   pl.BlockSpec((ML_CHUNK, e), lambda bb, c: (bw(bb, c), 0))],
        out_shape=[out, out],
        scratch_shapes=[pltpu.VMEM((2, ML_HEADS, dh, dh), F32), pltpu.VMEM((2, ML_HEADS, 1, dh), F32),
                        pltpu.VMEM((2, ML_HEADS, 1, 1), F32)],
        compiler_params=_cparams(("arbitrary", "arbitrary")),
        name="mlstm_scan",
    )(q, k, v, gates, gates_t, q, k, v, gates, gates_t)


def _mlstm_layer(x, norm_w, w_in, conv_w, conv_b, wq, wk, wv, w_gate, b_gate, head_norm, skip,
                 w_out, final_norm=None):
    b, l, _ = x.shape
    x2d = x.reshape(b * l, D_MODEL)
    proj, xc = _norm_matmul_conv(x2d, norm_w, w_in, conv_w, conv_b, l, act=True, dual=True, tm=1024)
    q, k, v, gates, gates_t = _ml_qkv(xc, proj, wq, wk, wv, w_gate, b_gate)
    h_fwd, h_bwd = _ml_scan(q, k, v, gates, gates_t, b, l)
    out = _out_matmul([h_fwd, h_bwd, xc], proj, 1, w_out, x2d, final_norm,
                      extra=(head_norm, skip))
    return out.reshape(b, l, D_MODEL)


def _trunk(x, layers, final_norm):
    kinds = (_hyena_layer, _swa_layer, _mlstm_layer, _hyena_layer)
    for i, (fn, params) in enumerate(zip(kinds, layers)):
        x = fn(x, *params, final_norm=final_norm if i == len(kinds) - 1 else None)
    return x


def kernel(x_prompt, x_sample, l0_norm, l0_hy_w_in, l0_hy_conv_w, l0_hy_conv_b, l0_hy_filt_w1, l0_hy_filt_b1, l0_hy_filt_w2, l0_hy_filt_b2, l0_hy_filt_w3, l0_hy_filt_b3, l0_hy_filt_freq, l0_hy_filt_wout, l0_hy_bias_d, l0_hy_w_out, l1_norm, l1_swa_w_in, l1_swa_sink, l1_swa_w_out, l2_norm, l2_ml_w_in, l2_ml_conv_w, l2_ml_conv_b, l2_ml_wq, l2_ml_wk, l2_ml_wv, l2_ml_w_gate, l2_ml_b_gate, l2_ml_head_norm, l2_ml_skip, l2_ml_w_out, l3_norm, l3_hy_w_in, l3_hy_conv_w, l3_hy_conv_b, l3_hy_filt_w1, l3_hy_filt_b1, l3_hy_filt_w2, l3_hy_filt_b2, l3_hy_filt_w3, l3_hy_filt_b3, l3_hy_filt_freq, l3_hy_filt_wout, l3_hy_bias_d, l3_hy_w_out, final_norm):
    layers = [
        (l0_norm, l0_hy_w_in, l0_hy_conv_w, l0_hy_conv_b, l0_hy_filt_w1, l0_hy_filt_b1, l0_hy_filt_w2,
         l0_hy_filt_b2, l0_hy_filt_w3, l0_hy_filt_b3, l0_hy_filt_freq, l0_hy_filt_wout, l0_hy_bias_d,
         l0_hy_w_out),
        (l1_norm, l1_swa_w_in, l1_swa_sink, l1_swa_w_out),
        (l2_norm, l2_ml_w_in, l2_ml_conv_w, l2_ml_conv_b, l2_ml_wq, l2_ml_wk, l2_ml_wv, l2_ml_w_gate,
         l2_ml_b_gate, l2_ml_head_norm, l2_ml_skip, l2_ml_w_out),
        (l3_norm, l3_hy_w_in, l3_hy_conv_w, l3_hy_conv_b, l3_hy_filt_w1, l3_hy_filt_b1, l3_hy_filt_w2,
         l3_hy_filt_b2, l3_hy_filt_w3, l3_hy_filt_b3, l3_hy_filt_freq, l3_hy_filt_wout, l3_hy_bias_d,
         l3_hy_w_out),
    ]
    return (_trunk(x_prompt, layers, final_norm), _trunk(x_sample, layers, final_norm))
```
